```python
import jax, jax.numpy as jnp
from jax import lax
import numpy as np

D_MODEL = 1024
BATCH = 16
SEQ = 256
DEPTH = 2
DEC_BATCH = 2
DEC_SEQ = 4096
PAST_LEN = 512

GRID_W = 64
BLOCK = 128
M_HEADS = 4
M_DH = 64
M_W = M_HEADS * M_DH
A_HEADS = 8
A_KV_HEADS = 2
A_GROUP = A_HEADS // A_KV_HEADS
A_DH = 64
A_W = A_HEADS * A_DH
A_KV_W = A_KV_HEADS * A_DH
WINDOW = 128
ROPE_BASE = 10000.0
C_W = 256
CONV_K = 31
R_HEADS = 4
R_DH = 64
R_W = R_HEADS * R_DH
D_FF = 4 * D_MODEL
N_BRANCH = 4
EPS = 1e-6
IN_SIZES = (M_W, M_W, M_W, M_W, 4 * M_HEADS, A_W, A_KV_W, A_KV_W, 2 * C_W, R_W, R_W, R_W, R_W, N_BRANCH * D_MODEL)
IN_W = sum(IN_SIZES)
GATE_OFF = 4 * M_W
BR_SIZES = (M_W, A_W, C_W, R_W)
BR_W = sum(BR_SIZES)

kernel_name = 'hybrid_diffusion_prefix_trunk_step'

f32 = jnp.float32


def _split(a, sizes, axis=-1):
    idx = np.cumsum(sizes)[:-1].tolist()
    return jnp.split(a, idx, axis=axis)


def _flip(a):
    return jnp.flip(a, axis=1)


def rmsnorm(x, g):
    xf = x.astype(f32)
    y = xf * lax.rsqrt(jnp.mean(xf * xf, axis=-1, keepdims=True) + EPS)
    return (y * g.astype(f32)).astype(x.dtype)


def layernorm(x, g, b):
    xf = x.astype(f32)
    mu = jnp.mean(xf, axis=-1, keepdims=True)
    var = jnp.mean(jnp.square(xf - mu), axis=-1, keepdims=True)
    y = (xf - mu) * lax.rsqrt(var + EPS)
    return (y * g.astype(f32) + b.astype(f32)).astype(x.dtype)


def _rope_half(x, pos, freqs):
    ang = pos[:, None] * freqs[None, :]
    cos = jnp.cos(ang)[:, None, :]
    sin = jnp.sin(ang)[:, None, :]
    x1, x2 = jnp.split(x, 2, axis=-1)
    return jnp.concatenate([x1 * cos - x2 * sin, x1 * sin + x2 * cos], axis=-1)


def axial_rope(x):
    n_tok = x.shape[1]
    rows = n_tok // GRID_W
    row = jnp.repeat(jnp.arange(rows, dtype=f32), GRID_W)
    col = jnp.tile(jnp.arange(GRID_W, dtype=f32), rows)
    half = A_DH // 2
    freqs = jnp.power(ROPE_BASE, -jnp.arange(half // 2, dtype=f32) / (half // 2))
    xf = x.astype(f32)
    y = jnp.concatenate([_rope_half(xf[..., :half], row, freqs), _rope_half(xf[..., half:], col, freqs)], axis=-1)
    return y.astype(x.dtype)


def _to_chunks(a):
    B, N = a.shape[:2]
    return jnp.moveaxis(a.astype(f32).reshape(B, N // BLOCK, BLOCK, *a.shape[2:]), 1, 0)


def _from_chunks(a):
    nc, B = a.shape[:2]
    return jnp.moveaxis(a, 0, 1).reshape(B, nc * BLOCK, *a.shape[3:])


def mlstm_scan(q, k, v, li, lf, state):
    causal = jnp.tril(jnp.ones((BLOCK, BLOCK), bool))

    def step(carry, xc):
        C, n, m = carry
        qc, kc, vc, lic, lfc = xc
        b = jnp.cumsum(lfc, axis=1)
        logD = b[:, :, None, :] - b[:, None, :, :] + lic[:, None, :, :]
        logD = jnp.where(causal[None, :, :, None], logD, -jnp.inf)
        inter = b + m[:, None, :]
        m_t = jnp.maximum(inter, jnp.max(logD, axis=2))
        s = jnp.einsum('bthd,bshd->btsh', qc, kc) * jnp.exp(logD - m_t[:, :, None, :])
        w_inter = jnp.exp(inter - m_t)
        num = jnp.einsum('btsh,bshd->bthd', s, vc) + w_inter[..., None] * jnp.einsum('bthk,bhkv->bthv', qc, C)
        den = jnp.sum(s, axis=2) + w_inter * jnp.einsum('bthk,bhk->bth', qc, n)
        h = num / jnp.maximum(jnp.abs(den), jnp.exp(-m_t))[..., None]
        bL = b[:, -1]
        logw = bL[:, None, :] - b + lic
        m_new = jnp.maximum(bL + m, jnp.max(logw, axis=1))
        w = jnp.exp(logw - m_new[:, None, :])
        decay = jnp.exp(bL + m - m_new)
        C_new = decay[..., None, None] * C + jnp.einsum('bsh,bshk,bshv->bhkv', w, kc, vc)
        n_new = decay[..., None] * n + jnp.einsum('bsh,bshk->bhk', w, kc)
        return (C_new, n_new, m_new), h

    xs = tuple(_to_chunks(a) for a in (q, k, v, li, lf))
    state = tuple(s.astype(f32) for s in state)
    final, hs = lax.scan(step, state, xs)
    return _from_chunks(hs), final


def retention_scan(q, k, v, log_gamma, S):
    pos = jnp.arange(BLOCK, dtype=f32)
    rel = pos[:, None] - pos[None, :]
    Dmat = jnp.where((rel >= 0)[..., None], jnp.exp(jnp.maximum(rel, 0.0)[..., None] * log_gamma), 0.0)
    q_dec = jnp.exp((pos + 1.0)[:, None] * log_gamma)
    k_dec = jnp.exp((BLOCK - 1.0 - pos)[:, None] * log_gamma)
    chunk_dec = jnp.exp(BLOCK * log_gamma)

    def step(S, xc):
        qc, kc, vc = xc
        s = jnp.einsum('bthd,bshd->btsh', qc, kc) * Dmat[None]
        o = jnp.einsum('btsh,bshv->bthv', s, vc) + q_dec[None, :, :, None] * jnp.einsum('bthk,bhkv->bthv', qc, S)
        S_new = chunk_dec[None, :, None, None] * S + jnp.einsum('sh,bshk,bshv->bhkv', k_dec, kc, vc)
        return S_new, o

    xs = tuple(_to_chunks(a) for a in (q, k, v))
    S_fin, os_ = lax.scan(step, S.astype(f32), xs)
    return _from_chunks(os_), S_fin


def sink_attend(s, vals, sink):
    sk = sink.astype(f32)[None, :, :, None, None]
    m = jnp.maximum(jnp.max(s, axis=-1, keepdims=True), sk)
    p = jnp.exp(s - m)
    p = p / (jnp.sum(p, axis=-1, keepdims=True) + jnp.exp(sk - m))
    return jnp.einsum('bkgqs,bskd->bqkgd', p, vals.astype(f32))


def _q_blocks(q):
    B, N = q.shape[:2]
    return jnp.moveaxis(q.reshape(B, N // BLOCK, BLOCK, A_KV_HEADS, A_GROUP, A_DH), 1, 0)


def _from_q_blocks(ob, dtype):
    nb, B = ob.shape[:2]
    return jnp.moveaxis(ob, 0, 1).reshape(B, nb * BLOCK, A_W).astype(dtype)


def context_attention(q, k, v, sink):
    scale = A_DH ** -0.5
    kf = k.astype(f32)

    def one(qblk):
        s = jnp.einsum('bqkgd,bskd->bkgqs', qblk.astype(f32), kf) * scale
        return sink_attend(s, v, sink)

    return _from_q_blocks(lax.map(one, _q_blocks(q)), q.dtype)


def latent_attention(q, k, v, ctx_k, ctx_v, sink):
    B, N = q.shape[:2]
    nb = N // BLOCK
    n_ctx = ctx_k.shape[1]
    scale = A_DH ** -0.5

    def band(a):
        ap = jnp.pad(a, ((0, 0), (BLOCK, BLOCK), (0, 0), (0, 0))).reshape(B, nb + 2, BLOCK, A_KV_HEADS, A_DH)
        return jnp.moveaxis(jnp.concatenate([ap[:, :-2], ap[:, 1:-1], ap[:, 2:]], axis=2), 1, 0)

    qi = jnp.arange(BLOCK)[:, None]
    kj = jnp.arange(3 * BLOCK)[None, :]
    in_win = jnp.abs(qi + BLOCK - kj) <= WINDOW
    ctx_ok = jnp.ones((BLOCK, n_ctx), bool)
    ck = ctx_k.astype(f32)
    cv = ctx_v.astype(f32)

    def one(args):
        qblk, kb, vb, bi = args
        j = bi * BLOCK - BLOCK + kj
        mask = jnp.concatenate([ctx_ok, in_win & (j >= 0) & (j < N)], axis=1)
        keys = jnp.concatenate([ck, kb.astype(f32)], axis=1)
        vals = jnp.concatenate([cv, vb.astype(f32)], axis=1)
        s = jnp.einsum('bqkgd,bskd->bkgqs', qblk.astype(f32), keys) * scale
        s = jnp.where(mask, s, -jnp.inf)
        return sink_attend(s, vals, sink)

    ob = lax.map(one, (_q_blocks(q), band(k), band(v), jnp.arange(nb)))
    return _from_q_blocks(ob, q.dtype)


def conformer_conv(u, w, b, ln_g, ln_b):
    a, g = jnp.split(u, 2, axis=-1)
    z = a * jax.nn.sigmoid(g)
    z = lax.conv_general_dilated(z, w.astype(z.dtype)[:, None, :], window_strides=(1,),
                                 padding=[(CONV_K // 2, CONV_K // 2)],
                                 dimension_numbers=('NWC', 'WIO', 'NWC'), feature_group_count=C_W)
    z = layernorm(z + b, ln_g, ln_b)
    return jax.nn.silu(z)


def token_mixer(h, lw, cache):
    B, N, _ = h.shape
    is_ctx = cache is None
    proj = h @ lw['w_in'] + lw['b_in']
    (mq, mk, mv, mo, mg, aq, ak, av, cu, rq, rk, rv, rg, mrg) = _split(proj, IN_SIZES)

    q = mq.reshape(B, N, M_HEADS, M_DH)
    k = mk.reshape(B, N, M_HEADS, M_DH) * (M_DH ** -0.5)
    v = mv.reshape(B, N, M_HEADS, M_DH)
    g = mg.reshape(B, N, 4, M_HEADS).astype(f32)
    if is_ctx:
        C0 = jnp.zeros((B, 2, M_HEADS, M_DH, M_DH), f32)
        n0 = jnp.zeros((B, 2, M_HEADS, M_DH), f32)
        m0 = jnp.zeros((B, 2, M_HEADS), f32)
    else:
        C0, n0, m0 = cache['C'], cache['n'], cache['m']
    h_f, (Cf, nf, mf) = mlstm_scan(q, k, v, g[:, :, 0], jax.nn.log_sigmoid(g[:, :, 1]),
                                   (C0[:, 0], n0[:, 0], m0[:, 0]))
    h_b, (Cb, nb_, mb) = mlstm_scan(_flip(q), _flip(k), _flip(v), _flip(g[:, :, 2]),
                                    _flip(jax.nn.log_sigmoid(g[:, :, 3])), (C0[:, 1], n0[:, 1], m0[:, 1]))
    y_m = jax.nn.sigmoid(mo) * (h_f + _flip(h_b)).reshape(B, N, M_W).astype(h.dtype)

    qa = aq.reshape(B, N, A_HEADS, A_DH)
    ka = ak.reshape(B, N, A_KV_HEADS, A_DH)
    va = av.reshape(B, N, A_KV_HEADS, A_DH)
    sink = lw['sink'].reshape(A_KV_HEADS, A_GROUP)
    if is_ctx:
        y_a = context_attention(qa, ka, va, sink)
    else:
        y_a = latent_attention(axial_rope(qa), axial_rope(ka), va, cache['k'], cache['v'], sink)

    y_c = conformer_conv(cu, lw['conv_w'], lw['conv_b'], lw['conv_ln_g'], lw['conv_ln_b'])

    qr = rq.reshape(B, N, R_HEADS, R_DH)
    kr = rk.reshape(B, N, R_HEADS, R_DH) * (R_DH ** -0.5)
    vr = rv.reshape(B, N, R_HEADS, R_DH)
    lg = jax.nn.log_sigmoid(lw['ret_decay'].astype(f32))
    S0 = jnp.zeros((B, 2, R_HEADS, R_DH, R_DH), f32) if is_ctx else cache['S']
    o_f, Sf = retention_scan(qr, kr, vr, lg[0], S0[:, 0])
    o_b, Sb = retention_scan(_flip(qr), _flip(kr), _flip(vr), lg[1], S0[:, 1])
    o = o_f + _flip(o_b)
    mu = jnp.mean(o, axis=-1, keepdims=True)
    var = jnp.mean(jnp.square(o - mu), axis=-1, keepdims=True)
    o = (o - mu) * lax.rsqrt(var + EPS)
    y_r = jax.nn.silu(rg) * o.reshape(B, N, R_W).astype(h.dtype)

    gates = jax.nn.sigmoid(mrg).reshape(B, N, N_BRANCH, D_MODEL)
    w_br = _split(lw['w_branch'], BR_SIZES, axis=0)
    merged = gates[:, :, 0] * (y_m @ w_br[0])
    merged = merged + gates[:, :, 1] * (y_a @ w_br[1])
    merged = merged + gates[:, :, 2] * (y_c @ w_br[2])
    merged = merged + gates[:, :, 3] * (y_r @ w_br[3])
    out = merged @ lw['w_out']
    if is_ctx:
        ctx = (ka, va, jnp.stack([Cf, Cb], axis=1), jnp.stack([nf, nb_], axis=1),
               jnp.stack([mf, mb], axis=1), jnp.stack([Sf, Sb], axis=1))
        return out, ctx
    return out, None


def trunk_layer(x, mod, lw, cache):
    sh1, sc1, gt1, sh2, sc2, gt2 = jnp.split(mod, 6, axis=-1)
    ng = lw['norm_g']
    h = rmsnorm(x, ng[0]) * (1 + sc1) + sh1
    mix, ctx = token_mixer(h, lw, cache)
    x = x + gt1 * rmsnorm(mix, ng[1])
    h = rmsnorm(x, ng[2]) * (1 + sc2) + sh2
    f = jnp.square(jax.nn.relu(h @ lw['w_mlp1'])) @ lw['w_mlp2']
    x = x + gt2 * rmsnorm(f, ng[3])
    return x, ctx


def setup_inputs(seed: int = 0) -> dict:
    key = jax.random.key(seed)
    ks = jax.random.split(key, 26)

    def nrm(k, shape, s):
        return jax.random.normal(k, shape, f32) * s

    gate_bias = jnp.zeros((4, M_HEADS), f32)
    gate_bias = gate_bias.at[1].set(jnp.linspace(3.0, 6.0, M_HEADS)).at[3].set(jnp.linspace(3.0, 6.0, M_HEADS))
    b_in = nrm(ks[14], (DEPTH, IN_W), 0.01).at[:, GATE_OFF:GATE_OFF + 4 * M_HEADS].add(gate_bias.reshape(-1))
    br_scale = jnp.concatenate([jnp.full((w,), w ** -0.5, f32) for w in BR_SIZES])
    eps = jnp.power(2.0, -5.0 - jnp.arange(R_HEADS, dtype=f32))
    return {
        'x_prompt': nrm(ks[0], (BATCH, SEQ, D_MODEL), 1.0),
        'x_sample': nrm(ks[1], (DEC_BATCH, DEC_SEQ, D_MODEL), 1.0),
        'c': nrm(ks[2], (DEC_BATCH, D_MODEL), 1.0),
        'cache_k': nrm(ks[3], (DEC_BATCH, DEPTH, PAST_LEN, A_KV_HEADS, A_DH), 1.0),
        'cache_v': nrm(ks[4], (DEC_BATCH, DEPTH, PAST_LEN, A_KV_HEADS, A_DH), 1.0),
        'state_mlstm_C': nrm(ks[5], (DEC_BATCH, DEPTH, 2, M_HEADS, M_DH, M_DH), 0.1),
        'state_mlstm_n': nrm(ks[6], (DEC_BATCH, DEPTH, 2, M_HEADS, M_DH), 0.1),
        'state_mlstm_m': nrm(ks[7], (DEC_BATCH, DEPTH, 2, M_HEADS), 1.0),
        'state_ret': nrm(ks[8], (DEC_BATCH, DEPTH, 2, R_HEADS, R_DH, R_DH), 1.0),
        'c_ctx': nrm(ks[9], (D_MODEL,), 1.0),
        'w_ada': nrm(ks[10], (DEPTH, D_MODEL, 6 * D_MODEL), 0.5 * D_MODEL ** -0.5),
        'b_ada': nrm(ks[11], (DEPTH, 6 * D_MODEL), 0.02),
        'norm_g': 1.0 + nrm(ks[12], (DEPTH, 4, D_MODEL), 0.02),
        'w_in': nrm(ks[13], (DEPTH, D_MODEL, IN_W), D_MODEL ** -0.5),
        'b_in': b_in,
        'w_branch': jax.random.normal(ks[15], (DEPTH, BR_W, D_MODEL), f32) * br_scale[None, :, None],
        'w_out': nrm(ks[16], (DEPTH, D_MODEL, D_MODEL), D_MODEL ** -0.5),
        'attn_sink': nrm(ks[17], (DEPTH, A_HEADS), 0.5),
        'ret_decay_logit': jnp.log((1.0 - eps) / eps)[None, None, :] + nrm(ks[18], (DEPTH, 2, R_HEADS), 0.1),
        'conv_w': nrm(ks[19], (DEPTH, CONV_K, C_W), CONV_K ** -0.5),
        'conv_b': nrm(ks[20], (DEPTH, C_W), 0.02),
        'conv_ln_g': 1.0 + nrm(ks[21], (DEPTH, C_W), 0.02),
        'conv_ln_b': nrm(ks[22], (DEPTH, C_W), 0.02),
        'w_mlp1': nrm(ks[23], (DEPTH, D_MODEL, D_FF), D_MODEL ** -0.5),
        'w_mlp2': nrm(ks[24], (DEPTH, D_FF, D_MODEL), D_FF ** -0.5),
    }


def reference(x_prompt, x_sample, c, cache_k, cache_v, state_mlstm_C, state_mlstm_n, state_mlstm_m, state_ret,
              c_ctx, w_ada, b_ada, norm_g, w_in, b_in, w_branch, w_out, attn_sink, ret_decay_logit,
              conv_w, conv_b, conv_ln_g, conv_ln_b, w_mlp1, w_mlp2):
    def layer_weights(l):
        return {'norm_g': norm_g[l], 'w_in': w_in[l], 'b_in': b_in[l], 'w_branch': w_branch[l],
                'w_out': w_out[l], 'sink': attn_sink[l], 'ret_decay': ret_decay_logit[l],
                'conv_w': conv_w[l], 'conv_b': conv_b[l], 'conv_ln_g': conv_ln_g[l], 'conv_ln_b': conv_ln_b[l],
                'w_mlp1': w_mlp1[l], 'w_mlp2': w_mlp2[l]}

    xp = x_prompt
    ks_, vs_, Cs, ns, ms, Ss = [], [], [], [], [], []
    for l in range(DEPTH):
        mod = (jax.nn.silu(c_ctx) @ w_ada[l] + b_ada[l])[None, None, :]
        xp, ctx = trunk_layer(xp, mod, layer_weights(l), None)
        ks_.append(ctx[0]); vs_.append(ctx[1]); Cs.append(ctx[2]); ns.append(ctx[3]); ms.append(ctx[4]); Ss.append(ctx[5])

    xs = x_sample
    for l in range(DEPTH):
        mod = (jax.nn.silu(c) @ w_ada[l] + b_ada[l])[:, None, :]
        cache_l = {'k': cache_k[:, l], 'v': cache_v[:, l], 'C': state_mlstm_C[:, l], 'n': state_mlstm_n[:, l],
                   'm': state_mlstm_m[:, l], 'S': state_ret[:, l]}
        xs, _ = trunk_layer(xs, mod, layer_weights(l), cache_l)

    return (xp, xs, jnp.stack(ks_, axis=1), jnp.stack(vs_, axis=1), jnp.stack(Cs, axis=1),
            jnp.stack(ns, axis=1), jnp.stack(ms, axis=1), jnp.stack(Ss, axis=1))
```

```python
import functools

import numpy as np
import jax
import jax.numpy as jnp
from jax import lax
from jax.experimental import pallas as pl
from jax.experimental.pallas import tpu as pltpu

f32 = jnp.float32
bf16 = jnp.bfloat16

D_MODEL = 1024
DEPTH = 2
GRID_W = 64
BLOCK = 128
N_HEADS = 4
DH = 64
HW = N_HEADS * DH
A_HEADS = 8
A_KV_HEADS = 2
A_GROUP = A_HEADS // A_KV_HEADS
A_W = A_HEADS * DH
A_KV_W = A_KV_HEADS * DH
WINDOW = 128
ROPE_BASE = 10000.0
CONV_K = 31
D_FF = 4 * D_MODEL
N_BRANCH = 4
EPS = 1e-6
NEG = -1e30
LANES = 128
SUBLANES = 8

N_GATES = 4 * N_HEADS
PK_MQ = 0
PK_MG = PK_MQ + 4 * HW
PK_AT = PK_MG + LANES
PK_CV = PK_AT + A_W + 2 * A_KV_W
PK_RT = PK_CV + 2 * HW
PK_GT = PK_RT + 4 * HW
PK_W = PK_GT + N_BRANCH * D_MODEL
BR_OFFS = (0, HW, HW + A_W, 2 * HW + A_W, 3 * HW + A_W)

VMEM_LIMIT = 56 * 1024 * 1024
ROW_TILE = 512


def _sigmoid(x):
    return 0.5 * (jnp.tanh(0.5 * x) + 1.0)


def _log_sigmoid(x):
    return jnp.minimum(x, 0.0) - jnp.log1p(jnp.exp(-jnp.abs(x)))


def _dot(a, b):
    return jnp.dot(a, b, preferred_element_type=f32)


def _dot_nt(a, b):
    return lax.dot_general(a, b, (((1,), (1,)), ((), ())), preferred_element_type=f32)


def _split3(x):
    x1 = x.astype(bf16)
    r1 = x - x1.astype(f32)
    x2 = r1.astype(bf16)
    x3 = (r1 - x2.astype(f32)).astype(bf16)
    return x1, x2, x3


def _dot3_r(t, x):
    x1, x2, x3 = _split3(x)
    return _dot(t, x1) + _dot(t, x2) + _dot(t, x3)


def _dot3_l(x, t):
    x1, x2, x3 = _split3(x)
    return _dot(x1, t) + _dot(x2, t) + _dot(x3, t)


def _const_spec(shape):
    nd = len(shape)
    return pl.BlockSpec(shape, lambda *_: (0,) * nd, pipeline_mode=pl.Buffered(1))


def _params(n_axes):
    return pltpu.CompilerParams(dimension_semantics=("arbitrary",) * n_axes, vmem_limit_bytes=VMEM_LIMIT)


def _ada_kernel(c_ref, w_ref, b_ref, o_ref):
    c = c_ref[...]
    s = (c * _sigmoid(c)).astype(bf16)
    o_ref[...] = _dot(s, w_ref[...].astype(bf16)) + b_ref[...]


def _ada(c_rows, w_ada, b_ada):
    tn = 1536
    n_out = 6 * D_MODEL
    return pl.pallas_call(
        _ada_kernel,
        grid=(DEPTH, n_out // tn),
        in_specs=[
            pl.BlockSpec((SUBLANES, D_MODEL), lambda l, j: (0, 0)),
            pl.BlockSpec((None, D_MODEL, tn), lambda l, j: (l, 0, j)),
            pl.BlockSpec((None, 1, tn), lambda l, j: (l, 0, j)),
        ],
        out_specs=pl.BlockSpec((None, SUBLANES, tn), lambda l, j: (l, 0, j)),
        out_shape=jax.ShapeDtypeStruct((DEPTH, SUBLANES, n_out), f32),
        compiler_params=_params(2),
        name="ada",
    )(c_rows, w_ada, b_ada.reshape(DEPTH, 1, n_out))


def _inproj_kernel(x_ref, mod_ref, ng_ref, w_ref, b_ref, mq_ref, mg_ref, at_ref, cv_ref, rt_ref, gt_ref):
    x = x_ref[...]
    y = x * lax.rsqrt(jnp.mean(x * x, axis=-1, keepdims=True) + EPS) * ng_ref[0:1, :]
    h = (y * (1.0 + mod_ref[1:2, :]) + mod_ref[0:1, :]).astype(bf16)
    chunk = 512
    for ref, base in ((mq_ref, PK_MQ), (mg_ref, PK_MG), (at_ref, PK_AT), (cv_ref, PK_CV), (rt_ref, PK_RT),
                      (gt_ref, PK_GT)):
        total = ref.shape[-1]
        for c in range(0, total, chunk):
            w = min(chunk, total - c)
            r = _dot(h, w_ref[:, base + c:base + c + w]) + b_ref[:, base + c:base + c + w]
            ref[:, c:c + w] = r.astype(ref.dtype)


def _inproj(x2d, mod4, ng, w_pk, b_pk, layer, row_of_tile):
    m = x2d.shape[0]
    tm = ROW_TILE
    widths = (4 * HW, LANES, A_W + 2 * A_KV_W, 2 * HW, 4 * HW, N_BRANCH * D_MODEL)
    dtypes = (bf16, f32, f32, bf16, bf16, bf16)
    return pl.pallas_call(
        _inproj_kernel,
        grid=(m // tm,),
        in_specs=[
            pl.BlockSpec((tm, D_MODEL), lambda i: (i, 0)),
            pl.BlockSpec((None, None, 6, D_MODEL), lambda i: (layer, row_of_tile(i), 0, 0)),
            _const_spec((4, D_MODEL)),
            _const_spec((D_MODEL, PK_W)),
            _const_spec((1, PK_W)),
        ],
        out_specs=[pl.BlockSpec((tm, w), lambda i: (i, 0)) for w in widths],
        out_shape=[jax.ShapeDtypeStruct((m, w), dt) for w, dt in zip(widths, dtypes)],
        compiler_params=_params(1),
        name="inproj",
    )(x2d, mod4, ng, w_pk, b_pk)


def _head_lane_masks():
    lane_head = lax.broadcasted_iota(jnp.int32, (1, HW), 1) // DH
    return [lane_head == h for h in range(N_HEADS)]


def _block_diag_mask():
    r = lax.broadcasted_iota(jnp.int32, (HW, HW), 0) // DH
    c = lax.broadcasted_iota(jnp.int32, (HW, HW), 1) // DH
    return r == c


def _expand_state(x):
    y = x + pltpu.roll(x, DH, 1)
    return jnp.where(_block_diag_mask(), jnp.concatenate([y, y], axis=1), 0.0)


def _compact_state(c):
    a = c[:, 0:LANES] + c[:, LANES:2 * LANES]
    return (a + pltpu.roll(a, DH, 1))[:, 0:DH]


def _mlstm_kernel(x_ref, g_ref, c0_ref, n0_ref, m0_ref, y_ref, cn_ref, nn_ref, mn_ref,
                  hf_s, hb_s, c_s, n_s, m_s, *, nc):
    row = lax.broadcasted_iota(jnp.int32, (BLOCK, BLOCK), 0)
    col = lax.broadcasted_iota(jnp.int32, (BLOCK, BLOCK), 1)
    tri = (row >= col, row <= col)
    tsum = tuple(jnp.where(t, 1.0, 0.0).astype(bf16) for t in tri)
    masks = _head_lane_masks()
    hsel = jnp.where(lax.broadcasted_iota(jnp.int32, (HW, LANES), 0) // DH ==
                     lax.broadcasted_iota(jnp.int32, (HW, LANES), 1), 1.0, 0.0).astype(bf16)
    row8 = lax.broadcasted_iota(jnp.int32, (SUBLANES, BLOCK), 0)

    for d in range(2):
        c_s[d] = _expand_state(c0_ref[d])
        n_s[d] = n0_ref[d]
        for h in range(N_HEADS):
            m_s[d, h] = m0_ref[d, h]

    def chunk(d, c):
        r0 = pl.multiple_of(c * BLOCK, BLOCK)
        blk = x_ref[pl.ds(r0, BLOCK), :]
        q = blk[:, 0:HW]
        k = blk[:, HW:2 * HW] * (DH ** -0.5)
        v = blk[:, 2 * HW:3 * HW]
        g = g_ref[pl.ds(r0, BLOCK), :]
        gt = g.T
        bc = _dot3_r(tsum[d], _log_sigmoid(g))
        bct = bc.T
        ktf = k.astype(f32).T
        kt = ktf.astype(bf16)
        cm = c_s[d]
        nrow = n_s[d]
        qc = _dot(q, cm.astype(bf16))
        qn = _dot((q.astype(f32) * nrow).astype(bf16), hsel)
        hacc = jnp.zeros((BLOCK, HW), f32)
        nnew = jnp.zeros((1, HW), f32)
        wr = jnp.zeros((SUBLANES, BLOCK), f32)
        kws, decs = [], []
        for h in range(N_HEADS):
            gi = 2 * N_HEADS * d + h
            gf = gi + N_HEADS
            b_col = bc[:, gf:gf + 1]
            b_row = bct[gf:gf + 1, :]
            li_row = gt[gi:gi + 1, :]
            m_h = m_s[d, h][:, 0:1]
            logd = jnp.where(tri[d], b_col - b_row + li_row, NEG)
            inter = b_col + m_h
            m_t = jnp.maximum(inter, jnp.max(logd, axis=1, keepdims=True))
            s_raw = _dot(jnp.where(masks[h], q, jnp.zeros_like(q)), kt)
            p = s_raw * jnp.exp(logd - m_t)
            w_int = jnp.exp(inter - m_t)
            den = jnp.sum(p, axis=1, keepdims=True) + w_int * qn[:, h:h + 1]
            num = _dot(p.astype(bf16), v) + w_int * qc
            inv = 1.0 / jnp.maximum(jnp.abs(den), jnp.exp(-m_t))
            hacc = jnp.where(masks[h], num * inv, hacc)
            b_last = b_row[:, BLOCK - 1:BLOCK] if d == 0 else b_row[:, 0:1]
            logw = b_last - b_row + li_row
            m_new = jnp.maximum(b_last + m_h, jnp.max(logw, axis=1, keepdims=True))
            w_row = jnp.exp(logw - m_new)
            decs.append(jnp.exp(b_last + m_h - m_new))
            kws.append(ktf[DH * h:DH * (h + 1), :] * w_row)
            wr = jnp.where(row8 == h, w_row, wr)
            m_s[d, h] = jnp.broadcast_to(m_new, (1, LANES))
        upd = _dot(jnp.concatenate(kws, axis=0).astype(bf16), v)
        wk = _dot(wr.astype(bf16), k)
        for h in range(N_HEADS):
            rs = slice(DH * h, DH * (h + 1))
            c_s[d, rs, :] = decs[h] * cm[rs, :] + jnp.where(masks[h], upd[rs, :], 0.0)
            nnew = jnp.where(masks[h], decs[h] * nrow + wk[h:h + 1, :], nnew)
        n_s[d] = nnew
        (hf_s, hb_s)[d][pl.ds(r0, BLOCK), :] = hacc

    def body(i, carry):
        chunk(0, i)
        chunk(1, nc - 1 - i)
        return carry

    lax.fori_loop(0, nc, body, 0)

    def fin(c, carry):
        r0 = pl.multiple_of(c * BLOCK, BLOCK)
        o = x_ref[pl.ds(r0, BLOCK), 3 * HW:4 * HW].astype(f32)
        hsum = hf_s[pl.ds(r0, BLOCK), :] + hb_s[pl.ds(r0, BLOCK), :]
        y_ref[pl.ds(r0, BLOCK), :] = (_sigmoid(o) * hsum).astype(y_ref.dtype)
        return carry

    lax.fori_loop(0, nc, fin, 0)

    for d in range(2):
        cn_ref[d] = _compact_state(c_s[d])
        nn_ref[d] = n_s[d]
        for h in range(N_HEADS):
            mn_ref[d, h] = m_s[d, h]


def _mlstm(mq, mg, c0, n0, m0, bsz, n):
    nc = n // BLOCK
    state_spec = lambda shape: pl.BlockSpec((None,) + shape, lambda b: (b,) + (0,) * len(shape))
    return pl.pallas_call(
        functools.partial(_mlstm_kernel, nc=nc),
        grid=(bsz,),
        in_specs=[
            pl.BlockSpec((None, n, 4 * HW), lambda b: (b, 0, 0)),
            pl.BlockSpec((None, n, LANES), lambda b: (b, 0, 0)),
            state_spec((2, HW, LANES)),
            state_spec((2, 1, HW)),
            state_spec((2, N_HEADS, 1, LANES)),
        ],
        out_specs=[
            pl.BlockSpec((None, n, HW), lambda b: (b, 0, 0)),
            state_spec((2, HW, DH)),
            state_spec((2, 1, HW)),
            state_spec((2, N_HEADS, 1, LANES)),
        ],
        out_shape=[
            jax.ShapeDtypeStruct((bsz, n, HW), bf16),
            jax.ShapeDtypeStruct((bsz, 2, HW, DH), f32),
            jax.ShapeDtypeStruct((bsz, 2, 1, HW), f32),
            jax.ShapeDtypeStruct((bsz, 2, N_HEADS, 1, LANES), f32),
        ],
        scratch_shapes=[
            pltpu.VMEM((n, HW), f32),
            pltpu.VMEM((n, HW), f32),
            pltpu.VMEM((2, HW, HW), f32),
            pltpu.VMEM((2, 1, HW), f32),
            pltpu.VMEM((2, N_HEADS, 1, LANES), f32),
        ],
        compiler_params=_params(1),
        name="mlstm",
    )(mq.reshape(bsz, n, 4 * HW), mg.reshape(bsz, n, LANES), c0, n0, m0)


def _ret_kernel(x_ref, s0_ref, dl_ref, dr_ref, y_ref, sn_ref, of_s, ob_s, s_s, qd_s, kd_s, cd_s, dm_s, *, nc):
    masks = _head_lane_masks()
    bdm = _block_diag_mask()
    rowf = lax.broadcasted_iota(jnp.int32, (BLOCK, HW), 0).astype(f32)
    rel = (lax.broadcasted_iota(jnp.int32, (BLOCK, BLOCK), 0) -
           lax.broadcasted_iota(jnp.int32, (BLOCK, BLOCK), 1)).astype(f32)
    avg = jnp.where(bdm, 1.0 / DH, 0.0).astype(bf16)

    for d in range(2):
        lg = _log_sigmoid(dl_ref[d])
        if d == 0:
            qd_s[d] = jnp.exp((rowf + 1.0) * lg)
            kd_s[d] = jnp.exp((BLOCK - 1.0 - rowf) * lg)
            reld = rel
        else:
            qd_s[d] = jnp.exp((BLOCK - rowf) * lg)
            kd_s[d] = jnp.exp(rowf * lg)
            reld = -rel
        cdh = jnp.exp(float(BLOCK) * _log_sigmoid(dr_ref[d]))
        cd_s[d] = jnp.concatenate([cdh, cdh], axis=1)
        for h in range(N_HEADS):
            lgh = lg[:, DH * h:DH * h + 1]
            dm_s[d, h] = jnp.where(reld >= 0.0, jnp.exp(jnp.maximum(reld, 0.0) * lgh), 0.0)
        s_s[d] = _expand_state(s0_ref[d])

    def chunk(d, c):
        r0 = pl.multiple_of(c * BLOCK, BLOCK)
        blk = x_ref[pl.ds(r0, BLOCK), :]
        q = blk[:, 0:HW]
        k = blk[:, HW:2 * HW] * (DH ** -0.5)
        v = blk[:, 2 * HW:3 * HW]
        kf = k.astype(f32)
        kt = kf.T.astype(bf16)
        sm = s_s[d]
        o = qd_s[d] * _dot(q, sm.astype(bf16))
        for h in range(N_HEADS):
            p = _dot(jnp.where(masks[h], q, jnp.zeros_like(q)), kt) * dm_s[d, h]
            o = o + jnp.where(masks[h], _dot(p.astype(bf16), v), 0.0)
        kdt = (kf * kd_s[d]).T.astype(bf16)
        s_s[d] = cd_s[d] * sm + jnp.where(bdm, _dot(kdt, v), 0.0)
        (of_s, ob_s)[d][pl.ds(r0, BLOCK), :] = o

    def body(i, carry):
        chunk(0, i)
        chunk(1, nc - 1 - i)
        return carry

    lax.fori_loop(0, nc, body, 0)

    def fin(c, carry):
        r0 = pl.multiple_of(c * BLOCK, BLOCK)
        gate = x_ref[pl.ds(r0, BLOCK), 3 * HW:4 * HW].astype(f32)
        o = of_s[pl.ds(r0, BLOCK), :] + ob_s[pl.ds(r0, BLOCK), :]
        dev = o - _dot3_l(o, avg)
        var = _dot3_l(dev * dev, avg)
        y = gate * _sigmoid(gate) * (dev * lax.rsqrt(var + EPS))
        y_ref[pl.ds(r0, BLOCK), :] = y.astype(y_ref.dtype)
        return carry

    lax.fori_loop(0, nc, fin, 0)

    for d in range(2):
        sn_ref[d] = _compact_state(s_s[d])


def _retention(rt, s0, dl, dr, bsz, n):
    nc = n // BLOCK
    return pl.pallas_call(
        functools.partial(_ret_kernel, nc=nc),
        grid=(bsz,),
        in_specs=[
            pl.BlockSpec((None, n, 4 * HW), lambda b: (b, 0, 0)),
            pl.BlockSpec((None, 2, HW, LANES), lambda b: (b, 0, 0, 0)),
            _const_spec((2, 1, HW)),
            _const_spec((2, HW, LANES)),
        ],
        out_specs=[
            pl.BlockSpec((None, n, HW), lambda b: (b, 0, 0)),
            pl.BlockSpec((None, 2, HW, DH), lambda b: (b, 0, 0, 0)),
        ],
        out_shape=[
            jax.ShapeDtypeStruct((bsz, n, HW), bf16),
            jax.ShapeDtypeStruct((bsz, 2, HW, DH), f32),
        ],
        scratch_shapes=[
            pltpu.VMEM((n, HW), f32),
            pltpu.VMEM((n, HW), f32),
            pltpu.VMEM((2, HW, HW), f32),
            pltpu.VMEM((2, BLOCK, HW), f32),
            pltpu.VMEM((2, BLOCK, HW), f32),
            pltpu.VMEM((2, HW, HW), f32),
            pltpu.VMEM((2, N_HEADS, BLOCK, BLOCK), f32),
        ],
        compiler_params=_params(1),
        name="retention",
    )(rt.reshape(bsz, n, 4 * HW), s0, dl, dr)


def _attn_kernel(q_ref, k_ref, v_ref, ck_ref, cv_ref, cos_ref, sin_ref, sk_ref, o_ref, kt_s, vt_s,
                 *, n_ctx, n, band):
    i = pl.program_id(1)
    lane = lax.broadcasted_iota(jnp.int32, (1, LANES), 1)
    lo = lane < DH
    first = (lane % (DH // 2)) < (DH // 4)
    gmasks = _head_lane_masks()

    def tile4(x):
        xr = pltpu.roll(x, DH, 1)
        a2 = jnp.where(lo, x, xr).astype(bf16)
        b2 = jnp.where(lo, xr, x).astype(bf16)
        return jnp.concatenate([a2, a2], axis=1), jnp.concatenate([b2, b2], axis=1)

    def rope(x, cos, sin):
        sw = jnp.where(first, pltpu.roll(x, LANES - DH // 4, 1), pltpu.roll(x, DH // 4, 1))
        return x * cos + sw * sin

    @pl.when(i == 0)
    def _prepare_keys():
        for r in range(0, n_ctx, BLOCK):
            k0, k1 = tile4(ck_ref[r:r + BLOCK, :])
            v0, v1 = tile4(cv_ref[r:r + BLOCK, :])
            kt_s[0, r:r + BLOCK, :] = k0
            kt_s[1, r:r + BLOCK, :] = k1
            vt_s[0, r:r + BLOCK, :] = v0
            vt_s[1, r:r + BLOCK, :] = v1
        if band:
            zeros = jnp.zeros((BLOCK, HW), bf16)
            for j in range(A_KV_HEADS):
                for s in (kt_s, vt_s):
                    s[j, n_ctx:n_ctx + BLOCK, :] = zeros
                    s[j, n_ctx + BLOCK + n:n_ctx + 2 * BLOCK + n, :] = zeros

            def body(c, carry):
                r0 = pl.multiple_of(c * BLOCK, BLOCK)
                kk = rope(k_ref[pl.ds(r0, BLOCK), :], cos_ref[pl.ds(r0, BLOCK), :], sin_ref[pl.ds(r0, BLOCK), :])
                k0, k1 = tile4(kk)
                v0, v1 = tile4(v_ref[pl.ds(r0, BLOCK), :])
                w0 = pl.multiple_of(n_ctx + BLOCK + r0, BLOCK)
                kt_s[0, pl.ds(w0, BLOCK), :] = k0
                kt_s[1, pl.ds(w0, BLOCK), :] = k1
                vt_s[0, pl.ds(w0, BLOCK), :] = v0
                vt_s[1, pl.ds(w0, BLOCK), :] = v1
                return carry

            lax.fori_loop(0, n // BLOCK, body, 0)

    q = q_ref[...]
    if band:
        q0 = pl.multiple_of(i * BLOCK, BLOCK)
        cos = cos_ref[pl.ds(q0, BLOCK), :]
        sin = sin_ref[pl.ds(q0, BLOCK), :]
        q = jnp.concatenate([rope(q[:, LANES * t:LANES * (t + 1)], cos, sin) for t in range(A_W // LANES)], axis=1)
        qi = lax.broadcasted_iota(jnp.int32, (BLOCK, 3 * BLOCK), 0)
        kj = lax.broadcasted_iota(jnp.int32, (BLOCK, 3 * BLOCK), 1)
        tok = kj + (i - 1) * BLOCK
        ok = (kj >= qi) & (kj <= qi + 2 * WINDOW) & (tok >= 0) & (tok < n)
        bias = jnp.where(ok, 0.0, NEG)
        bias4 = jnp.concatenate([bias] * A_GROUP, axis=0)
        b0 = pl.multiple_of(n_ctx + i * BLOCK, BLOCK)
    q = q * (DH ** -0.5)

    for j in range(A_KV_HEADS):
        q256 = q[:, HW * j:HW * (j + 1)]
        qs = jnp.concatenate([jnp.where(gmasks[g], q256, 0.0) for g in range(A_GROUP)], axis=0).astype(bf16)
        sink = jnp.concatenate(
            [jnp.broadcast_to(sk_ref[A_GROUP * j + g:A_GROUP * j + g + 1, 0:1], (BLOCK, 1)) for g in range(A_GROUP)],
            axis=0)
        s1 = _dot_nt(qs, kt_s[j, 0:n_ctx, :])
        m = jnp.maximum(jnp.max(s1, axis=1, keepdims=True), sink)
        if band:
            s2 = _dot_nt(qs, kt_s[j, pl.ds(b0, 3 * BLOCK), :]) + bias4
            m = jnp.maximum(m, jnp.max(s2, axis=1, keepdims=True))
        p1 = jnp.exp(s1 - m)
        den = jnp.sum(p1, axis=1, keepdims=True) + jnp.exp(sink - m)
        acc = _dot(p1.astype(bf16), vt_s[j, 0:n_ctx, :])
        if band:
            p2 = jnp.exp(s2 - m)
            den = den + jnp.sum(p2, axis=1, keepdims=True)
            acc = acc + _dot(p2.astype(bf16), vt_s[j, pl.ds(b0, 3 * BLOCK), :])
        acc = acc * (1.0 / den)
        out = jnp.zeros((BLOCK, HW), f32)
        for g in range(A_GROUP):
            out = jnp.where(gmasks[g], acc[BLOCK * g:BLOCK * (g + 1), :], out)
        o_ref[:, HW * j:HW * (j + 1)] = out.astype(o_ref.dtype)


def _attention(at, ck, cv, ck_spec, cv_spec, cos, sin, sk, bsz, n, n_ctx, band):
    nq = n // BLOCK
    s_tot = n_ctx + (n + 2 * BLOCK if band else 0)
    at3 = at.reshape(bsz, n, A_W + 2 * A_KV_W)
    kcol = A_W // A_KV_W
    return pl.pallas_call(
        functools.partial(_attn_kernel, n_ctx=n_ctx, n=n, band=band),
        grid=(bsz, nq),
        in_specs=[
            pl.BlockSpec((None, BLOCK, A_W), lambda b, i: (b, i, 0)),
            pl.BlockSpec((None, n, A_KV_W), lambda b, i: (b, 0, kcol)),
            pl.BlockSpec((None, n, A_KV_W), lambda b, i: (b, 0, kcol + 1)),
            ck_spec,
            cv_spec,
            _const_spec(cos.shape),
            _const_spec(sin.shape),
            _const_spec((A_HEADS, LANES)),
        ],
        out_specs=pl.BlockSpec((None, BLOCK, A_W), lambda b, i: (b, i, 0)),
        out_shape=jax.ShapeDtypeStruct((bsz, n, A_W), bf16),
        scratch_shapes=[
            pltpu.VMEM((A_KV_HEADS, s_tot, HW), bf16),
            pltpu.VMEM((A_KV_HEADS, s_tot, HW), bf16),
        ],
        compiler_params=_params(2),
        name="attention",
    )(at3, at3, at3, ck, cv, cos, sin, sk)


def _rope_tables(n):
    tok = np.arange(n)
    pos = np.stack([tok // GRID_W, tok % GRID_W], axis=1).astype(np.float32)
    quarter = DH // 4
    freqs = jnp.power(ROPE_BASE, -jnp.arange(quarter, dtype=f32) / quarter)
    lane = np.arange(LANES) % DH
    axis = lane // (DH // 2)
    fidx = lane % quarter
    sign = np.where((lane % (DH // 2)) < quarter, -1.0, 1.0).astype(np.float32)
    ang = jnp.asarray(pos)[:, axis] * freqs[fidx][None, :]
    return jnp.cos(ang), jnp.sin(ang) * sign[None, :]


CONV_PAD = 16


def _conv_kernel(u_ref, w_ref, b_ref, lg_ref, lb_ref, y_ref, z_s, *, n):
    nc = n // BLOCK
    z_s[0:CONV_PAD, :] = jnp.zeros((CONV_PAD, HW), f32)
    z_s[CONV_PAD + n:2 * CONV_PAD + n, :] = jnp.zeros((CONV_PAD, HW), f32)

    def glu(c, carry):
        r0 = pl.multiple_of(c * BLOCK, BLOCK)
        blk = u_ref[pl.ds(r0, BLOCK), :].astype(f32)
        z_s[pl.ds(pl.multiple_of(r0 + CONV_PAD, SUBLANES), BLOCK), :] = blk[:, 0:HW] * _sigmoid(blk[:, HW:2 * HW])
        return carry

    lax.fori_loop(0, nc, glu, 0)

    first = CONV_PAD - CONV_K // 2
    span = BLOCK + 2 * CONV_PAD

    def tile(c, carry):
        r0 = pl.multiple_of(c * BLOCK, BLOCK)
        win = z_s[pl.ds(r0, span), :]
        acc = jnp.zeros((BLOCK, HW), f32)
        for r in range(SUBLANES):
            taps = [kk for kk in range(CONV_K) if (first + kk) % SUBLANES == r]
            if not taps:
                continue
            shifted = win[r:r + span - SUBLANES, :]
            for kk in taps:
                a = (first + kk) // SUBLANES * SUBLANES
                acc = acc + shifted[a:a + BLOCK, :] * w_ref[kk:kk + 1, :]
        zc = acc + b_ref[...]
        mu = jnp.mean(zc, axis=-1, keepdims=True)
        dev = zc - mu
        var = jnp.mean(dev * dev, axis=-1, keepdims=True)
        t = dev * lax.rsqrt(var + EPS) * lg_ref[...] + lb_ref[...]
        y_ref[pl.ds(r0, BLOCK), :] = (t * _sigmoid(t)).astype(y_ref.dtype)
        return carry

    lax.fori_loop(0, nc, tile, 0)


def _conv(cu, w, b, lg, lb, bsz, n):
    return pl.pallas_call(
        functools.partial(_conv_kernel, n=n),
        grid=(bsz,),
        in_specs=[
            pl.BlockSpec((None, n, 2 * HW), lambda bb: (bb, 0, 0)),
            _const_spec((CONV_K, HW)),
            _const_spec((1, HW)),
            _const_spec((1, HW)),
            _const_spec((1, HW)),
        ],
        out_specs=pl.BlockSpec((None, n, HW), lambda bb: (bb, 0, 0)),
        out_shape=jax.ShapeDtypeStruct((bsz, n, HW), bf16),
        scratch_shapes=[pltpu.VMEM((n + 2 * CONV_PAD, HW), f32)],
        compiler_params=_params(1),
        name="conv",
    )(cu.reshape(bsz, n, 2 * HW), w, b, lg, lb)


def _merge_kernel(x_ref, ym_ref, ya_ref, yc_ref, yr_ref, gt_ref, mod_ref, ng_ref, wbr_ref, wout_ref, o_ref):
    acc = None
    for i, y_ref in enumerate((ym_ref, ya_ref, yc_ref, yr_ref)):
        br = _dot(y_ref[...], wbr_ref[BR_OFFS[i]:BR_OFFS[i + 1], :])
        gate = _sigmoid(gt_ref[:, D_MODEL * i:D_MODEL * (i + 1)].astype(f32))
        acc = gate * br if acc is None else acc + gate * br
    out = _dot(acc.astype(bf16), wout_ref[...])
    r = out * lax.rsqrt(jnp.mean(out * out, axis=-1, keepdims=True) + EPS) * ng_ref[1:2, :]
    o_ref[...] = x_ref[...] + mod_ref[2:3, :] * r


def _merge(x2d, ym, ya, yc, yr, gt, mod4, ng, wbr, wout, layer, row_of_tile):
    m = x2d.shape[0]
    tm = ROW_TILE
    rows = lambda w: pl.BlockSpec((tm, w), lambda i: (i, 0))
    return pl.pallas_call(
        _merge_kernel,
        grid=(m // tm,),
        in_specs=[
            rows(D_MODEL), rows(HW), rows(A_W), rows(HW), rows(HW), rows(N_BRANCH * D_MODEL),
            pl.BlockSpec((None, None, 6, D_MODEL), lambda i: (layer, row_of_tile(i), 0, 0)),
            _const_spec((4, D_MODEL)),
            _const_spec((BR_OFFS[-1], D_MODEL)),
            _const_spec((D_MODEL, D_MODEL)),
        ],
        out_specs=rows(D_MODEL),
        out_shape=jax.ShapeDtypeStruct((m, D_MODEL), f32),
        compiler_params=_params(1),
        name="merge",
    )(x2d, ym.reshape(m, HW), ya.reshape(m, A_W), yc.reshape(m, HW), yr.reshape(m, HW), gt, mod4, ng, wbr, wout)


def _mlp_kernel(x_ref, mod_ref, ng_ref, w1_ref, w2_ref, o_ref):
    x = x_ref[...]
    y = x * lax.rsqrt(jnp.mean(x * x, axis=-1, keepdims=True) + EPS) * ng_ref[2:3, :]
    h = (y * (1.0 + mod_ref[4:5, :]) + mod_ref[3:4, :]).astype(bf16)
    chunk = 512
    acc = jnp.zeros(x.shape, f32)
    for c in range(0, D_FF, chunk):
        a = jnp.maximum(_dot(h, w1_ref[:, c:c + chunk]), 0.0)
        acc = acc + _dot((a * a).astype(bf16), w2_ref[c:c + chunk, :])
    r = acc * lax.rsqrt(jnp.mean(acc * acc, axis=-1, keepdims=True) + EPS) * ng_ref[3:4, :]
    o_ref[...] = x + mod_ref[5:6, :] * r


def _mlp(x2d, mod4, ng, w1, w2, layer, row_of_tile):
    m = x2d.shape[0]
    tm = ROW_TILE
    return pl.pallas_call(
        _mlp_kernel,
        grid=(m // tm,),
        in_specs=[
            pl.BlockSpec((tm, D_MODEL), lambda i: (i, 0)),
            pl.BlockSpec((None, None, 6, D_MODEL), lambda i: (layer, row_of_tile(i), 0, 0)),
            _const_spec((4, D_MODEL)),
            _const_spec((D_MODEL, D_FF)),
            _const_spec((D_FF, D_MODEL)),
        ],
        out_specs=pl.BlockSpec((tm, D_MODEL), lambda i: (i, 0)),
        out_shape=jax.ShapeDtypeStruct((m, D_MODEL), f32),
        compiler_params=_params(1),
        name="mlp",
    )(x2d, mod4, ng, w1, w2)


def _pack_w_in(w, b):
    pad_w = jnp.zeros((w.shape[0], LANES - N_GATES), w.dtype)
    cut = PK_MG + N_GATES
    w_pk = jnp.concatenate([w[:, :cut], pad_w, w[:, cut:]], axis=1).astype(bf16)
    b_pk = jnp.concatenate([b[:cut], jnp.zeros((LANES - N_GATES,), b.dtype), b[cut:]])[None, :]
    return w_pk, b_pk


def _stack_state(s):
    bsz = s.shape[0]
    s = s.reshape(bsz, 2, HW, DH)
    return jnp.pad(s, ((0, 0), (0, 0), (0, 0), (0, LANES - DH)))


def _layer(x2d, bsz, n, layer, mod4, lw, cache, row_of_tile):
    is_ctx = cache is None
    mq, mg, at, cv, rt, gt = _inproj(x2d, mod4, lw['ng'], lw['w_in'], lw['b_in'], layer, row_of_tile)

    if is_ctx:
        c0 = jnp.zeros((bsz, 2, HW, LANES), f32)
        n0 = jnp.zeros((bsz, 2, 1, HW), f32)
        m0 = jnp.zeros((bsz, 2, N_HEADS, 1, LANES), f32)
        s0 = jnp.zeros((bsz, 2, HW, LANES), f32)
    else:
        c0 = _stack_state(cache['C'])
        n0 = cache['n'].reshape(bsz, 2, 1, HW)
        m0 = jnp.broadcast_to(cache['m'][:, :, :, None, None], (bsz, 2, N_HEADS, 1, LANES))
        s0 = _stack_state(cache['S'])
    ym, c_fin, n_fin, m_fin = _mlstm(mq, mg, c0, n0, m0, bsz, n)
    yr, s_fin = _retention(rt, s0, lw['ret_dl'], lw['ret_dr'], bsz, n)

    at3 = at.reshape(bsz, n, A_W + 2 * A_KV_W)
    kcol = A_W // A_KV_W
    if is_ctx:
        dummy = jnp.zeros((SUBLANES, LANES), f32)
        ck_spec = pl.BlockSpec((None, n, A_KV_W), lambda b, i: (b, 0, kcol))
        cv_spec = pl.BlockSpec((None, n, A_KV_W), lambda b, i: (b, 0, kcol + 1))
        ya = _attention(at, at3, at3, ck_spec, cv_spec, dummy, dummy, lw['sink'], bsz, n, n, False)
    else:
        n_ctx = cache['k'].shape[2]
        ck_spec = pl.BlockSpec((None, None, n_ctx, A_KV_W), lambda b, i: (b, layer, 0, 0))
        cos, sin = _rope_tables(n)
        ya = _attention(at, cache['k'], cache['v'], ck_spec, ck_spec, cos, sin, lw['sink'], bsz, n, n_ctx, True)

    yc = _conv(cv, lw['conv_w'], lw['conv_b'], lw['conv_ln_g'], lw['conv_ln_b'], bsz, n)

    x2d = _merge(x2d, ym, ya, yc, yr, gt, mod4, lw['ng'], lw['w_branch'], lw['w_out'], layer, row_of_tile)
    x2d = _mlp(x2d, mod4, lw['ng'], lw['w_mlp1'], lw['w_mlp2'], layer, row_of_tile)

    ctx = None
    if is_ctx:
        ctx = (at3[:, :, A_W:A_W + A_KV_W].reshape(bsz, n, A_KV_HEADS, DH),
               at3[:, :, A_W + A_KV_W:].reshape(bsz, n, A_KV_HEADS, DH),
               c_fin.reshape(bsz, 2, N_HEADS, DH, DH),
               n_fin.reshape(bsz, 2, N_HEADS, DH),
               m_fin[:, :, :, 0, 0],
               s_fin.reshape(bsz, 2, N_HEADS, DH, DH))
    return x2d, ctx


def kernel(x_prompt, x_sample, c, cache_k, cache_v, state_mlstm_C, state_mlstm_n, state_mlstm_m, state_ret,
           c_ctx, w_ada, b_ada, norm_g, w_in, b_in, w_branch, w_out, attn_sink, ret_decay_logit,
           conv_w, conv_b, conv_ln_g, conv_ln_b, w_mlp1, w_mlp2):
    bsz, seq, _ = x_prompt.shape
    dbsz, dseq, _ = x_sample.shape
    past = cache_k.shape[2]

    c_rows = jnp.zeros((SUBLANES, D_MODEL), f32).at[0].set(c_ctx).at[1:1 + dbsz].set(c)
    mod4 = _ada(c_rows, w_ada, b_ada).reshape(DEPTH, SUBLANES, 6, D_MODEL)

    layers = []
    for l in range(DEPTH):
        w_pk, b_pk = _pack_w_in(w_in[l], b_in[l])
        layers.append({
            'ng': norm_g[l], 'w_in': w_pk, 'b_in': b_pk,
            'w_branch': w_branch[l].astype(bf16), 'w_out': w_out[l].astype(bf16),
            'w_mlp1': w_mlp1[l].astype(bf16), 'w_mlp2': w_mlp2[l].astype(bf16),
            'sink': jnp.broadcast_to(attn_sink[l][:, None], (A_HEADS, LANES)),
            'ret_dl': jnp.repeat(ret_decay_logit[l], DH, axis=-1).reshape(2, 1, HW),
            'ret_dr': jnp.broadcast_to(jnp.repeat(ret_decay_logit[l], DH, axis=-1)[:, :, None], (2, HW, LANES)),
            'conv_w': conv_w[l], 'conv_b': conv_b[l][None, :],
            'conv_ln_g': conv_ln_g[l][None, :], 'conv_ln_b': conv_ln_b[l][None, :],
        })

    xp = x_prompt.reshape(bsz * seq, D_MODEL)
    ctxs = []
    for l in range(DEPTH):
        xp, ctx = _layer(xp, bsz, seq, l, mod4, layers[l], None, lambda i: 0)
        ctxs.append(ctx)

    xs = x_sample.reshape(dbsz * dseq, D_MODEL)
    tiles_per_seq = dseq // ROW_TILE
    ck = cache_k.reshape(dbsz, DEPTH, past, A_KV_W)
    cv = cache_v.reshape(dbsz, DEPTH, past, A_KV_W)
    for l in range(DEPTH):
        cache = {'k': ck, 'v': cv, 'C': state_mlstm_C[:, l], 'n': state_mlstm_n[:, l],
                 'm': state_mlstm_m[:, l], 'S': state_ret[:, l]}
        xs, _ = _layer(xs, dbsz, dseq, l, mod4, layers[l], cache, lambda i: 1 + i // tiles_per_seq)

    stack = lambda j: jnp.stack([ctxs[l][j] for l in range(DEPTH)], axis=1)
    return (xp.reshape(bsz, seq, D_MODEL), xs.reshape(dbsz, dseq, D_MODEL),
            stack(0), stack(1), stack(2), stack(3), stack(4), stack(5))
```

```python
import functools

import numpy as np
import jax
import jax.numpy as jnp
from jax import lax
from jax.experimental import pallas as pl
from jax.experimental.pallas import tpu as pltpu

f32 = jnp.float32
bf16 = jnp.bfloat16

D_MODEL = 1024
DEPTH = 2
GRID_W = 64
BLOCK = 128
N_HEADS = 4
DH = 64
HW = N_HEADS * DH
A_HEADS = 8
A_KV_HEADS = 2
A_GROUP = A_HEADS // A_KV_HEADS
A_W = A_HEADS * DH
A_KV_W = A_KV_HEADS * DH
WINDOW = 128
ROPE_BASE = 10000.0
CONV_K = 31
D_FF = 4 * D_MODEL
N_BRANCH = 4
EPS = 1e-6
NEG = -1e30
LANES = 128
SUBLANES = 8

N_GATES = 4 * N_HEADS
PK_MQ = 0
PK_MG = PK_MQ + 4 * HW
PK_AT = PK_MG + 2 * LANES
PK_CV = PK_AT + A_W + 2 * A_KV_W
PK_RT = PK_CV + 2 * HW
PK_GT = PK_RT + 4 * HW
PK_W = PK_GT + N_BRANCH * D_MODEL
BR_OFFS = (0, HW, HW + A_W, 2 * HW + A_W, 3 * HW + A_W)

VMEM_LIMIT = 56 * 1024 * 1024
ROW_TILE = 512


def _sigmoid(x):
    return 0.5 * (jnp.tanh(0.5 * x) + 1.0)


def _log_sigmoid(x):
    return jnp.minimum(x, 0.0) - jnp.log1p(jnp.exp(-jnp.abs(x)))


def _dot(a, b):
    return jnp.dot(a, b, preferred_element_type=f32)


def _dot_nt(a, b):
    return lax.dot_general(a, b, (((1,), (1,)), ((), ())), preferred_element_type=f32)


def _split3(x):
    x1 = x.astype(bf16)
    r1 = x - x1.astype(f32)
    x2 = r1.astype(bf16)
    x3 = (r1 - x2.astype(f32)).astype(bf16)
    return x1, x2, x3


def _dot2_r(t, x):
    x1 = x.astype(bf16)
    x2 = (x - x1.astype(f32)).astype(bf16)
    return _dot(t, x1) + _dot(t, x2)


def _dot3_l(x, t):
    x1, x2, x3 = _split3(x)
    return _dot(x1, t) + _dot(x2, t) + _dot(x3, t)


def _dot2_l(x, t):
    x1 = x.astype(bf16)
    x2 = (x - x1.astype(f32)).astype(bf16)
    return _dot(x1, t) + _dot(x2, t)


def _const_spec(shape):
    nd = len(shape)
    return pl.BlockSpec(shape, lambda *_: (0,) * nd, pipeline_mode=pl.Buffered(1))


def _params(n_axes):
    return pltpu.CompilerParams(dimension_semantics=("arbitrary",) * n_axes, vmem_limit_bytes=VMEM_LIMIT)


def _ada_kernel(c_ref, w_ref, b_ref, o_ref):
    c = c_ref[...]
    s = (c * _sigmoid(c)).astype(bf16)
    o_ref[...] = _dot(s, w_ref[...].astype(bf16)) + b_ref[...]


def _ada(c_rows, w_ada, b_ada):
    tn = 1536
    n_out = 6 * D_MODEL
    return pl.pallas_call(
        _ada_kernel,
        grid=(DEPTH, n_out // tn),
        in_specs=[
            pl.BlockSpec((SUBLANES, D_MODEL), lambda l, j: (0, 0)),
            pl.BlockSpec((None, D_MODEL, tn), lambda l, j: (l, 0, j)),
            pl.BlockSpec((None, 1, tn), lambda l, j: (l, 0, j)),
        ],
        out_specs=pl.BlockSpec((None, SUBLANES, tn), lambda l, j: (l, 0, j)),
        out_shape=jax.ShapeDtypeStruct((DEPTH, SUBLANES, n_out), f32),
        compiler_params=_params(2),
        name="ada",
    )(c_rows, w_ada, b_ada.reshape(DEPTH, 1, n_out))


def _inproj_kernel(x_ref, mod_ref, ng_ref, w_ref, b_ref, mq_ref, mg_ref, at_ref, cv_ref, rt_ref, gt_ref):
    x = x_ref[...]
    y = x * lax.rsqrt(jnp.mean(x * x, axis=-1, keepdims=True) + EPS) * ng_ref[0:1, :]
    h = (y * (1.0 + mod_ref[1:2, :]) + mod_ref[0:1, :]).astype(bf16)
    chunk = 512
    for ref, base in ((mq_ref, PK_MQ), (mg_ref, PK_MG), (at_ref, PK_AT), (cv_ref, PK_CV), (rt_ref, PK_RT),
                      (gt_ref, PK_GT)):
        total = ref.shape[-1]
        for c in range(0, total, chunk):
            w = min(chunk, total - c)
            r = _dot(h, w_ref[:, base + c:base + c + w]) + b_ref[:, base + c:base + c + w]
            ref[:, c:c + w] = r.astype(ref.dtype)


def _inproj(x2d, mod4, ng, w_pk, b_pk, layer, row_of_tile):
    m = x2d.shape[0]
    tm = ROW_TILE
    widths = (4 * HW, 2 * LANES, A_W + 2 * A_KV_W, 2 * HW, 4 * HW, N_BRANCH * D_MODEL)
    dtypes = (bf16, f32, f32, bf16, bf16, bf16)
    return pl.pallas_call(
        _inproj_kernel,
        grid=(m // tm,),
        in_specs=[
            pl.BlockSpec((tm, D_MODEL), lambda i: (i, 0)),
            pl.BlockSpec((None, None, 6, D_MODEL), lambda i: (layer, row_of_tile(i), 0, 0)),
            _const_spec((4, D_MODEL)),
            _const_spec((D_MODEL, PK_W)),
            _const_spec((1, PK_W)),
        ],
        out_specs=[pl.BlockSpec((tm, w), lambda i: (i, 0)) for w in widths],
        out_shape=[jax.ShapeDtypeStruct((m, w), dt) for w, dt in zip(widths, dtypes)],
        compiler_params=_params(1),
        name="inproj",
    )(x2d, mod4, ng, w_pk, b_pk)


def _head_lane_masks():
    lane_head = lax.broadcasted_iota(jnp.int32, (1, HW), 1) // DH
    return [lane_head == h for h in range(N_HEADS)]


def _block_diag_mask():
    r = lax.broadcasted_iota(jnp.int32, (HW, HW), 0) // DH
    c = lax.broadcasted_iota(jnp.int32, (HW, HW), 1) // DH
    return r == c


def _expand_state(x):
    y = x + pltpu.roll(x, DH, 1)
    return jnp.where(_block_diag_mask(), jnp.concatenate([y, y], axis=1), 0.0)


def _compact_state(c):
    a = c[:, 0:LANES] + c[:, LANES:2 * LANES]
    return (a + pltpu.roll(a, DH, 1))[:, 0:DH]


def _head_block_kt(kt):
    zeros = jnp.zeros((DH, BLOCK), kt.dtype)
    cols = []
    for h in range(N_HEADS):
        cols.append(jnp.concatenate([kt[DH * h:DH * (h + 1)] if hh == h else zeros for hh in range(N_HEADS)], axis=0))
    return jnp.concatenate(cols, axis=1)


def _head_block_v(v, masks):
    return [jnp.where(masks[h], v, jnp.zeros_like(v)) for h in range(N_HEADS)]


def _mlstm_kernel(x_ref, g_ref, c0_ref, n0_ref, m0_ref, y_ref, cn_ref, nn_ref, mn_ref,
                  hf_s, hb_s, c_s, n_s, m_s, *, nc):
    wide = N_HEADS * BLOCK
    row = lax.broadcasted_iota(jnp.int32, (BLOCK, BLOCK), 0)
    col = lax.broadcasted_iota(jnp.int32, (BLOCK, BLOCK), 1)
    tsum = (jnp.where(row >= col, 1.0, 0.0).astype(bf16), jnp.where(row <= col, 1.0, 0.0).astype(bf16))
    roww = lax.broadcasted_iota(jnp.int32, (BLOCK, wide), 0)
    colw = lax.broadcasted_iota(jnp.int32, (BLOCK, wide), 1) & (BLOCK - 1)
    tri4 = (roww >= colw, roww <= colw)
    masks = _head_lane_masks()
    lane = lax.broadcasted_iota(jnp.int32, (1, LANES), 1)
    lane8 = lax.broadcasted_iota(jnp.int32, (SUBLANES, BLOCK), 1)
    row8 = lax.broadcasted_iota(jnp.int32, (SUBLANES, BLOCK), 0)

    def onehot(shape, row_expr, col_expr):
        r = lax.broadcasted_iota(jnp.int32, shape, 0)
        c = lax.broadcasted_iota(jnp.int32, shape, 1)
        return jnp.where(row_expr(r) == col_expr(c), 1.0, 0.0).astype(bf16)

    g0s = (0, 2 * N_HEADS)
    valid = tuple((lane >= g0) & (lane < g0 + N_HEADS) for g0 in g0s)
    sel = tuple(onehot((LANES, wide), lambda r: r, lambda c, g0=g0: c // BLOCK + g0) for g0 in g0s)
    hexp = tuple(onehot((LANES, HW), lambda r: r, lambda c, g0=g0: c // DH + g0) for g0 in g0s)
    hselg = tuple(onehot((HW, LANES), lambda r, g0=g0: r // DH + g0, lambda c: c) for g0 in g0s)
    ones_blk = tuple(onehot((wide, LANES), lambda r, g0=g0: r // BLOCK + g0, lambda c: c) for g0 in g0s)

    for d in range(2):
        c_s[d] = _expand_state(c0_ref[d])
        n_s[d] = n0_ref[d]
        m_s[d] = m0_ref[d]

    def load(d, c):
        r0 = pl.multiple_of(c * BLOCK, BLOCK)
        return (r0, x_ref[pl.ds(r0, BLOCK), 0:3 * HW], g_ref[pl.ds(r0, BLOCK), :], c_s[d], n_s[d], m_s[d])

    def compute(d, blk, g, cmat, nrow, m_row):
        g0 = g0s[d]
        q = blk[:, 0:HW]
        k = blk[:, HW:2 * HW] * (DH ** -0.5)
        v = blk[:, 2 * HW:3 * HW]
        ga = g[:, 0:LANES]
        gb = g[:, LANES:2 * LANES]
        bc = _dot2_r(tsum[d], _log_sigmoid(gb))
        beta = ga - bc
        beta_t = beta.T
        cmx = beta
        sh = 1
        while sh < BLOCK:
            if d == 0:
                cmx = jnp.maximum(cmx, jnp.where(row >= sh, pltpu.roll(cmx, sh, 0), NEG))
            else:
                cmx = jnp.maximum(cmx, jnp.where(row < BLOCK - sh, pltpu.roll(cmx, BLOCK - sh, 0), NEG))
            sh *= 2
        mx = jnp.maximum(m_row, cmx)
        alpha = jnp.where(valid[d], -mx, 0.0)
        w_int = jnp.where(valid[d], jnp.exp(m_row - mx), 0.0)
        e_nb = jnp.exp(alpha - bc)
        z = _dot2_l(alpha, sel[d])
        beta_w = jnp.concatenate([beta_t[g0 + h:g0 + h + 1, :] for h in range(N_HEADS)], axis=1)
        ktf = k.astype(f32).T
        s_wide = _dot(q, _head_block_kt(ktf.astype(bf16)))
        p = s_wide * jnp.exp(jnp.where(tri4[d], z + beta_w, NEG))
        vm = _head_block_v(v, masks)
        vblk = jnp.concatenate(
            [jnp.concatenate([vm[h], ones_blk[d][BLOCK * h:BLOCK * (h + 1)]], axis=1) for h in range(N_HEADS)], axis=0)
        res = _dot(p.astype(bf16), vblk)
        qc = _dot(q, cmat.astype(bf16))
        qn = _dot((q.astype(f32) * nrow).astype(bf16), hselg[d])
        den = res[:, HW:HW + LANES] + w_int * qn
        inv = jnp.where(valid[d], 1.0 / jnp.maximum(jnp.abs(den), e_nb), 0.0)
        fac = _dot(jnp.concatenate([w_int, inv], axis=0).astype(bf16), hexp[d])
        hout = (res[:, 0:HW] + fac[0:BLOCK] * qc) * fac[BLOCK:2 * BLOCK]
        last = BLOCK - 1 if d == 0 else 0
        a_last = alpha[last:last + 1, :]
        m_new = jnp.where(valid[d], bc[last:last + 1, :] - a_last, 0.0)
        dec = jnp.exp(m_row + a_last)
        wr = jnp.zeros((SUBLANES, BLOCK), f32)
        kws = []
        for h in range(N_HEADS):
            w_row = jnp.exp(beta_t[g0 + h:g0 + h + 1, :] + a_last[:, g0 + h:g0 + h + 1])
            kws.append(ktf[DH * h:DH * (h + 1), :] * w_row)
            wr = jnp.where(row8 == h, w_row, wr)
        upd = _dot(jnp.concatenate(kws, axis=0).astype(bf16), v)
        wk = _dot(wr.astype(bf16), k)
        nnew = jnp.zeros((1, HW), f32)
        cnew = []
        for h in range(N_HEADS):
            rs = slice(DH * h, DH * (h + 1))
            dh = dec[:, g0 + h:g0 + h + 1]
            cnew.append(dh * cmat[rs, :] + jnp.where(masks[h], upd[rs, :], 0.0))
            nnew = jnp.where(masks[h], dh * nrow + wk[h:h + 1, :], nnew)
        return hout, jnp.concatenate(cnew, axis=0), nnew, m_new

    def body(i, carry):
        loaded = [load(0, i), load(1, nc - 1 - i)]
        outs = [compute(d, *loaded[d][1:]) for d in range(2)]
        for d in range(2):
            hout, cnew, nnew, m_new = outs[d]
            (hf_s, hb_s)[d][pl.ds(loaded[d][0], BLOCK), :] = hout
            c_s[d] = cnew
            n_s[d] = nnew
            m_s[d] = m_new
        return carry

    lax.fori_loop(0, nc, body, 0)

    def fin(c, carry):
        r0 = pl.multiple_of(c * BLOCK, BLOCK)
        o = x_ref[pl.ds(r0, BLOCK), 3 * HW:4 * HW].astype(f32)
        hsum = hf_s[pl.ds(r0, BLOCK), :] + hb_s[pl.ds(r0, BLOCK), :]
        y_ref[pl.ds(r0, BLOCK), :] = (_sigmoid(o) * hsum).astype(y_ref.dtype)
        return carry

    lax.fori_loop(0, nc, fin, 0)

    for d in range(2):
        cn_ref[d] = _compact_state(c_s[d])
        nn_ref[d] = n_s[d]
        mn_ref[d] = m_s[d]


def _mlstm(mq, mg, c0, n0, m0, bsz, n):
    nc = n // BLOCK
    state_spec = lambda shape: pl.BlockSpec((None,) + shape, lambda b: (b,) + (0,) * len(shape))
    return pl.pallas_call(
        functools.partial(_mlstm_kernel, nc=nc),
        grid=(bsz,),
        in_specs=[
            pl.BlockSpec((None, n, 4 * HW), lambda b: (b, 0, 0)),
            pl.BlockSpec((None, n, 2 * LANES), lambda b: (b, 0, 0)),
            state_spec((2, HW, LANES)),
            state_spec((2, 1, HW)),
            state_spec((2, 1, LANES)),
        ],
        out_specs=[
            pl.BlockSpec((None, n, HW), lambda b: (b, 0, 0)),
            state_spec((2, HW, DH)),
            state_spec((2, 1, HW)),
            state_spec((2, 1, LANES)),
        ],
        out_shape=[
            jax.ShapeDtypeStruct((bsz, n, HW), bf16),
            jax.ShapeDtypeStruct((bsz, 2, HW, DH), f32),
            jax.ShapeDtypeStruct((bsz, 2, 1, HW), f32),
            jax.ShapeDtypeStruct((bsz, 2, 1, LANES), f32),
        ],
        scratch_shapes=[
            pltpu.VMEM((n, HW), f32),
            pltpu.VMEM((n, HW), f32),
            pltpu.VMEM((2, HW, HW), f32),
            pltpu.VMEM((2, 1, HW), f32),
            pltpu.VMEM((2, 1, LANES), f32),
        ],
        compiler_params=_params(1),
        name="mlstm",
    )(mq.reshape(bsz, n, 4 * HW), mg.reshape(bsz, n, 2 * LANES), c0, n0, m0)


def _ret_kernel(x_ref, s0_ref, dl_ref, dr_ref, y_ref, sn_ref, of_s, ob_s, s_s, qd_s, kd_s, cd_s, dm_s, *, nc):
    masks = _head_lane_masks()
    bdm = _block_diag_mask()
    rowf = lax.broadcasted_iota(jnp.int32, (BLOCK, HW), 0).astype(f32)
    rel = (lax.broadcasted_iota(jnp.int32, (BLOCK, BLOCK), 0) -
           lax.broadcasted_iota(jnp.int32, (BLOCK, BLOCK), 1)).astype(f32)
    avg = jnp.where(bdm, 1.0 / DH, 0.0).astype(bf16)

    for d in range(2):
        lg = _log_sigmoid(dl_ref[d])
        if d == 0:
            qd_s[d] = jnp.exp((rowf + 1.0) * lg)
            kd_s[d] = jnp.exp((BLOCK - 1.0 - rowf) * lg)
            reld = rel
        else:
            qd_s[d] = jnp.exp((BLOCK - rowf) * lg)
            kd_s[d] = jnp.exp(rowf * lg)
            reld = -rel
        cdh = jnp.exp(float(BLOCK) * _log_sigmoid(dr_ref[d]))
        cd_s[d] = jnp.concatenate([cdh, cdh], axis=1)
        for h in range(N_HEADS):
            lgh = lg[:, DH * h:DH * h + 1]
            dm_s[d, :, BLOCK * h:BLOCK * (h + 1)] = jnp.where(reld >= 0.0, jnp.exp(jnp.maximum(reld, 0.0) * lgh), 0.0)
        s_s[d] = _expand_state(s0_ref[d])

    def compute(d, blk, sm):
        q = blk[:, 0:HW]
        k = blk[:, HW:2 * HW] * (DH ** -0.5)
        v = blk[:, 2 * HW:3 * HW]
        kf = k.astype(f32)
        kt = kf.T.astype(bf16)
        p = _dot(q, _head_block_kt(kt)) * dm_s[d]
        o = _dot(p.astype(bf16), jnp.concatenate(_head_block_v(v, masks), axis=0)) + qd_s[d] * _dot(q, sm.astype(bf16))
        kdt = (kf * kd_s[d]).T.astype(bf16)
        return o, cd_s[d] * sm + jnp.where(bdm, _dot(kdt, v), 0.0)

    def body(i, carry):
        r0s = [pl.multiple_of(i * BLOCK, BLOCK), pl.multiple_of((nc - 1 - i) * BLOCK, BLOCK)]
        loaded = [(x_ref[pl.ds(r0s[d], BLOCK), 0:3 * HW], s_s[d]) for d in range(2)]
        outs = [compute(d, *loaded[d]) for d in range(2)]
        for d in range(2):
            (of_s, ob_s)[d][pl.ds(r0s[d], BLOCK), :] = outs[d][0]
            s_s[d] = outs[d][1]
        return carry

    lax.fori_loop(0, nc, body, 0)

    fin_chunks = 4 if nc % 4 == 0 else (2 if nc % 2 == 0 else 1)
    rows = fin_chunks * BLOCK

    def fin(c, carry):
        r0 = pl.multiple_of(c * rows, rows)
        gate = x_ref[pl.ds(r0, rows), 3 * HW:4 * HW].astype(f32)
        o = of_s[pl.ds(r0, rows), :] + ob_s[pl.ds(r0, rows), :]
        dev = o - _dot3_l(o, avg)
        var = _dot2_l(dev * dev, avg)
        y = gate * _sigmoid(gate) * (dev * lax.rsqrt(var + EPS))
        y_ref[pl.ds(r0, rows), :] = y.astype(y_ref.dtype)
        return carry

    lax.fori_loop(0, nc // fin_chunks, fin, 0)

    for d in range(2):
        sn_ref[d] = _compact_state(s_s[d])


def _retention(rt, s0, dl, dr, bsz, n):
    nc = n // BLOCK
    return pl.pallas_call(
        functools.partial(_ret_kernel, nc=nc),
        grid=(bsz,),
        in_specs=[
            pl.BlockSpec((None, n, 4 * HW), lambda b: (b, 0, 0)),
            pl.BlockSpec((None, 2, HW, LANES), lambda b: (b, 0, 0, 0)),
            _const_spec((2, 1, HW)),
            _const_spec((2, HW, LANES)),
        ],
        out_specs=[
            pl.BlockSpec((None, n, HW), lambda b: (b, 0, 0)),
            pl.BlockSpec((None, 2, HW, DH), lambda b: (b, 0, 0, 0)),
        ],
        out_shape=[
            jax.ShapeDtypeStruct((bsz, n, HW), bf16),
            jax.ShapeDtypeStruct((bsz, 2, HW, DH), f32),
        ],
        scratch_shapes=[
            pltpu.VMEM((n, HW), f32),
            pltpu.VMEM((n, HW), f32),
            pltpu.VMEM((2, HW, HW), f32),
            pltpu.VMEM((2, BLOCK, HW), f32),
            pltpu.VMEM((2, BLOCK, HW), f32),
            pltpu.VMEM((2, HW, HW), f32),
            pltpu.VMEM((2, BLOCK, N_HEADS * BLOCK), f32),
        ],
        compiler_params=_params(1),
        name="retention",
    )(rt.reshape(bsz, n, 4 * HW), s0, dl, dr)


def _attn_kernel(q_ref, k_ref, v_ref, ck_ref, cv_ref, cos_ref, sin_ref, sk_ref, o_ref, kt_s, vt_s,
                 *, n_ctx, n, band):
    i = pl.program_id(1)
    lane = lax.broadcasted_iota(jnp.int32, (1, LANES), 1)
    lo = lane < DH
    first = (lane % (DH // 2)) < (DH // 4)
    gmasks = _head_lane_masks()

    def tile4(x):
        xr = pltpu.roll(x, DH, 1)
        a2 = jnp.where(lo, x, xr).astype(bf16)
        b2 = jnp.where(lo, xr, x).astype(bf16)
        return jnp.concatenate([a2, a2], axis=1), jnp.concatenate([b2, b2], axis=1)

    def rope(x, cos, sin):
        sw = jnp.where(first, pltpu.roll(x, LANES - DH // 4, 1), pltpu.roll(x, DH // 4, 1))
        return x * cos + sw * sin

    @pl.when(i == 0)
    def _prepare_keys():
        for r in range(0, n_ctx, BLOCK):
            k0, k1 = tile4(ck_ref[r:r + BLOCK, :])
            v0, v1 = tile4(cv_ref[r:r + BLOCK, :])
            kt_s[0, r:r + BLOCK, :] = k0
            kt_s[1, r:r + BLOCK, :] = k1
            vt_s[0, r:r + BLOCK, :] = v0
            vt_s[1, r:r + BLOCK, :] = v1
        if band:
            zeros = jnp.zeros((BLOCK, HW), bf16)
            for j in range(A_KV_HEADS):
                for s in (kt_s, vt_s):
                    s[j, n_ctx:n_ctx + BLOCK, :] = zeros
                    s[j, n_ctx + BLOCK + n:n_ctx + 2 * BLOCK + n, :] = zeros

            def body(c, carry):
                r0 = pl.multiple_of(c * BLOCK, BLOCK)
                kk = rope(k_ref[pl.ds(r0, BLOCK), :], cos_ref[pl.ds(r0, BLOCK), :], sin_ref[pl.ds(r0, BLOCK), :])
                k0, k1 = tile4(kk)
                v0, v1 = tile4(v_ref[pl.ds(r0, BLOCK), :])
                w0 = pl.multiple_of(n_ctx + BLOCK + r0, BLOCK)
                kt_s[0, pl.ds(w0, BLOCK), :] = k0
                kt_s[1, pl.ds(w0, BLOCK), :] = k1
                vt_s[0, pl.ds(w0, BLOCK), :] = v0
                vt_s[1, pl.ds(w0, BLOCK), :] = v1
                return carry

            lax.fori_loop(0, n // BLOCK, body, 0)

    q = q_ref[...]
    if band:
        q0 = pl.multiple_of(i * BLOCK, BLOCK)
        cos = cos_ref[pl.ds(q0, BLOCK), :]
        sin = sin_ref[pl.ds(q0, BLOCK), :]
        q = jnp.concatenate([rope(q[:, LANES * t:LANES * (t + 1)], cos, sin) for t in range(A_W // LANES)], axis=1)
        qi = lax.broadcasted_iota(jnp.int32, (BLOCK, 3 * BLOCK), 0)
        kj = lax.broadcasted_iota(jnp.int32, (BLOCK, 3 * BLOCK), 1)
        tok = kj + (i - 1) * BLOCK
        ok = (kj >= qi) & (kj <= qi + 2 * WINDOW) & (tok >= 0) & (tok < n)
        bias = jnp.where(ok, 0.0, NEG)
        bias4 = jnp.concatenate([bias] * A_GROUP, axis=0)
        b0 = pl.multiple_of(n_ctx + i * BLOCK, BLOCK)
    q = q * (DH ** -0.5)

    for j in range(A_KV_HEADS):
        q256 = q[:, HW * j:HW * (j + 1)]
        qs = jnp.concatenate([jnp.where(gmasks[g], q256, 0.0) for g in range(A_GROUP)], axis=0).astype(bf16)
        sink = jnp.concatenate(
            [jnp.broadcast_to(sk_ref[A_GROUP * j + g:A_GROUP * j + g + 1, 0:1], (BLOCK, 1)) for g in range(A_GROUP)],
            axis=0)
        s1 = _dot_nt(qs, kt_s[j, 0:n_ctx, :])
        m = jnp.maximum(jnp.max(s1, axis=1, keepdims=True), sink)
        if band:
            s2 = _dot_nt(qs, kt_s[j, pl.ds(b0, 3 * BLOCK), :]) + bias4
            m = jnp.maximum(m, jnp.max(s2, axis=1, keepdims=True))
        p1 = jnp.exp(s1 - m)
        den = jnp.sum(p1, axis=1, keepdims=True) + jnp.exp(sink - m)
        acc = _dot(p1.astype(bf16), vt_s[j, 0:n_ctx, :])
        if band:
            p2 = jnp.exp(s2 - m)
            den = den + jnp.sum(p2, axis=1, keepdims=True)
            acc = acc + _dot(p2.astype(bf16), vt_s[j, pl.ds(b0, 3 * BLOCK), :])
        acc = acc * (1.0 / den)
        out = jnp.zeros((BLOCK, HW), f32)
        for g in range(A_GROUP):
            out = jnp.where(gmasks[g], acc[BLOCK * g:BLOCK * (g + 1), :], out)
        o_ref[:, HW * j:HW * (j + 1)] = out.astype(o_ref.dtype)


def _attention(at, ck, cv, ck_spec, cv_spec, cos, sin, sk, bsz, n, n_ctx, band):
    nq = n // BLOCK
    s_tot = n_ctx + (n + 2 * BLOCK if band else 0)
    at3 = at.reshape(bsz, n, A_W + 2 * A_KV_W)
    kcol = A_W // A_KV_W
    return pl.pallas_call(
        functools.partial(_attn_kernel, n_ctx=n_ctx, n=n, band=band),
        grid=(bsz, nq),
        in_specs=[
            pl.BlockSpec((None, BLOCK, A_W), lambda b, i: (b, i, 0)),
            pl.BlockSpec((None, n, A_KV_W), lambda b, i: (b, 0, kcol)),
            pl.BlockSpec((None, n, A_KV_W), lambda b, i: (b, 0, kcol + 1)),
            ck_spec,
            cv_spec,
            _const_spec(cos.shape),
            _const_spec(sin.shape),
            _const_spec((A_HEADS, LANES)),
        ],
        out_specs=pl.BlockSpec((None, BLOCK, A_W), lambda b, i: (b, i, 0)),
        out_shape=jax.ShapeDtypeStruct((bsz, n, A_W), bf16),
        scratch_shapes=[
            pltpu.VMEM((A_KV_HEADS, s_tot, HW), bf16),
            pltpu.VMEM((A_KV_HEADS, s_tot, HW), bf16),
        ],
        compiler_params=_params(2),
        name="attention",
    )(at3, at3, at3, ck, cv, cos, sin, sk)


def _rope_tables(n):
    tok = np.arange(n)
    pos = np.stack([tok // GRID_W, tok % GRID_W], axis=1).astype(np.float32)
    quarter = DH // 4
    freqs = np.power(np.float32(ROPE_BASE), -np.arange(quarter, dtype=np.float32) / np.float32(quarter))
    lane = np.arange(LANES) % DH
    axis = lane // (DH // 2)
    fidx = lane % quarter
    sign = np.where((lane % (DH // 2)) < quarter, -1.0, 1.0)
    ang = (pos[:, axis] * freqs.astype(np.float32)[fidx][None, :]).astype(np.float32).astype(np.float64)
    return jnp.asarray(np.cos(ang), f32), jnp.asarray(np.sin(ang) * sign[None, :], f32)


CONV_PAD = 16


def _conv_kernel(u_ref, w_ref, b_ref, lg_ref, lb_ref, y_ref, z_s, *, n):
    nc = n // BLOCK
    z_s[0:CONV_PAD, :] = jnp.zeros((CONV_PAD, HW), f32)
    z_s[CONV_PAD + n:2 * CONV_PAD + n, :] = jnp.zeros((CONV_PAD, HW), f32)

    def glu(c, carry):
        r0 = pl.multiple_of(c * BLOCK, BLOCK)
        blk = u_ref[pl.ds(r0, BLOCK), :].astype(f32)
        z_s[pl.ds(pl.multiple_of(r0 + CONV_PAD, SUBLANES), BLOCK), :] = blk[:, 0:HW] * _sigmoid(blk[:, HW:2 * HW])
        return carry

    lax.fori_loop(0, nc, glu, 0)

    first = CONV_PAD - CONV_K // 2
    span = BLOCK + 2 * CONV_PAD

    def tile(c, carry):
        r0 = pl.multiple_of(c * BLOCK, BLOCK)
        win = z_s[pl.ds(r0, span), :]
        acc = jnp.zeros((BLOCK, HW), f32)
        for r in range(SUBLANES):
            taps = [kk for kk in range(CONV_K) if (first + kk) % SUBLANES == r]
            if not taps:
                continue
            shifted = win[r:r + span - SUBLANES, :]
            for kk in taps:
                a = (first + kk) // SUBLANES * SUBLANES
                acc = acc + shifted[a:a + BLOCK, :] * w_ref[kk:kk + 1, :]
        zc = acc + b_ref[...]
        mu = jnp.mean(zc, axis=-1, keepdims=True)
        dev = zc - mu
        var = jnp.mean(dev * dev, axis=-1, keepdims=True)
        t = dev * lax.rsqrt(var + EPS) * lg_ref[...] + lb_ref[...]
        y_ref[pl.ds(r0, BLOCK), :] = (t * _sigmoid(t)).astype(y_ref.dtype)
        return carry

    lax.fori_loop(0, nc, tile, 0)


def _conv(cu, w, b, lg, lb, bsz, n):
    return pl.pallas_call(
        functools.partial(_conv_kernel, n=n),
        grid=(bsz,),
        in_specs=[
            pl.BlockSpec((None, n, 2 * HW), lambda bb: (bb, 0, 0)),
            _const_spec((CONV_K, HW)),
            _const_spec((1, HW)),
            _const_spec((1, HW)),
            _const_spec((1, HW)),
        ],
        out_specs=pl.BlockSpec((None, n, HW), lambda bb: (bb, 0, 0)),
        out_shape=jax.ShapeDtypeStruct((bsz, n, HW), bf16),
        scratch_shapes=[pltpu.VMEM((n + 2 * CONV_PAD, HW), f32)],
        compiler_params=_params(1),
        name="conv",
    )(cu.reshape(bsz, n, 2 * HW), w, b, lg, lb)


def _merge_kernel(x_ref, ym_ref, ya_ref, yc_ref, yr_ref, gt_ref, mod_ref, ng_ref, wbr_ref, wout_ref, o_ref):
    acc = None
    for i, y_ref in enumerate((ym_ref, ya_ref, yc_ref, yr_ref)):
        br = _dot(y_ref[...], wbr_ref[BR_OFFS[i]:BR_OFFS[i + 1], :])
        gate = _sigmoid(gt_ref[:, D_MODEL * i:D_MODEL * (i + 1)].astype(f32))
        acc = gate * br if acc is None else acc + gate * br
    out = _dot(acc.astype(bf16), wout_ref[...])
    r = out * lax.rsqrt(jnp.mean(out * out, axis=-1, keepdims=True) + EPS) * ng_ref[1:2, :]
    o_ref[...] = x_ref[...] + mod_ref[2:3, :] * r


def _merge(x2d, ym, ya, yc, yr, gt, mod4, ng, wbr, wout, layer, row_of_tile):
    m = x2d.shape[0]
    tm = ROW_TILE
    rows = lambda w: pl.BlockSpec((tm, w), lambda i: (i, 0))
    return pl.pallas_call(
        _merge_kernel,
        grid=(m // tm,),
        in_specs=[
            rows(D_MODEL), rows(HW), rows(A_W), rows(HW), rows(HW), rows(N_BRANCH * D_MODEL),
            pl.BlockSpec((None, None, 6, D_MODEL), lambda i: (layer, row_of_tile(i), 0, 0)),
            _const_spec((4, D_MODEL)),
            _const_spec((BR_OFFS[-1], D_MODEL)),
            _const_spec((D_MODEL, D_MODEL)),
        ],
        out_specs=rows(D_MODEL),
        out_shape=jax.ShapeDtypeStruct((m, D_MODEL), f32),
        compiler_params=_params(1),
        name="merge",
    )(x2d, ym.reshape(m, HW), ya.reshape(m, A_W), yc.reshape(m, HW), yr.reshape(m, HW), gt, mod4, ng, wbr, wout)


def _mlp_kernel(x_ref, mod_ref, ng_ref, w1_ref, w2_ref, o_ref):
    x = x_ref[...]
    y = x * lax.rsqrt(jnp.mean(x * x, axis=-1, keepdims=True) + EPS) * ng_ref[2:3, :]
    h = (y * (1.0 + mod_ref[4:5, :]) + mod_ref[3:4, :]).astype(bf16)
    chunk = 512
    acc = jnp.zeros(x.shape, f32)
    for c in range(0, D_FF, chunk):
        a = jnp.maximum(_dot(h, w1_ref[:, c:c + chunk]), 0.0)
        acc = acc + _dot((a * a).astype(bf16), w2_ref[c:c + chunk, :])
    r = acc * lax.rsqrt(jnp.mean(acc * acc, axis=-1, keepdims=True) + EPS) * ng_ref[3:4, :]
    o_ref[...] = x + mod_ref[5:6, :] * r


def _mlp(x2d, mod4, ng, w1, w2, layer, row_of_tile):
    m = x2d.shape[0]
    tm = ROW_TILE
    return pl.pallas_call(
        _mlp_kernel,
        grid=(m // tm,),
        in_specs=[
            pl.BlockSpec((tm, D_MODEL), lambda i: (i, 0)),
            pl.BlockSpec((None, None, 6, D_MODEL), lambda i: (layer, row_of_tile(i), 0, 0)),
            _const_spec((4, D_MODEL)),
            _const_spec((D_MODEL, D_FF)),
            _const_spec((D_FF, D_MODEL)),
        ],
        out_specs=pl.BlockSpec((tm, D_MODEL), lambda i: (i, 0)),
        out_shape=jax.ShapeDtypeStruct((m, D_MODEL), f32),
        compiler_params=_params(1),
        name="mlp",
    )(x2d, mod4, ng, w1, w2)


def _pack_w_in(w, b):
    def pack(a):
        g0 = PK_MG
        pad = lambda cols: jnp.zeros(a.shape[:-1] + (cols,), a.dtype)
        return jnp.concatenate(
            [a[..., :g0 + N_GATES], pad(LANES - N_GATES),
             a[..., g0 + N_HEADS:g0 + N_GATES], pad(LANES - N_GATES + N_HEADS),
             a[..., g0 + N_GATES:]], axis=-1)

    return pack(w).astype(bf16), pack(b)[None, :]


def _stack_state(s):
    bsz = s.shape[0]
    s = s.reshape(bsz, 2, HW, DH)
    return jnp.pad(s, ((0, 0), (0, 0), (0, 0), (0, LANES - DH)))


def _layer(x2d, bsz, n, layer, mod4, lw, cache, row_of_tile):
    is_ctx = cache is None
    mq, mg, at, cv, rt, gt = _inproj(x2d, mod4, lw['ng'], lw['w_in'], lw['b_in'], layer, row_of_tile)

    if is_ctx:
        c0 = jnp.zeros((bsz, 2, HW, LANES), f32)
        n0 = jnp.zeros((bsz, 2, 1, HW), f32)
        m0 = jnp.zeros((bsz, 2, 1, LANES), f32)
        s0 = jnp.zeros((bsz, 2, HW, LANES), f32)
    else:
        c0 = _stack_state(cache['C'])
        n0 = cache['n'].reshape(bsz, 2, 1, HW)
        m0 = jnp.stack([jnp.pad(cache['m'][:, d], ((0, 0), (2 * N_HEADS * d, LANES - N_HEADS - 2 * N_HEADS * d)))
                        for d in range(2)], axis=1)[:, :, None, :]
        s0 = _stack_state(cache['S'])
    ym, c_fin, n_fin, m_fin = _mlstm(mq, mg, c0, n0, m0, bsz, n)
    yr, s_fin = _retention(rt, s0, lw['ret_dl'], lw['ret_dr'], bsz, n)

    at3 = at.reshape(bsz, n, A_W + 2 * A_KV_W)
    kcol = A_W // A_KV_W
    if is_ctx:
        dummy = jnp.zeros((SUBLANES, LANES), f32)
        ck_spec = pl.BlockSpec((None, n, A_KV_W), lambda b, i: (b, 0, kcol))
        cv_spec = pl.BlockSpec((None, n, A_KV_W), lambda b, i: (b, 0, kcol + 1))
        ya = _attention(at, at3, at3, ck_spec, cv_spec, dummy, dummy, lw['sink'], bsz, n, n, False)
    else:
        n_ctx = cache['k'].shape[2]
        ck_spec = pl.BlockSpec((None, None, n_ctx, A_KV_W), lambda b, i: (b, layer, 0, 0))
        cos, sin = _rope_tables(n)
        ya = _attention(at, cache['k'], cache['v'], ck_spec, ck_spec, cos, sin, lw['sink'], bsz, n, n_ctx, True)

    yc = _conv(cv, lw['conv_w'], lw['conv_b'], lw['conv_ln_g'], lw['conv_ln_b'], bsz, n)

    x2d = _merge(x2d, ym, ya, yc, yr, gt, mod4, lw['ng'], lw['w_branch'], lw['w_out'], layer, row_of_tile)
    x2d = _mlp(x2d, mod4, lw['ng'], lw['w_mlp1'], lw['w_mlp2'], layer, row_of_tile)

    ctx = None
    if is_ctx:
        ctx = (at3[:, :, A_W:A_W + A_KV_W].reshape(bsz, n, A_KV_HEADS, DH),
               at3[:, :, A_W + A_KV_W:].reshape(bsz, n, A_KV_HEADS, DH),
               c_fin.reshape(bsz, 2, N_HEADS, DH, DH),
               n_fin.reshape(bsz, 2, N_HEADS, DH),
               jnp.stack([m_fin[:, d, 0, 2 * N_HEADS * d:2 * N_HEADS * d + N_HEADS] for d in range(2)], axis=1),
               s_fin.reshape(bsz, 2, N_HEADS, DH, DH))
    return x2d, ctx


def kernel(x_prompt, x_sample, c, cache_k, cache_v, state_mlstm_C, state_mlstm_n, state_mlstm_m, state_ret,
           c_ctx, w_ada, b_ada, norm_g, w_in, b_in, w_branch, w_out, attn_sink, ret_decay_logit,
           conv_w, conv_b, conv_ln_g, conv_ln_b, w_mlp1, w_mlp2):
    bsz, seq, _ = x_prompt.shape
    dbsz, dseq, _ = x_sample.shape
    past = cache_k.shape[2]

    c_rows = jnp.zeros((SUBLANES, D_MODEL), f32).at[0].set(c_ctx).at[1:1 + dbsz].set(c)
    mod4 = _ada(c_rows, w_ada, b_ada).reshape(DEPTH, SUBLANES, 6, D_MODEL)

    layers = []
    for l in range(DEPTH):
        w_pk, b_pk = _pack_w_in(w_in[l], b_in[l])
        layers.append({
            'ng': norm_g[l], 'w_in': w_pk, 'b_in': b_pk,
            'w_branch': w_branch[l].astype(bf16), 'w_out': w_out[l].astype(bf16),
            'w_mlp1': w_mlp1[l].astype(bf16), 'w_mlp2': w_mlp2[l].astype(bf16),
            'sink': jnp.broadcast_to(attn_sink[l][:, None], (A_HEADS, LANES)),
            'ret_dl': jnp.repeat(ret_decay_logit[l], DH, axis=-1).reshape(2, 1, HW),
            'ret_dr': jnp.broadcast_to(jnp.repeat(ret_decay_logit[l], DH, axis=-1)[:, :, None], (2, HW, LANES)),
            'conv_w': conv_w[l], 'conv_b': conv_b[l][None, :],
            'conv_ln_g': conv_ln_g[l][None, :], 'conv_ln_b': conv_ln_b[l][None, :],
        })

    xp = x_prompt.reshape(bsz * seq, D_MODEL)
    ctxs = []
    for l in range(DEPTH):
        xp, ctx = _layer(xp, bsz, seq, l, mod4, layers[l], None, lambda i: 0)
        ctxs.append(ctx)

    xs = x_sample.reshape(dbsz * dseq, D_MODEL)
    tiles_per_seq = dseq // ROW_TILE
    ck = cache_k.reshape(dbsz, DEPTH, past, A_KV_W)
    cv = cache_v.reshape(dbsz, DEPTH, past, A_KV_W)
    for l in range(DEPTH):
        cache = {'k': ck, 'v': cv, 'C': state_mlstm_C[:, l], 'n': state_mlstm_n[:, l],
                 'm': state_mlstm_m[:, l], 'S': state_ret[:, l]}
        xs, _ = _layer(xs, dbsz, dseq, l, mod4, layers[l], cache, lambda i: 1 + i // tiles_per_seq)

    stack = lambda j: jnp.stack([ctxs[l][j] for l in range(DEPTH)], axis=1)
    return (xp.reshape(bsz, seq, D_MODEL), xs.reshape(dbsz, dseq, D_MODEL),
            stack(0), stack(1), stack(2), stack(3), stack(4), stack(5))
```

```python
import functools

import numpy as np
import jax
import jax.numpy as jnp
from jax import lax
from jax.experimental import pallas as pl
from jax.experimental.pallas import tpu as pltpu

f32 = jnp.float32
bf16 = jnp.bfloat16

D_MODEL = 1024
DEPTH = 2
GRID_W = 64
BLOCK = 128
N_HEADS = 4
DH = 64
HW = N_HEADS * DH
A_HEADS = 8
A_KV_HEADS = 2
A_GROUP = A_HEADS // A_KV_HEADS
A_W = A_HEADS * DH
A_KV_W = A_KV_HEADS * DH
WINDOW = 128
ROPE_BASE = 10000.0
CONV_K = 31
D_FF = 4 * D_MODEL
N_BRANCH = 4
EPS = 1e-6
NEG = -1e30
LANES = 128
SUBLANES = 8

N_GATES = 4 * N_HEADS
PK_MQ = 0
PK_MG = PK_MQ + 4 * HW
PK_AT = PK_MG + 2 * LANES
PK_CV = PK_AT + A_W + 2 * A_KV_W
PK_RT = PK_CV + 2 * HW
PK_GT = PK_RT + 4 * HW
PK_W = PK_GT + N_BRANCH * D_MODEL
BR_OFFS = (0, HW, HW + A_W, 2 * HW + A_W, 3 * HW + A_W)

VMEM_LIMIT = 56 * 1024 * 1024
ROW_TILE = 512


def _sigmoid(x):
    return 0.5 * (jnp.tanh(0.5 * x) + 1.0)


def _log_sigmoid(x):
    return jnp.minimum(x, 0.0) - jnp.log1p(jnp.exp(-jnp.abs(x)))


def _dot(a, b):
    return jnp.dot(a, b, preferred_element_type=f32)


def _dot_nt(a, b):
    return lax.dot_general(a, b, (((1,), (1,)), ((), ())), preferred_element_type=f32)


def _split3(x):
    x1 = x.astype(bf16)
    r1 = x - x1.astype(f32)
    x2 = r1.astype(bf16)
    x3 = (r1 - x2.astype(f32)).astype(bf16)
    return x1, x2, x3


def _dot2_r(t, x):
    x1 = x.astype(bf16)
    x2 = (x - x1.astype(f32)).astype(bf16)
    return _dot(t, x1) + _dot(t, x2)


def _dot3_l(x, t):
    x1, x2, x3 = _split3(x)
    return _dot(x1, t) + _dot(x2, t) + _dot(x3, t)


def _dot2_l(x, t):
    x1 = x.astype(bf16)
    x2 = (x - x1.astype(f32)).astype(bf16)
    return _dot(x1, t) + _dot(x2, t)


def _const_spec(shape):
    nd = len(shape)
    return pl.BlockSpec(shape, lambda *_: (0,) * nd, pipeline_mode=pl.Buffered(1))


def _params(n_axes):
    return pltpu.CompilerParams(dimension_semantics=("arbitrary",) * n_axes, vmem_limit_bytes=VMEM_LIMIT)


def _ada_kernel(c_ref, w_ref, b_ref, o_ref):
    c = c_ref[...]
    s = (c * _sigmoid(c)).astype(bf16)
    o_ref[...] = _dot(s, w_ref[...].astype(bf16)) + b_ref[...]


def _ada(c_rows, w_ada, b_ada):
    tn = 1536
    n_out = 6 * D_MODEL
    return pl.pallas_call(
        _ada_kernel,
        grid=(DEPTH, n_out // tn),
        in_specs=[
            pl.BlockSpec((SUBLANES, D_MODEL), lambda l, j: (0, 0)),
            pl.BlockSpec((None, D_MODEL, tn), lambda l, j: (l, 0, j)),
            pl.BlockSpec((None, 1, tn), lambda l, j: (l, 0, j)),
        ],
        out_specs=pl.BlockSpec((None, SUBLANES, tn), lambda l, j: (l, 0, j)),
        out_shape=jax.ShapeDtypeStruct((DEPTH, SUBLANES, n_out), f32),
        compiler_params=_params(2),
        name="ada",
    )(c_rows, w_ada, b_ada.reshape(DEPTH, 1, n_out))


def _inproj_kernel(x_ref, mod_ref, ng_ref, w_ref, b_ref, mq_ref, mg_ref, at_ref, cv_ref, rt_ref, gt_ref):
    x = x_ref[...]
    y = x * lax.rsqrt(jnp.mean(x * x, axis=-1, keepdims=True) + EPS) * ng_ref[0:1, :]
    h = (y * (1.0 + mod_ref[1:2, :]) + mod_ref[0:1, :]).astype(bf16)
    chunk = 512
    for ref, base in ((mq_ref, PK_MQ), (mg_ref, PK_MG), (at_ref, PK_AT), (cv_ref, PK_CV), (rt_ref, PK_RT),
                      (gt_ref, PK_GT)):
        total = ref.shape[-1]
        for c in range(0, total, chunk):
            w = min(chunk, total - c)
            r = _dot(h, w_ref[:, base + c:base + c + w]) + b_ref[:, base + c:base + c + w]
            ref[:, c:c + w] = r.astype(ref.dtype)


def _inproj(x2d, mod4, ng, w_pk, b_pk, layer, row_of_tile):
    m = x2d.shape[0]
    tm = ROW_TILE
    widths = (4 * HW, 2 * LANES, A_W + 2 * A_KV_W, 2 * HW, 4 * HW, N_BRANCH * D_MODEL)
    dtypes = (bf16, f32, f32, bf16, bf16, bf16)
    return pl.pallas_call(
        _inproj_kernel,
        grid=(m // tm,),
        in_specs=[
            pl.BlockSpec((tm, D_MODEL), lambda i: (i, 0)),
            pl.BlockSpec((None, None, 6, D_MODEL), lambda i: (layer, row_of_tile(i), 0, 0)),
            _const_spec((4, D_MODEL)),
            pl.BlockSpec((None, D_MODEL, PK_W), lambda i: (layer, 0, 0), pipeline_mode=pl.Buffered(1)),
            pl.BlockSpec((None, 1, PK_W), lambda i: (layer, 0, 0), pipeline_mode=pl.Buffered(1)),
        ],
        out_specs=[pl.BlockSpec((tm, w), lambda i: (i, 0)) for w in widths],
        out_shape=[jax.ShapeDtypeStruct((m, w), dt) for w, dt in zip(widths, dtypes)],
        compiler_params=_params(1),
        name="inproj",
    )(x2d, mod4, ng, w_pk, b_pk)


def _head_lane_masks():
    lane_head = lax.broadcasted_iota(jnp.int32, (1, HW), 1) // DH
    return [lane_head == h for h in range(N_HEADS)]


def _block_diag_mask():
    r = lax.broadcasted_iota(jnp.int32, (HW, HW), 0) // DH
    c = lax.broadcasted_iota(jnp.int32, (HW, HW), 1) // DH
    return r == c


def _expand_state(x):
    y = x + pltpu.roll(x, DH, 1)
    return jnp.where(_block_diag_mask(), jnp.concatenate([y, y], axis=1), 0.0)


def _compact_state(c):
    a = c[:, 0:LANES] + c[:, LANES:2 * LANES]
    return (a + pltpu.roll(a, DH, 1))[:, 0:DH]


def _head_block_kt(kt):
    zeros = jnp.zeros((DH, BLOCK), kt.dtype)
    cols = []
    for h in range(N_HEADS):
        cols.append(jnp.concatenate([kt[DH * h:DH * (h + 1)] if hh == h else zeros for hh in range(N_HEADS)], axis=0))
    return jnp.concatenate(cols, axis=1)


def _head_block_v(v, masks):
    return [jnp.where(masks[h], v, jnp.zeros_like(v)) for h in range(N_HEADS)]


def _mlstm_kernel(x_ref, g_ref, c0_ref, n0_ref, m0_ref, y_ref, cn_ref, nn_ref, mn_ref,
                  hf_s, hb_s, c_s, n_s, m_s, *, nc):
    wide = N_HEADS * BLOCK
    row = lax.broadcasted_iota(jnp.int32, (BLOCK, BLOCK), 0)
    col = lax.broadcasted_iota(jnp.int32, (BLOCK, BLOCK), 1)
    tsum = (jnp.where(row >= col, 1.0, 0.0).astype(bf16), jnp.where(row <= col, 1.0, 0.0).astype(bf16))
    roww = lax.broadcasted_iota(jnp.int32, (BLOCK, wide), 0)
    colw = lax.broadcasted_iota(jnp.int32, (BLOCK, wide), 1) & (BLOCK - 1)
    tri4 = (roww >= colw, roww <= colw)
    masks = _head_lane_masks()
    lane = lax.broadcasted_iota(jnp.int32, (1, LANES), 1)
    lane8 = lax.broadcasted_iota(jnp.int32, (SUBLANES, BLOCK), 1)
    row8 = lax.broadcasted_iota(jnp.int32, (SUBLANES, BLOCK), 0)

    def onehot(shape, row_expr, col_expr):
        r = lax.broadcasted_iota(jnp.int32, shape, 0)
        c = lax.broadcasted_iota(jnp.int32, shape, 1)
        return jnp.where(row_expr(r) == col_expr(c), 1.0, 0.0).astype(bf16)

    g0s = (0, 2 * N_HEADS)
    valid = tuple((lane >= g0) & (lane < g0 + N_HEADS) for g0 in g0s)
    sel = tuple(onehot((LANES, wide), lambda r: r, lambda c, g0=g0: c // BLOCK + g0) for g0 in g0s)
    hexp = tuple(onehot((LANES, HW), lambda r: r, lambda c, g0=g0: c // DH + g0) for g0 in g0s)
    hselg = tuple(onehot((HW, LANES), lambda r, g0=g0: r // DH + g0, lambda c: c) for g0 in g0s)
    ones_blk = tuple(onehot((wide, LANES), lambda r, g0=g0: r // BLOCK + g0, lambda c: c) for g0 in g0s)

    for d in range(2):
        c_s[d] = _expand_state(c0_ref[d])
        n_s[d] = n0_ref[d]
        m_s[d] = m0_ref[d]

    def load(d, c):
        r0 = pl.multiple_of(c * BLOCK, BLOCK)
        return (r0, x_ref[pl.ds(r0, BLOCK), 0:3 * HW], g_ref[pl.ds(r0, BLOCK), :], c_s[d], n_s[d], m_s[d])

    def compute(d, blk, g, cmat, nrow, m_row):
        g0 = g0s[d]
        q = blk[:, 0:HW]
        k = blk[:, HW:2 * HW] * (DH ** -0.5)
        v = blk[:, 2 * HW:3 * HW]
        ga = g[:, 0:LANES]
        gb = g[:, LANES:2 * LANES]
        bc = _dot2_r(tsum[d], _log_sigmoid(gb))
        beta = ga - bc
        beta_t = beta.T
        cmx = beta
        sh = 1
        while sh < BLOCK:
            if d == 0:
                cmx = jnp.maximum(cmx, jnp.where(row >= sh, pltpu.roll(cmx, sh, 0), NEG))
            else:
                cmx = jnp.maximum(cmx, jnp.where(row < BLOCK - sh, pltpu.roll(cmx, BLOCK - sh, 0), NEG))
            sh *= 2
        mx = jnp.maximum(m_row, cmx)
        alpha = jnp.where(valid[d], -mx, 0.0)
        w_int = jnp.where(valid[d], jnp.exp(m_row - mx), 0.0)
        e_nb = jnp.exp(alpha - bc)
        z = _dot2_l(alpha, sel[d])
        beta_w = jnp.concatenate([beta_t[g0 + h:g0 + h + 1, :] for h in range(N_HEADS)], axis=1)
        ktf = k.astype(f32).T
        s_wide = _dot(q, _head_block_kt(ktf.astype(bf16)))
        p = s_wide * jnp.exp(jnp.where(tri4[d], z + beta_w, NEG))
        vm = _head_block_v(v, masks)
        vblk = jnp.concatenate(
            [jnp.concatenate([vm[h], ones_blk[d][BLOCK * h:BLOCK * (h + 1)]], axis=1) for h in range(N_HEADS)], axis=0)
        res = _dot(p.astype(bf16), vblk)
        qc = _dot(q, cmat.astype(bf16))
        qn = _dot((q.astype(f32) * nrow).astype(bf16), hselg[d])
        den = res[:, HW:HW + LANES] + w_int * qn
        inv = jnp.where(valid[d], 1.0 / jnp.maximum(jnp.abs(den), e_nb), 0.0)
        fac = _dot(jnp.concatenate([w_int, inv], axis=0).astype(bf16), hexp[d])
        hout = (res[:, 0:HW] + fac[0:BLOCK] * qc) * fac[BLOCK:2 * BLOCK]
        last = BLOCK - 1 if d == 0 else 0
        a_last = alpha[last:last + 1, :]
        m_new = jnp.where(valid[d], bc[last:last + 1, :] - a_last, 0.0)
        dec = jnp.exp(m_row + a_last)
        wr = jnp.zeros((SUBLANES, BLOCK), f32)
        kws = []
        for h in range(N_HEADS):
            w_row = jnp.exp(beta_t[g0 + h:g0 + h + 1, :] + a_last[:, g0 + h:g0 + h + 1])
            kws.append(ktf[DH * h:DH * (h + 1), :] * w_row)
            wr = jnp.where(row8 == h, w_row, wr)
        upd = _dot(jnp.concatenate(kws, axis=0).astype(bf16), v)
        wk = _dot(wr.astype(bf16), k)
        nnew = jnp.zeros((1, HW), f32)
        cnew = []
        for h in range(N_HEADS):
            rs = slice(DH * h, DH * (h + 1))
            dh = dec[:, g0 + h:g0 + h + 1]
            cnew.append(dh * cmat[rs, :] + jnp.where(masks[h], upd[rs, :], 0.0))
            nnew = jnp.where(masks[h], dh * nrow + wk[h:h + 1, :], nnew)
        return hout, jnp.concatenate(cnew, axis=0), nnew, m_new

    def body(i, carry):
        loaded = [load(0, i), load(1, nc - 1 - i)]
        outs = [compute(d, *loaded[d][1:]) for d in range(2)]
        for d in range(2):
            hout, cnew, nnew, m_new = outs[d]
            (hf_s, hb_s)[d][pl.ds(loaded[d][0], BLOCK), :] = hout
            c_s[d] = cnew
            n_s[d] = nnew
            m_s[d] = m_new
        return carry

    lax.fori_loop(0, nc, body, 0)

    def fin(c, carry):
        r0 = pl.multiple_of(c * BLOCK, BLOCK)
        o = x_ref[pl.ds(r0, BLOCK), 3 * HW:4 * HW].astype(f32)
        hsum = hf_s[pl.ds(r0, BLOCK), :] + hb_s[pl.ds(r0, BLOCK), :]
        y_ref[pl.ds(r0, BLOCK), :] = (_sigmoid(o) * hsum).astype(y_ref.dtype)
        return carry

    lax.fori_loop(0, nc, fin, 0)

    for d in range(2):
        cn_ref[d] = _compact_state(c_s[d])
        nn_ref[d] = n_s[d]
        mn_ref[d] = m_s[d]


def _mlstm(mq, mg, c0, n0, m0, bsz, n):
    nc = n // BLOCK
    state_spec = lambda shape: pl.BlockSpec((None,) + shape, lambda b: (b,) + (0,) * len(shape))
    return pl.pallas_call(
        functools.partial(_mlstm_kernel, nc=nc),
        grid=(bsz,),
        in_specs=[
            pl.BlockSpec((None, n, 4 * HW), lambda b: (b, 0, 0)),
            pl.BlockSpec((None, n, 2 * LANES), lambda b: (b, 0, 0)),
            state_spec((2, HW, LANES)),
            state_spec((2, 1, HW)),
            state_spec((2, 1, LANES)),
        ],
        out_specs=[
            pl.BlockSpec((None, n, HW), lambda b: (b, 0, 0)),
            state_spec((2, HW, DH)),
            state_spec((2, 1, HW)),
            state_spec((2, 1, LANES)),
        ],
        out_shape=[
            jax.ShapeDtypeStruct((bsz, n, HW), bf16),
            jax.ShapeDtypeStruct((bsz, 2, HW, DH), f32),
            jax.ShapeDtypeStruct((bsz, 2, 1, HW), f32),
            jax.ShapeDtypeStruct((bsz, 2, 1, LANES), f32),
        ],
        scratch_shapes=[
            pltpu.VMEM((n, HW), f32),
            pltpu.VMEM((n, HW), f32),
            pltpu.VMEM((2, HW, HW), f32),
            pltpu.VMEM((2, 1, HW), f32),
            pltpu.VMEM((2, 1, LANES), f32),
        ],
        compiler_params=_params(1),
        name="mlstm",
    )(mq.reshape(bsz, n, 4 * HW), mg.reshape(bsz, n, 2 * LANES), c0, n0, m0)


def _ret_kernel(x_ref, s0_ref, dl_ref, dr_ref, y_ref, sn_ref, of_s, ob_s, s_s, qd_s, kd_s, cd_s, dm_s, *, nc):
    masks = _head_lane_masks()
    bdm = _block_diag_mask()
    rowf = lax.broadcasted_iota(jnp.int32, (BLOCK, HW), 0).astype(f32)
    rel = (lax.broadcasted_iota(jnp.int32, (BLOCK, BLOCK), 0) -
           lax.broadcasted_iota(jnp.int32, (BLOCK, BLOCK), 1)).astype(f32)
    avg = jnp.where(bdm, 1.0 / DH, 0.0).astype(bf16)

    for d in range(2):
        lg = _log_sigmoid(dl_ref[d])
        if d == 0:
            qd_s[d] = jnp.exp((rowf + 1.0) * lg)
            kd_s[d] = jnp.exp((BLOCK - 1.0 - rowf) * lg)
            reld = rel
        else:
            qd_s[d] = jnp.exp((BLOCK - rowf) * lg)
            kd_s[d] = jnp.exp(rowf * lg)
            reld = -rel
        cdh = jnp.exp(float(BLOCK) * _log_sigmoid(dr_ref[d]))
        cd_s[d] = jnp.concatenate([cdh, cdh], axis=1)
        for h in range(N_HEADS):
            lgh = lg[:, DH * h:DH * h + 1]
            dm_s[d, :, BLOCK * h:BLOCK * (h + 1)] = jnp.where(reld >= 0.0, jnp.exp(jnp.maximum(reld, 0.0) * lgh), 0.0)
        s_s[d] = _expand_state(s0_ref[d])

    def compute(d, blk, sm):
        q = blk[:, 0:HW]
        k = blk[:, HW:2 * HW] * (DH ** -0.5)
        v = blk[:, 2 * HW:3 * HW]
        kf = k.astype(f32)
        kt = kf.T.astype(bf16)
        p = _dot(q, _head_block_kt(kt)) * dm_s[d]
        o = _dot(p.astype(bf16), jnp.concatenate(_head_block_v(v, masks), axis=0)) + qd_s[d] * _dot(q, sm.astype(bf16))
        kdt = (kf * kd_s[d]).T.astype(bf16)
        return o, cd_s[d] * sm + jnp.where(bdm, _dot(kdt, v), 0.0)

    def body(i, carry):
        r0s = [pl.multiple_of(i * BLOCK, BLOCK), pl.multiple_of((nc - 1 - i) * BLOCK, BLOCK)]
        loaded = [(x_ref[pl.ds(r0s[d], BLOCK), 0:3 * HW], s_s[d]) for d in range(2)]
        outs = [compute(d, *loaded[d]) for d in range(2)]
        for d in range(2):
            (of_s, ob_s)[d][pl.ds(r0s[d], BLOCK), :] = outs[d][0]
            s_s[d] = outs[d][1]
        return carry

    lax.fori_loop(0, nc, body, 0)

    fin_chunks = 4 if nc % 4 == 0 else (2 if nc % 2 == 0 else 1)
    rows = fin_chunks * BLOCK

    def fin(c, carry):
        r0 = pl.multiple_of(c * rows, rows)
        gate = x_ref[pl.ds(r0, rows), 3 * HW:4 * HW].astype(f32)
        o = of_s[pl.ds(r0, rows), :] + ob_s[pl.ds(r0, rows), :]
        dev = o - _dot3_l(o, avg)
        var = _dot2_l(dev * dev, avg)
        y = gate * _sigmoid(gate) * (dev * lax.rsqrt(var + EPS))
        y_ref[pl.ds(r0, rows), :] = y.astype(y_ref.dtype)
        return carry

    lax.fori_loop(0, nc // fin_chunks, fin, 0)

    for d in range(2):
        sn_ref[d] = _compact_state(s_s[d])


def _retention(rt, s0, dl, dr, bsz, n):
    nc = n // BLOCK
    return pl.pallas_call(
        functools.partial(_ret_kernel, nc=nc),
        grid=(bsz,),
        in_specs=[
            pl.BlockSpec((None, n, 4 * HW), lambda b: (b, 0, 0)),
            pl.BlockSpec((None, 2, HW, LANES), lambda b: (b, 0, 0, 0)),
            _const_spec((2, 1, HW)),
            _const_spec((2, HW, LANES)),
        ],
        out_specs=[
            pl.BlockSpec((None, n, HW), lambda b: (b, 0, 0)),
            pl.BlockSpec((None, 2, HW, DH), lambda b: (b, 0, 0, 0)),
        ],
        out_shape=[
            jax.ShapeDtypeStruct((bsz, n, HW), bf16),
            jax.ShapeDtypeStruct((bsz, 2, HW, DH), f32),
        ],
        scratch_shapes=[
            pltpu.VMEM((n, HW), f32),
            pltpu.VMEM((n, HW), f32),
            pltpu.VMEM((2, HW, HW), f32),
            pltpu.VMEM((2, BLOCK, HW), f32),
            pltpu.VMEM((2, BLOCK, HW), f32),
            pltpu.VMEM((2, HW, HW), f32),
            pltpu.VMEM((2, BLOCK, N_HEADS * BLOCK), f32),
        ],
        compiler_params=_params(1),
        name="retention",
    )(rt.reshape(bsz, n, 4 * HW), s0, dl, dr)


def _attn_kernel(q_ref, k_ref, v_ref, ck_ref, cv_ref, cos_ref, sin_ref, sk_ref, o_ref, kt_s, vt_s,
                 *, n_ctx, n, band, qb):
    i = pl.program_id(1)
    lane = lax.broadcasted_iota(jnp.int32, (1, LANES), 1)
    lo = lane < DH
    first = (lane % (DH // 2)) < (DH // 4)
    gmasks = _head_lane_masks()
    n_cblk = n_ctx // BLOCK
    wide = A_GROUP * qb

    def tile4(x):
        xr = pltpu.roll(x, DH, 1)
        a2 = jnp.where(lo, x, xr).astype(bf16)
        b2 = jnp.where(lo, xr, x).astype(bf16)
        return jnp.concatenate([a2, a2], axis=1), jnp.concatenate([b2, b2], axis=1)

    def rope(x, cos, sin):
        sw = jnp.where(first, pltpu.roll(x, LANES - DH // 4, 1), pltpu.roll(x, DH // 4, 1))
        return x * cos + sw * sin

    def put_block(blk, r0, kk, vv):
        k0, k1 = tile4(kk)
        kt_s[0, pl.ds(r0, BLOCK), :] = k0
        kt_s[1, pl.ds(r0, BLOCK), :] = k1
        vt = vv.T
        for j in range(A_KV_HEADS):
            vj = vt[DH * j:DH * (j + 1)]
            vt_s[j, blk] = jnp.concatenate([vj, vj], axis=0).astype(bf16)

    @pl.when(i == 0)
    def _prepare_keys():
        for c in range(n_cblk):
            put_block(c, c * BLOCK, ck_ref[c * BLOCK:(c + 1) * BLOCK, :], cv_ref[c * BLOCK:(c + 1) * BLOCK, :])
        if band:
            nb = n // BLOCK
            for blk in (n_cblk, n_cblk + 1 + nb):
                for j in range(A_KV_HEADS):
                    kt_s[j, blk * BLOCK:(blk + 1) * BLOCK, :] = jnp.zeros((BLOCK, HW), bf16)
                    vt_s[j, blk] = jnp.zeros((BLOCK, BLOCK), bf16)

            def body(c, carry):
                r0 = pl.multiple_of(c * BLOCK, BLOCK)
                kk = rope(k_ref[pl.ds(r0, BLOCK), :], cos_ref[pl.ds(r0, BLOCK), :], sin_ref[pl.ds(r0, BLOCK), :])
                put_block(n_cblk + 1 + c, pl.multiple_of(n_ctx + BLOCK + r0, BLOCK), kk, v_ref[pl.ds(r0, BLOCK), :])
                return carry

            lax.fori_loop(0, nb, body, 0)

    q = q_ref[...]
    if band:
        q0 = pl.multiple_of(i * BLOCK, BLOCK)
        cos = cos_ref[pl.ds(q0, BLOCK), :]
        sin = sin_ref[pl.ds(q0, BLOCK), :]
        q = jnp.concatenate([rope(q[:, LANES * t:LANES * (t + 1)], cos, sin) for t in range(A_W // LANES)], axis=1)
        kj = lax.broadcasted_iota(jnp.int32, (3 * BLOCK, BLOCK), 0)
        qi = lax.broadcasted_iota(jnp.int32, (3 * BLOCK, BLOCK), 1)
        tok = kj + (i - 1) * BLOCK
        ok = (kj >= qi) & (kj <= qi + 2 * WINDOW) & (tok >= 0) & (tok < n)
        bias = jnp.where(ok, 0.0, NEG)
        bias4 = jnp.concatenate([bias] * A_GROUP, axis=1)
        b0 = pl.multiple_of(n_ctx + i * BLOCK, BLOCK)
        vband = [vt_s[j, pl.ds(n_cblk + i, 3)] for j in range(A_KV_HEADS)]
    q = q * (DH ** -0.5)
    row_lo = lax.broadcasted_iota(jnp.int32, (2 * DH, qb), 0) < DH

    outs = []
    for j in range(A_KV_HEADS):
        q256 = q[:, HW * j:HW * (j + 1)]
        qs = jnp.concatenate([jnp.where(gmasks[g], q256, 0.0) for g in range(A_GROUP)], axis=0).astype(bf16)
        sink = jnp.concatenate(
            [jnp.broadcast_to(sk_ref[A_GROUP * j + g:A_GROUP * j + g + 1, 0:1], (1, qb)) for g in range(A_GROUP)],
            axis=1)
        s1 = _dot_nt(kt_s[j, 0:n_ctx, :], qs)
        m = jnp.maximum(jnp.max(s1, axis=0, keepdims=True), sink)
        if band:
            s2 = _dot_nt(kt_s[j, pl.ds(b0, 3 * BLOCK), :], qs) + bias4
            m = jnp.maximum(m, jnp.max(s2, axis=0, keepdims=True))
        p1 = jnp.exp(s1 - m)
        den = jnp.sum(p1, axis=0, keepdims=True) + jnp.exp(sink - m)
        vctx = jnp.concatenate([vt_s[j, c] for c in range(n_cblk)], axis=1)
        acc = _dot(vctx, p1.astype(bf16))
        if band:
            p2 = jnp.exp(s2 - m)
            den = den + jnp.sum(p2, axis=0, keepdims=True)
            acc = acc + _dot(jnp.concatenate([vband[j][t] for t in range(3)], axis=1), p2.astype(bf16))
        acc = acc * (1.0 / den)
        for g in range(0, A_GROUP, 2):
            pair = jnp.where(row_lo, acc[:, qb * g:qb * (g + 1)], acc[:, qb * (g + 1):qb * (g + 2)])
            outs.append(pair.T)
    o_ref[...] = jnp.concatenate(outs, axis=1).astype(o_ref.dtype)


def _attention(at, ck, cv, ck_spec, cv_spec, cos, sin, sk, bsz, n, n_ctx, band):
    qb = BLOCK if band else min(n, 2 * BLOCK)
    nq = n // qb
    s_tot = n_ctx + (n + 2 * BLOCK if band else 0)
    at3 = at.reshape(bsz, n, A_W + 2 * A_KV_W)
    kcol = A_W // A_KV_W
    return pl.pallas_call(
        functools.partial(_attn_kernel, n_ctx=n_ctx, n=n, band=band, qb=qb),
        grid=(bsz, nq),
        in_specs=[
            pl.BlockSpec((None, qb, A_W), lambda b, i: (b, i, 0)),
            pl.BlockSpec((None, n, A_KV_W), lambda b, i: (b, 0, kcol)),
            pl.BlockSpec((None, n, A_KV_W), lambda b, i: (b, 0, kcol + 1)),
            ck_spec,
            cv_spec,
            _const_spec(cos.shape),
            _const_spec(sin.shape),
            _const_spec((A_HEADS, LANES)),
        ],
        out_specs=pl.BlockSpec((None, qb, A_W), lambda b, i: (b, i, 0)),
        out_shape=jax.ShapeDtypeStruct((bsz, n, A_W), bf16),
        scratch_shapes=[
            pltpu.VMEM((A_KV_HEADS, s_tot, HW), bf16),
            pltpu.VMEM((A_KV_HEADS, s_tot // BLOCK, BLOCK, BLOCK), bf16),
        ],
        compiler_params=_params(2),
        name="attention",
    )(at3, at3, at3, ck, cv, cos, sin, sk)


def _rope_tables(n):
    tok = np.arange(n)
    pos = np.stack([tok // GRID_W, tok % GRID_W], axis=1).astype(np.float32)
    quarter = DH // 4
    freqs = np.power(np.float32(ROPE_BASE), -np.arange(quarter, dtype=np.float32) / np.float32(quarter))
    lane = np.arange(LANES) % DH
    axis = lane // (DH // 2)
    fidx = lane % quarter
    sign = np.where((lane % (DH // 2)) < quarter, -1.0, 1.0)
    ang = (pos[:, axis] * freqs.astype(np.float32)[fidx][None, :]).astype(np.float32).astype(np.float64)
    return jnp.asarray(np.cos(ang), f32), jnp.asarray(np.sin(ang) * sign[None, :], f32)


CONV_PAD = 16


def _conv_kernel(u_ref, w_ref, b_ref, lg_ref, lb_ref, y_ref, z_s, *, n):
    nc = n // BLOCK
    z_s[0:CONV_PAD, :] = jnp.zeros((CONV_PAD, HW), f32)
    z_s[CONV_PAD + n:2 * CONV_PAD + n, :] = jnp.zeros((CONV_PAD, HW), f32)

    def glu(c, carry):
        r0 = pl.multiple_of(c * BLOCK, BLOCK)
        blk = u_ref[pl.ds(r0, BLOCK), :].astype(f32)
        z_s[pl.ds(pl.multiple_of(r0 + CONV_PAD, SUBLANES), BLOCK), :] = blk[:, 0:HW] * _sigmoid(blk[:, HW:2 * HW])
        return carry

    lax.fori_loop(0, nc, glu, 0)

    first = CONV_PAD - CONV_K // 2
    span = BLOCK + 2 * CONV_PAD

    def tile(c, carry):
        r0 = pl.multiple_of(c * BLOCK, BLOCK)
        win = z_s[pl.ds(r0, span), :]
        acc = jnp.zeros((BLOCK, HW), f32)
        for r in range(SUBLANES):
            taps = [kk for kk in range(CONV_K) if (first + kk) % SUBLANES == r]
            if not taps:
                continue
            shifted = win[r:r + span - SUBLANES, :]
            for kk in taps:
                a = (first + kk) // SUBLANES * SUBLANES
                acc = acc + shifted[a:a + BLOCK, :] * w_ref[kk:kk + 1, :]
        zc = acc + b_ref[...]
        mu = jnp.mean(zc, axis=-1, keepdims=True)
        dev = zc - mu
        var = jnp.mean(dev * dev, axis=-1, keepdims=True)
        t = dev * lax.rsqrt(var + EPS) * lg_ref[...] + lb_ref[...]
        y_ref[pl.ds(r0, BLOCK), :] = (t * _sigmoid(t)).astype(y_ref.dtype)
        return carry

    lax.fori_loop(0, nc, tile, 0)


def _conv(cu, w, b, lg, lb, bsz, n):
    return pl.pallas_call(
        functools.partial(_conv_kernel, n=n),
        grid=(bsz,),
        in_specs=[
            pl.BlockSpec((None, n, 2 * HW), lambda bb: (bb, 0, 0)),
            _const_spec((CONV_K, HW)),
            _const_spec((1, HW)),
            _const_spec((1, HW)),
            _const_spec((1, HW)),
        ],
        out_specs=pl.BlockSpec((None, n, HW), lambda bb: (bb, 0, 0)),
        out_shape=jax.ShapeDtypeStruct((bsz, n, HW), bf16),
        scratch_shapes=[pltpu.VMEM((n + 2 * CONV_PAD, HW), f32)],
        compiler_params=_params(1),
        name="conv",
    )(cu.reshape(bsz, n, 2 * HW), w, b, lg, lb)


def _post_kernel(x_ref, ym_ref, ya_ref, yc_ref, yr_ref, gt_ref, mod_ref, ng_ref, wbr_ref, wout_ref, w1_ref, w2_ref,
                 o_ref):
    def rms(a):
        return a * lax.rsqrt(jnp.mean(a * a, axis=-1, keepdims=True) + EPS)

    acc = None
    for i, y_ref in enumerate((ym_ref, ya_ref, yc_ref, yr_ref)):
        br = _dot(y_ref[...], wbr_ref[BR_OFFS[i]:BR_OFFS[i + 1], :])
        gate = _sigmoid(gt_ref[:, D_MODEL * i:D_MODEL * (i + 1)].astype(f32))
        acc = gate * br if acc is None else acc + gate * br
    mix = _dot(acc.astype(bf16), wout_ref[...])
    x = x_ref[...] + mod_ref[2:3, :] * (rms(mix) * ng_ref[1:2, :])
    h = (rms(x) * ng_ref[2:3, :] * (1.0 + mod_ref[4:5, :]) + mod_ref[3:4, :]).astype(bf16)
    chunk = 512
    f = jnp.zeros(x.shape, f32)
    for c in range(0, D_FF, chunk):
        a = jnp.maximum(_dot(h, w1_ref[:, c:c + chunk]), 0.0)
        f = f + _dot((a * a).astype(bf16), w2_ref[c:c + chunk, :])
    o_ref[...] = x + mod_ref[5:6, :] * (rms(f) * ng_ref[3:4, :])


def _post(x2d, ym, ya, yc, yr, gt, mod4, lw, layer, row_of_tile):
    m = x2d.shape[0]
    tm = ROW_TILE
    rows = lambda w: pl.BlockSpec((tm, w), lambda i: (i, 0))
    return pl.pallas_call(
        _post_kernel,
        grid=(m // tm,),
        in_specs=[
            rows(D_MODEL), rows(HW), rows(A_W), rows(HW), rows(HW), rows(N_BRANCH * D_MODEL),
            pl.BlockSpec((None, None, 6, D_MODEL), lambda i: (layer, row_of_tile(i), 0, 0)),
            _const_spec((4, D_MODEL)),
            _const_spec((BR_OFFS[-1], D_MODEL)),
            _const_spec((D_MODEL, D_MODEL)),
            _const_spec((D_MODEL, D_FF)),
            _const_spec((D_FF, D_MODEL)),
        ],
        out_specs=rows(D_MODEL),
        out_shape=jax.ShapeDtypeStruct((m, D_MODEL), f32),
        compiler_params=_params(1),
        name="post",
    )(x2d, ym.reshape(m, HW), ya.reshape(m, A_W), yc.reshape(m, HW), yr.reshape(m, HW), gt, mod4, lw['ng'],
      lw['w_branch'], lw['w_out'], lw['w_mlp1'], lw['w_mlp2'])


def _pack_kernel(w_ref, o_ref):
    g0 = PK_MG
    lane = lax.broadcasted_iota(jnp.int32, (1, LANES), 1)
    o_ref[:, 0:g0] = w_ref[:, 0:g0].astype(bf16)
    gates = w_ref[:, g0:g0 + LANES]
    o_ref[:, g0:g0 + LANES] = jnp.where(lane < N_GATES, gates, 0.0).astype(bf16)
    shifted = pltpu.roll(gates, LANES - N_HEADS, 1)
    o_ref[:, g0 + LANES:g0 + 2 * LANES] = jnp.where(lane < N_GATES - N_HEADS, shifted, 0.0).astype(bf16)
    o_ref[:, PK_AT:PK_W] = w_ref[:, g0 + N_GATES:].astype(bf16)


def _pack_w_in(w_in, b_in):
    tr = BLOCK
    in_w = w_in.shape[-1]
    w_pk = pl.pallas_call(
        _pack_kernel,
        grid=(DEPTH, D_MODEL // tr),
        in_specs=[pl.BlockSpec((None, tr, in_w), lambda l, r: (l, r, 0))],
        out_specs=pl.BlockSpec((None, tr, PK_W), lambda l, r: (l, r, 0)),
        out_shape=jax.ShapeDtypeStruct((DEPTH, D_MODEL, PK_W), bf16),
        compiler_params=_params(2),
        name="pack_w_in",
    )(w_in)
    g0 = PK_MG
    pad = lambda cols: jnp.zeros((DEPTH, cols), f32)
    b_pk = jnp.concatenate(
        [b_in[:, :g0 + N_GATES], pad(LANES - N_GATES), b_in[:, g0 + N_HEADS:g0 + N_GATES],
         pad(LANES - N_GATES + N_HEADS), b_in[:, g0 + N_GATES:]], axis=-1)
    return w_pk, b_pk[:, None, :]


def _stack_state(s):
    bsz = s.shape[0]
    s = s.reshape(bsz, 2, HW, DH)
    return jnp.pad(s, ((0, 0), (0, 0), (0, 0), (0, LANES - DH)))


def _layer(x2d, bsz, n, layer, mod4, lw, cache, row_of_tile):
    is_ctx = cache is None
    mq, mg, at, cv, rt, gt = _inproj(x2d, mod4, lw['ng'], lw['w_in'], lw['b_in'], layer, row_of_tile)

    if is_ctx:
        c0 = jnp.zeros((bsz, 2, HW, LANES), f32)
        n0 = jnp.zeros((bsz, 2, 1, HW), f32)
        m0 = jnp.zeros((bsz, 2, 1, LANES), f32)
        s0 = jnp.zeros((bsz, 2, HW, LANES), f32)
    else:
        c0 = _stack_state(cache['C'])
        n0 = cache['n'].reshape(bsz, 2, 1, HW)
        m0 = jnp.stack([jnp.pad(cache['m'][:, d], ((0, 0), (2 * N_HEADS * d, LANES - N_HEADS - 2 * N_HEADS * d)))
                        for d in range(2)], axis=1)[:, :, None, :]
        s0 = _stack_state(cache['S'])
    ym, c_fin, n_fin, m_fin = _mlstm(mq, mg, c0, n0, m0, bsz, n)
    yr, s_fin = _retention(rt, s0, lw['ret_dl'], lw['ret_dr'], bsz, n)

    at3 = at.reshape(bsz, n, A_W + 2 * A_KV_W)
    kcol = A_W // A_KV_W
    if is_ctx:
        dummy = jnp.zeros((SUBLANES, LANES), f32)
        ck_spec = pl.BlockSpec((None, n, A_KV_W), lambda b, i: (b, 0, kcol))
        cv_spec = pl.BlockSpec((None, n, A_KV_W), lambda b, i: (b, 0, kcol + 1))
        ya = _attention(at, at3, at3, ck_spec, cv_spec, dummy, dummy, lw['sink'], bsz, n, n, False)
    else:
        n_ctx = cache['k'].shape[2]
        ck_spec = pl.BlockSpec((None, None, n_ctx, A_KV_W), lambda b, i: (b, layer, 0, 0))
        cos, sin = _rope_tables(n)
        ya = _attention(at, cache['k'], cache['v'], ck_spec, ck_spec, cos, sin, lw['sink'], bsz, n, n_ctx, True)

    yc = _conv(cv, lw['conv_w'], lw['conv_b'], lw['conv_ln_g'], lw['conv_ln_b'], bsz, n)

    x2d = _post(x2d, ym, ya, yc, yr, gt, mod4, lw, layer, row_of_tile)

    ctx = None
    if is_ctx:
        ctx = (at3[:, :, A_W:A_W + A_KV_W].reshape(bsz, n, A_KV_HEADS, DH),
               at3[:, :, A_W + A_KV_W:].reshape(bsz, n, A_KV_HEADS, DH),
               c_fin.reshape(bsz, 2, N_HEADS, DH, DH),
               n_fin.reshape(bsz, 2, N_HEADS, DH),
               jnp.stack([m_fin[:, d, 0, 2 * N_HEADS * d:2 * N_HEADS * d + N_HEADS] for d in range(2)], axis=1),
               s_fin.reshape(bsz, 2, N_HEADS, DH, DH))
    return x2d, ctx


def kernel(x_prompt, x_sample, c, cache_k, cache_v, state_mlstm_C, state_mlstm_n, state_mlstm_m, state_ret,
           c_ctx, w_ada, b_ada, norm_g, w_in, b_in, w_branch, w_out, attn_sink, ret_decay_logit,
           conv_w, conv_b, conv_ln_g, conv_ln_b, w_mlp1, w_mlp2):
    bsz, seq, _ = x_prompt.shape
    dbsz, dseq, _ = x_sample.shape
    past = cache_k.shape[2]

    c_rows = jnp.zeros((SUBLANES, D_MODEL), f32).at[0].set(c_ctx).at[1:1 + dbsz].set(c)
    mod4 = _ada(c_rows, w_ada, b_ada).reshape(DEPTH, SUBLANES, 6, D_MODEL)

    w_pk, b_pk = _pack_w_in(w_in, b_in)
    layers = []
    for l in range(DEPTH):
        layers.append({
            'ng': norm_g[l], 'w_in': w_pk, 'b_in': b_pk,
            'w_branch': w_branch[l].astype(bf16), 'w_out': w_out[l].astype(bf16),
            'w_mlp1': w_mlp1[l].astype(bf16), 'w_mlp2': w_mlp2[l].astype(bf16),
            'sink': jnp.broadcast_to(attn_sink[l][:, None], (A_HEADS, LANES)),
            'ret_dl': jnp.repeat(ret_decay_logit[l], DH, axis=-1).reshape(2, 1, HW),
            'ret_dr': jnp.broadcast_to(jnp.repeat(ret_decay_logit[l], DH, axis=-1)[:, :, None], (2, HW, LANES)),
            'conv_w': conv_w[l], 'conv_b': conv_b[l][None, :],
            'conv_ln_g': conv_ln_g[l][None, :], 'conv_ln_b': conv_ln_b[l][None, :],
        })

    xp = x_prompt.reshape(bsz * seq, D_MODEL)
    ctxs = []
    for l in range(DEPTH):
        xp, ctx = _layer(xp, bsz, seq, l, mod4, layers[l], None, lambda i: 0)
        ctxs.append(ctx)

    xs = x_sample.reshape(dbsz * dseq, D_MODEL)
    tiles_per_seq = dseq // ROW_TILE
    ck = cache_k.reshape(dbsz, DEPTH, past, A_KV_W)
    cv = cache_v.reshape(dbsz, DEPTH, past, A_KV_W)
    for l in range(DEPTH):
        cache = {'k': ck, 'v': cv, 'C': state_mlstm_C[:, l], 'n': state_mlstm_n[:, l],
                 'm': state_mlstm_m[:, l], 'S': state_ret[:, l]}
        xs, _ = _layer(xs, dbsz, dseq, l, mod4, layers[l], cache, lambda i: 1 + i // tiles_per_seq)

    stack = lambda j: jnp.stack([ctxs[l][j] for l in range(DEPTH)], axis=1)
    return (xp.reshape(bsz, seq, D_MODEL), xs.reshape(dbsz, dseq, D_MODEL),
            stack(0), stack(1), stack(2), stack(3), stack(4), stack(5))
```

```python
import functools

import numpy as np
import jax
import jax.numpy as jnp
from jax import lax
from jax.experimental import pallas as pl
from jax.experimental.pallas import tpu as pltpu

f32 = jnp.float32
bf16 = jnp.bfloat16

D_MODEL = 1024
DEPTH = 2
GRID_W = 64
BLOCK = 128
N_HEADS = 4
DH = 64
HW = N_HEADS * DH
A_HEADS = 8
A_KV_HEADS = 2
A_GROUP = A_HEADS // A_KV_HEADS
A_W = A_HEADS * DH
A_KV_W = A_KV_HEADS * DH
WINDOW = 128
ROPE_BASE = 10000.0
CONV_K = 31
D_FF = 4 * D_MODEL
N_BRANCH = 4
EPS = 1e-6
NEG = -1e30
LANES = 128
SUBLANES = 8

N_GATES = 4 * N_HEADS
PK_MQ = 0
PK_MG = PK_MQ + 4 * HW
PK_AT = PK_MG + 2 * LANES
PK_CV = PK_AT + A_W + 2 * A_KV_W
PK_RT = PK_CV + 2 * HW
PK_GT = PK_RT + 4 * HW
PK_W = PK_GT + N_BRANCH * D_MODEL
BR_OFFS = (0, HW, HW + A_W, 2 * HW + A_W, 3 * HW + A_W)

VMEM_LIMIT = 56 * 1024 * 1024
ROW_TILE = 512


def _sigmoid(x):
    return 0.5 * (jnp.tanh(0.5 * x) + 1.0)


def _log_sigmoid(x):
    return jnp.minimum(x, 0.0) - jnp.log1p(jnp.exp(-jnp.abs(x)))


def _dot(a, b):
    return jnp.dot(a, b, preferred_element_type=f32)


def _dot_nt(a, b):
    return lax.dot_general(a, b, (((1,), (1,)), ((), ())), preferred_element_type=f32)


def _dot2_r(t, x):
    x1 = x.astype(bf16)
    x2 = (x - x1.astype(f32)).astype(bf16)
    r = _dot(t, jnp.concatenate([x1, x2], axis=1))
    return r[:, 0:x.shape[1]] + r[:, x.shape[1]:]


def _dot2_l(x, t):
    x1 = x.astype(bf16)
    x2 = (x - x1.astype(f32)).astype(bf16)
    r = _dot(jnp.concatenate([x1, x2], axis=0), t)
    return r[0:x.shape[0]] + r[x.shape[0]:]


def _const_spec(shape):
    nd = len(shape)
    return pl.BlockSpec(shape, lambda *_: (0,) * nd, pipeline_mode=pl.Buffered(1))


def _params(n_axes):
    return pltpu.CompilerParams(dimension_semantics=("arbitrary",) * n_axes, vmem_limit_bytes=VMEM_LIMIT)


def _ada_kernel(c_ref, w_ref, b_ref, o_ref):
    c = c_ref[...]
    s = (c * _sigmoid(c)).astype(bf16)
    o_ref[...] = _dot(s, w_ref[...].astype(bf16)) + b_ref[...]


def _ada(c_rows, w_ada, b_ada):
    tn = 1536
    n_out = 6 * D_MODEL
    return pl.pallas_call(
        _ada_kernel,
        grid=(DEPTH, n_out // tn),
        in_specs=[
            pl.BlockSpec((SUBLANES, D_MODEL), lambda l, j: (0, 0)),
            pl.BlockSpec((None, D_MODEL, tn), lambda l, j: (l, 0, j)),
            pl.BlockSpec((None, 1, tn), lambda l, j: (l, 0, j)),
        ],
        out_specs=pl.BlockSpec((None, SUBLANES, tn), lambda l, j: (l, 0, j)),
        out_shape=jax.ShapeDtypeStruct((DEPTH, SUBLANES, n_out), f32),
        compiler_params=_params(2),
        name="ada",
    )(c_rows, w_ada, b_ada.reshape(DEPTH, 1, n_out))


def _inproj_kernel(x_ref, mod_ref, ng_ref, w_ref, b_ref, mq_ref, mg_ref, at_ref, cv_ref, rt_ref, gt_ref):
    x = x_ref[...]
    y = x * lax.rsqrt(jnp.mean(x * x, axis=-1, keepdims=True) + EPS) * ng_ref[0:1, :]
    h = (y * (1.0 + mod_ref[1:2, :]) + mod_ref[0:1, :]).astype(bf16)
    chunk = 512
    for ref, base in ((mq_ref, PK_MQ), (mg_ref, PK_MG), (at_ref, PK_AT), (cv_ref, PK_CV), (rt_ref, PK_RT),
                      (gt_ref, PK_GT)):
        total = ref.shape[-1]
        for c in range(0, total, chunk):
            w = min(chunk, total - c)
            r = _dot(h, w_ref[:, base + c:base + c + w]) + b_ref[:, base + c:base + c + w]
            ref[:, c:c + w] = r.astype(ref.dtype)


def _inproj(x2d, mod4, ng, w_pk, b_pk, layer, row_of_tile):
    m = x2d.shape[0]
    tm = ROW_TILE
    widths = (4 * HW, 2 * LANES, A_W + 2 * A_KV_W, 2 * HW, 4 * HW, N_BRANCH * D_MODEL)
    dtypes = (bf16, f32, f32, bf16, bf16, bf16)
    return pl.pallas_call(
        _inproj_kernel,
        grid=(m // tm,),
        in_specs=[
            pl.BlockSpec((tm, D_MODEL), lambda i: (i, 0)),
            pl.BlockSpec((None, None, 6, D_MODEL), lambda i: (layer, row_of_tile(i), 0, 0)),
            _const_spec((4, D_MODEL)),
            pl.BlockSpec((None, D_MODEL, PK_W), lambda i: (layer, 0, 0), pipeline_mode=pl.Buffered(1)),
            pl.BlockSpec((None, 1, PK_W), lambda i: (layer, 0, 0), pipeline_mode=pl.Buffered(1)),
        ],
        out_specs=[pl.BlockSpec((tm, w), lambda i: (i, 0)) for w in widths],
        out_shape=[jax.ShapeDtypeStruct((m, w), dt) for w, dt in zip(widths, dtypes)],
        compiler_params=_params(1),
        name="inproj",
    )(x2d, mod4, ng, w_pk, b_pk)


def _head_lane_masks():
    lane_head = lax.broadcasted_iota(jnp.int32, (1, HW), 1) // DH
    return [lane_head == h for h in range(N_HEADS)]


def _block_diag_mask():
    r = lax.broadcasted_iota(jnp.int32, (HW, HW), 0) // DH
    c = lax.broadcasted_iota(jnp.int32, (HW, HW), 1) // DH
    return r == c


def _expand_state(x):
    y = x + pltpu.roll(x, DH, 1)
    return jnp.where(_block_diag_mask(), jnp.concatenate([y, y], axis=1), 0.0)


def _compact_state(c):
    a = c[:, 0:LANES] + c[:, LANES:2 * LANES]
    return (a + pltpu.roll(a, DH, 1))[:, 0:DH]


def _head_block_kt(kt):
    zeros = jnp.zeros((DH, BLOCK), kt.dtype)
    cols = []
    for h in range(N_HEADS):
        cols.append(jnp.concatenate([kt[DH * h:DH * (h + 1)] if hh == h else zeros for hh in range(N_HEADS)], axis=0))
    return jnp.concatenate(cols, axis=1)


def _head_block_v(v, masks):
    return [jnp.where(masks[h], v, jnp.zeros_like(v)) for h in range(N_HEADS)]


def _mlstm_kernel(x_ref, g_ref, c0_ref, n0_ref, m0_ref, y_ref, cn_ref, nn_ref, mn_ref,
                  h_s, c_s, n_s, m_s, *, nc, nb):
    wide = N_HEADS * BLOCK
    row = lax.broadcasted_iota(jnp.int32, (BLOCK, BLOCK), 0)
    col = lax.broadcasted_iota(jnp.int32, (BLOCK, BLOCK), 1)
    tsum = (jnp.where(row >= col, 1.0, 0.0).astype(bf16), jnp.where(row <= col, 1.0, 0.0).astype(bf16))
    roww = lax.broadcasted_iota(jnp.int32, (BLOCK, wide), 0)
    colw = lax.broadcasted_iota(jnp.int32, (BLOCK, wide), 1) & (BLOCK - 1)
    tri4 = (roww >= colw, roww <= colw)
    masks = _head_lane_masks()
    lane = lax.broadcasted_iota(jnp.int32, (1, LANES), 1)
    row8 = lax.broadcasted_iota(jnp.int32, (SUBLANES, BLOCK), 0)

    def onehot(shape, row_expr, col_expr):
        r = lax.broadcasted_iota(jnp.int32, shape, 0)
        c = lax.broadcasted_iota(jnp.int32, shape, 1)
        return jnp.where(row_expr(r) == col_expr(c), 1.0, 0.0).astype(bf16)

    g0s = (0, 2 * N_HEADS)
    valid = tuple((lane >= g0) & (lane < g0 + N_HEADS) for g0 in g0s)
    hexp = tuple(onehot((LANES, HW), lambda r: r, lambda c, g0=g0: c // DH + g0) for g0 in g0s)
    hselg = tuple(onehot((HW, LANES), lambda r, g0=g0: r // DH + g0, lambda c: c) for g0 in g0s)

    chains = [(bb, d) for bb in range(nb) for d in range(2)]
    for bb, d in chains:
        c_s[bb, d] = _expand_state(c0_ref[bb, d])
        n_s[bb, d] = n0_ref[bb, d]
        m_s[bb, d] = m0_ref[bb, d]

    def load(bb, d, c):
        r0 = pl.multiple_of(c * BLOCK, BLOCK)
        return (r0, x_ref[bb, pl.ds(r0, BLOCK), 0:3 * HW], g_ref[bb, pl.ds(r0, BLOCK), :],
                c_s[bb, d], n_s[bb, d], m_s[bb, d])

    def compute(d, blk, g, cmat, nrow, m_row):
        g0 = g0s[d]
        q = blk[:, 0:HW]
        k = blk[:, HW:2 * HW] * (DH ** -0.5)
        v = blk[:, 2 * HW:3 * HW]
        ga = g[:, 0:LANES]
        gb = g[:, LANES:2 * LANES]
        bc = _dot2_r(tsum[d], _log_sigmoid(gb))
        beta = ga - bc
        beta_t = beta.T
        cmx = beta
        sh = 1
        while sh < BLOCK:
            if d == 0:
                cmx = jnp.maximum(cmx, jnp.where(row >= sh, pltpu.roll(cmx, sh, 0), NEG))
            else:
                cmx = jnp.maximum(cmx, jnp.where(row < BLOCK - sh, pltpu.roll(cmx, BLOCK - sh, 0), NEG))
            sh *= 2
        mx = jnp.maximum(m_row, cmx)
        alpha = jnp.where(valid[d], -mx, 0.0)
        w_int = jnp.where(valid[d], jnp.exp(m_row - mx), 0.0)
        e_nb = jnp.exp(alpha - bc)
        z = jnp.concatenate([jnp.broadcast_to(alpha[:, g0 + h:g0 + h + 1], (BLOCK, BLOCK)) for h in range(N_HEADS)],
                            axis=1)
        beta_w = jnp.concatenate([beta_t[g0 + h:g0 + h + 1, :] for h in range(N_HEADS)], axis=1)
        ktf = k.astype(f32).T
        s_wide = _dot(q, _head_block_kt(ktf.astype(bf16)))
        p = s_wide * jnp.exp(jnp.where(tri4[d], z + beta_w, NEG))
        num = _dot(p.astype(bf16), jnp.concatenate(_head_block_v(v, masks), axis=0))
        den = jnp.zeros((BLOCK, LANES), f32)
        for h in range(N_HEADS):
            den = jnp.where(lane == g0 + h, jnp.sum(p[:, BLOCK * h:BLOCK * (h + 1)], axis=1, keepdims=True), den)
        qc = _dot(q, cmat.astype(bf16))
        qn = _dot((q.astype(f32) * nrow).astype(bf16), hselg[d])
        den = den + w_int * qn
        inv = jnp.where(valid[d], 1.0 / jnp.maximum(jnp.abs(den), e_nb), 0.0)
        fac = _dot(jnp.concatenate([w_int, inv], axis=0).astype(bf16), hexp[d])
        hout = (num + fac[0:BLOCK] * qc) * fac[BLOCK:2 * BLOCK]
        last = BLOCK - 1 if d == 0 else 0
        a_last = alpha[last:last + 1, :]
        m_new = jnp.where(valid[d], bc[last:last + 1, :] - a_last, 0.0)
        dec = jnp.exp(m_row + a_last)
        wr = jnp.zeros((SUBLANES, BLOCK), f32)
        kws = []
        for h in range(N_HEADS):
            w_row = jnp.exp(beta_t[g0 + h:g0 + h + 1, :] + a_last[:, g0 + h:g0 + h + 1])
            kws.append(ktf[DH * h:DH * (h + 1), :] * w_row)
            wr = jnp.where(row8 == h, w_row, wr)
        upd = _dot(jnp.concatenate(kws, axis=0).astype(bf16), v)
        wk = _dot(wr.astype(bf16), k)
        nnew = jnp.zeros((1, HW), f32)
        cnew = []
        for h in range(N_HEADS):
            rs = slice(DH * h, DH * (h + 1))
            dh = dec[:, g0 + h:g0 + h + 1]
            cnew.append(dh * cmat[rs, :] + jnp.where(masks[h], upd[rs, :], 0.0))
            nnew = jnp.where(masks[h], dh * nrow + wk[h:h + 1, :], nnew)
        return hout, jnp.concatenate(cnew, axis=0), nnew, m_new

    def step(i, second):
        loaded = [load(bb, d, i if d == 0 else nc - 1 - i) for bb, d in chains]
        if second:
            other = [h_s[bb, pl.ds(l[0], BLOCK), :] for (bb, d), l in zip(chains, loaded)]
            ogate = [x_ref[bb, pl.ds(l[0], BLOCK), 3 * HW:4 * HW] for (bb, d), l in zip(chains, loaded)]
        outs = [compute(d, *l[1:]) for (bb, d), l in zip(chains, loaded)]
        for idx, ((bb, d), l) in enumerate(zip(chains, loaded)):
            hout, cnew, nnew, m_new = outs[idx]
            if second:
                y = _sigmoid(ogate[idx].astype(f32)) * (other[idx] + hout)
                y_ref[bb, pl.ds(l[0], BLOCK), :] = y.astype(y_ref.dtype)
            else:
                h_s[bb, pl.ds(l[0], BLOCK), :] = hout
            c_s[bb, d] = cnew
            n_s[bb, d] = nnew
            m_s[bb, d] = m_new

    def first_half(i, carry):
        step(i, False)
        return carry

    def second_half(i, carry):
        step(i, True)
        return carry

    lax.fori_loop(0, nc // 2, first_half, 0)
    lax.fori_loop(nc // 2, nc, second_half, 0)

    for bb, d in chains:
        cn_ref[bb, d] = _compact_state(c_s[bb, d])
        nn_ref[bb, d] = n_s[bb, d]
        mn_ref[bb, d] = m_s[bb, d]


SCAN_SEQS = 2


def _mlstm(mq, mg, c0, n0, m0, bsz, n):
    nc = n // BLOCK
    nb = SCAN_SEQS
    assert nc % 2 == 0 and bsz % nb == 0
    spec = lambda shape: pl.BlockSpec((nb,) + shape, lambda b: (b,) + (0,) * len(shape))
    seq_in = (lambda shape: pl.BlockSpec((nb,) + shape, lambda b: (b,) + (0,) * len(shape),
                                         pipeline_mode=pl.Buffered(1))) if bsz == nb else spec
    return pl.pallas_call(
        functools.partial(_mlstm_kernel, nc=nc, nb=nb),
        grid=(bsz // nb,),
        in_specs=[
            seq_in((n, 4 * HW)),
            seq_in((n, 2 * LANES)),
            spec((2, HW, LANES)),
            spec((2, 1, HW)),
            spec((2, 1, LANES)),
        ],
        out_specs=[
            spec((n, HW)),
            spec((2, HW, DH)),
            spec((2, 1, HW)),
            spec((2, 1, LANES)),
        ],
        out_shape=[
            jax.ShapeDtypeStruct((bsz, n, HW), bf16),
            jax.ShapeDtypeStruct((bsz, 2, HW, DH), f32),
            jax.ShapeDtypeStruct((bsz, 2, 1, HW), f32),
            jax.ShapeDtypeStruct((bsz, 2, 1, LANES), f32),
        ],
        scratch_shapes=[
            pltpu.VMEM((nb, n, HW), f32),
            pltpu.VMEM((nb, 2, HW, HW), f32),
            pltpu.VMEM((nb, 2, 1, HW), f32),
            pltpu.VMEM((nb, 2, 1, LANES), f32),
        ],
        compiler_params=_params(1),
        name="mlstm",
    )(mq.reshape(bsz, n, 4 * HW), mg.reshape(bsz, n, 2 * LANES), c0, n0, m0)


def _ret_kernel(x_ref, s0_ref, dl_ref, dr_ref, y_ref, sn_ref, o_s, s_s, qd_s, kd_s, cd_s, dm_s, *, nc, nb):
    masks = _head_lane_masks()
    bdm = _block_diag_mask()
    rowf = lax.broadcasted_iota(jnp.int32, (BLOCK, HW), 0).astype(f32)
    rel = (lax.broadcasted_iota(jnp.int32, (BLOCK, BLOCK), 0) -
           lax.broadcasted_iota(jnp.int32, (BLOCK, BLOCK), 1)).astype(f32)
    avg = jnp.where(bdm, 1.0 / DH, 0.0).astype(bf16)

    for d in range(2):
        lg = _log_sigmoid(dl_ref[d])
        if d == 0:
            qd_s[d] = jnp.exp((rowf + 1.0) * lg)
            kd_s[d] = jnp.exp((BLOCK - 1.0 - rowf) * lg)
            reld = rel
        else:
            qd_s[d] = jnp.exp((BLOCK - rowf) * lg)
            kd_s[d] = jnp.exp(rowf * lg)
            reld = -rel
        cdh = jnp.exp(float(BLOCK) * _log_sigmoid(dr_ref[d]))
        cd_s[d] = jnp.concatenate([cdh, cdh], axis=1)
        for h in range(N_HEADS):
            lgh = lg[:, DH * h:DH * h + 1]
            dm_s[d, :, BLOCK * h:BLOCK * (h + 1)] = jnp.where(reld >= 0.0, jnp.exp(jnp.maximum(reld, 0.0) * lgh), 0.0)

    chains = [(bb, d) for bb in range(nb) for d in range(2)]
    for bb, d in chains:
        s_s[bb, d] = _expand_state(s0_ref[bb, d])

    def compute(d, blk, sm):
        q = blk[:, 0:HW]
        k = blk[:, HW:2 * HW] * (DH ** -0.5)
        v = blk[:, 2 * HW:3 * HW]
        kf = k.astype(f32)
        kt = kf.T.astype(bf16)
        p = _dot(q, _head_block_kt(kt)) * dm_s[d]
        o = _dot(p.astype(bf16), jnp.concatenate(_head_block_v(v, masks), axis=0)) + qd_s[d] * _dot(q, sm.astype(bf16))
        kdt = (kf * kd_s[d]).T.astype(bf16)
        return o, cd_s[d] * sm + jnp.where(bdm, _dot(kdt, v), 0.0)

    def step(i, second):
        r0s = [pl.multiple_of((i if d == 0 else nc - 1 - i) * BLOCK, BLOCK) for bb, d in chains]
        loaded = [(x_ref[bb, pl.ds(r0, BLOCK), 0:3 * HW], s_s[bb, d]) for (bb, d), r0 in zip(chains, r0s)]
        if second:
            other = [o_s[bb, pl.ds(r0, BLOCK), :] for (bb, d), r0 in zip(chains, r0s)]
            gates = [x_ref[bb, pl.ds(r0, BLOCK), 3 * HW:4 * HW] for (bb, d), r0 in zip(chains, r0s)]
        outs = [compute(d, *l) for (bb, d), l in zip(chains, loaded)]
        for idx, ((bb, d), r0) in enumerate(zip(chains, r0s)):
            o, snew = outs[idx]
            if second:
                o = o + other[idx]
                gate = gates[idx].astype(f32)
                dev = o - _dot2_l(o, avg)
                var = _dot((dev * dev).astype(bf16), avg)
                y = gate * _sigmoid(gate) * (dev * lax.rsqrt(var + EPS))
                y_ref[bb, pl.ds(r0, BLOCK), :] = y.astype(y_ref.dtype)
            else:
                o_s[bb, pl.ds(r0, BLOCK), :] = o
            s_s[bb, d] = snew

    def first_half(i, carry):
        step(i, False)
        return carry

    def second_half(i, carry):
        step(i, True)
        return carry

    lax.fori_loop(0, nc // 2, first_half, 0)
    lax.fori_loop(nc // 2, nc, second_half, 0)

    for bb, d in chains:
        sn_ref[bb, d] = _compact_state(s_s[bb, d])


def _retention(rt, s0, dl, dr, bsz, n):
    nc = n // BLOCK
    nb = SCAN_SEQS
    assert nc % 2 == 0 and bsz % nb == 0
    spec = lambda shape: pl.BlockSpec((nb,) + shape, lambda b: (b,) + (0,) * len(shape))
    seq_in = (lambda shape: pl.BlockSpec((nb,) + shape, lambda b: (b,) + (0,) * len(shape),
                                         pipeline_mode=pl.Buffered(1))) if bsz == nb else spec
    return pl.pallas_call(
        functools.partial(_ret_kernel, nc=nc, nb=nb),
        grid=(bsz // nb,),
        in_specs=[
            seq_in((n, 4 * HW)),
            spec((2, HW, LANES)),
            _const_spec((2, 1, HW)),
            _const_spec((2, HW, LANES)),
        ],
        out_specs=[
            spec((n, HW)),
            spec((2, HW, DH)),
        ],
        out_shape=[
            jax.ShapeDtypeStruct((bsz, n, HW), bf16),
            jax.ShapeDtypeStruct((bsz, 2, HW, DH), f32),
        ],
        scratch_shapes=[
            pltpu.VMEM((nb, n, HW), f32),
            pltpu.VMEM((nb, 2, HW, HW), f32),
            pltpu.VMEM((2, BLOCK, HW), f32),
            pltpu.VMEM((2, BLOCK, HW), f32),
            pltpu.VMEM((2, HW, HW), f32),
            pltpu.VMEM((2, BLOCK, N_HEADS * BLOCK), f32),
        ],
        compiler_params=_params(1),
        name="retention",
    )(rt.reshape(bsz, n, 4 * HW), s0, dl, dr)


def _attn_kernel(q_ref, k_ref, v_ref, ck_ref, cv_ref, cos_ref, sin_ref, sk_ref, o_ref, kt_s, vt_s,
                 *, n_ctx, n, band, qb):
    i = pl.program_id(1)
    lane = lax.broadcasted_iota(jnp.int32, (1, LANES), 1)
    lo = lane < DH
    first = (lane % (DH // 2)) < (DH // 4)
    gmasks = _head_lane_masks()
    n_cblk = n_ctx // BLOCK
    wide = A_GROUP * qb

    def tile4(x):
        xr = pltpu.roll(x, DH, 1)
        a2 = jnp.where(lo, x, xr).astype(bf16)
        b2 = jnp.where(lo, xr, x).astype(bf16)
        return jnp.concatenate([a2, a2], axis=1), jnp.concatenate([b2, b2], axis=1)

    def rope(x, cos, sin):
        sw = jnp.where(first, pltpu.roll(x, LANES - DH // 4, 1), pltpu.roll(x, DH // 4, 1))
        return x * cos + sw * sin

    def put_block(blk, r0, kk, vv):
        k0, k1 = tile4(kk)
        kt_s[0, pl.ds(r0, BLOCK), :] = k0
        kt_s[1, pl.ds(r0, BLOCK), :] = k1
        vt = vv.T
        for j in range(A_KV_HEADS):
            vj = vt[DH * j:DH * (j + 1)]
            vt_s[j, blk] = jnp.concatenate([vj, vj], axis=0).astype(bf16)

    @pl.when(i == 0)
    def _prepare_keys():
        for c in range(n_cblk):
            put_block(c, c * BLOCK, ck_ref[c * BLOCK:(c + 1) * BLOCK, :], cv_ref[c * BLOCK:(c + 1) * BLOCK, :])
        if band:
            nb = n // BLOCK
            for blk in (n_cblk, n_cblk + 1 + nb):
                for j in range(A_KV_HEADS):
                    kt_s[j, blk * BLOCK:(blk + 1) * BLOCK, :] = jnp.zeros((BLOCK, HW), bf16)
                    vt_s[j, blk] = jnp.zeros((BLOCK, BLOCK), bf16)

            def body(c, carry):
                r0 = pl.multiple_of(c * BLOCK, BLOCK)
                kk = rope(k_ref[pl.ds(r0, BLOCK), :], cos_ref[pl.ds(r0, BLOCK), :], sin_ref[pl.ds(r0, BLOCK), :])
                put_block(n_cblk + 1 + c, pl.multiple_of(n_ctx + BLOCK + r0, BLOCK), kk, v_ref[pl.ds(r0, BLOCK), :])
                return carry

            lax.fori_loop(0, nb, body, 0)

    q = q_ref[...]
    if band:
        q0 = pl.multiple_of(i * BLOCK, BLOCK)
        cos = cos_ref[pl.ds(q0, BLOCK), :]
        sin = sin_ref[pl.ds(q0, BLOCK), :]
        q = jnp.concatenate([rope(q[:, LANES * t:LANES * (t + 1)], cos, sin) for t in range(A_W // LANES)], axis=1)
        kj = lax.broadcasted_iota(jnp.int32, (3 * BLOCK, BLOCK), 0)
        qi = lax.broadcasted_iota(jnp.int32, (3 * BLOCK, BLOCK), 1)
        tok = kj + (i - 1) * BLOCK
        ok = (kj >= qi) & (kj <= qi + 2 * WINDOW) & (tok >= 0) & (tok < n)
        bias = jnp.where(ok, 0.0, NEG)
        bias4 = jnp.concatenate([bias] * A_GROUP, axis=1)
        b0 = pl.multiple_of(n_ctx + i * BLOCK, BLOCK)
        vband = [vt_s[j, pl.ds(n_cblk + i, 3)] for j in range(A_KV_HEADS)]
    q = q * (DH ** -0.5)
    row_lo = lax.broadcasted_iota(jnp.int32, (2 * DH, qb), 0) < DH

    outs = []
    for j in range(A_KV_HEADS):
        q256 = q[:, HW * j:HW * (j + 1)]
        qs = jnp.concatenate([jnp.where(gmasks[g], q256, 0.0) for g in range(A_GROUP)], axis=0).astype(bf16)
        sink = jnp.concatenate(
            [jnp.broadcast_to(sk_ref[A_GROUP * j + g:A_GROUP * j + g + 1, 0:1], (1, qb)) for g in range(A_GROUP)],
            axis=1)
        s1 = _dot_nt(kt_s[j, 0:n_ctx, :], qs)
        m = jnp.maximum(jnp.max(s1, axis=0, keepdims=True), sink)
        if band:
            s2 = _dot_nt(kt_s[j, pl.ds(b0, 3 * BLOCK), :], qs) + bias4
            m = jnp.maximum(m, jnp.max(s2, axis=0, keepdims=True))
        p1 = jnp.exp(s1 - m)
        den = jnp.sum(p1, axis=0, keepdims=True) + jnp.exp(sink - m)
        vctx = jnp.concatenate([vt_s[j, c] for c in range(n_cblk)], axis=1)
        acc = _dot(vctx, p1.astype(bf16))
        if band:
            p2 = jnp.exp(s2 - m)
            den = den + jnp.sum(p2, axis=0, keepdims=True)
            acc = acc + _dot(jnp.concatenate([vband[j][t] for t in range(3)], axis=1), p2.astype(bf16))
        acc = acc * (1.0 / den)
        for g in range(0, A_GROUP, 2):
            pair = jnp.where(row_lo, acc[:, qb * g:qb * (g + 1)], acc[:, qb * (g + 1):qb * (g + 2)])
            outs.append(pair.T)
    o_ref[...] = jnp.concatenate(outs, axis=1).astype(o_ref.dtype)


def _attention(at, ck, cv, ck_spec, cv_spec, cos, sin, sk, bsz, n, n_ctx, band):
    qb = BLOCK if band else min(n, 2 * BLOCK)
    nq = n // qb
    s_tot = n_ctx + (n + 2 * BLOCK if band else 0)
    at3 = at.reshape(bsz, n, A_W + 2 * A_KV_W)
    kcol = A_W // A_KV_W
    return pl.pallas_call(
        functools.partial(_attn_kernel, n_ctx=n_ctx, n=n, band=band, qb=qb),
        grid=(bsz, nq),
        in_specs=[
            pl.BlockSpec((None, qb, A_W), lambda b, i: (b, i, 0)),
            pl.BlockSpec((None, n, A_KV_W), lambda b, i: (b, 0, kcol)),
            pl.BlockSpec((None, n, A_KV_W), lambda b, i: (b, 0, kcol + 1)),
            ck_spec,
            cv_spec,
            _const_spec(cos.shape),
            _const_spec(sin.shape),
            _const_spec((A_HEADS, LANES)),
        ],
        out_specs=pl.BlockSpec((None, qb, A_W), lambda b, i: (b, i, 0)),
        out_shape=jax.ShapeDtypeStruct((bsz, n, A_W), bf16),
        scratch_shapes=[
            pltpu.VMEM((A_KV_HEADS, s_tot, HW), bf16),
            pltpu.VMEM((A_KV_HEADS, s_tot // BLOCK, BLOCK, BLOCK), bf16),
        ],
        compiler_params=_params(2),
        name="attention",
    )(at3, at3, at3, ck, cv, cos, sin, sk)


def _rope_tables(n):
    tok = np.arange(n)
    pos = np.stack([tok // GRID_W, tok % GRID_W], axis=1).astype(np.float32)
    quarter = DH // 4
    freqs = np.power(np.float32(ROPE_BASE), -np.arange(quarter, dtype=np.float32) / np.float32(quarter))
    lane = np.arange(LANES) % DH
    axis = lane // (DH // 2)
    fidx = lane % quarter
    sign = np.where((lane % (DH // 2)) < quarter, -1.0, 1.0)
    ang = (pos[:, axis] * freqs.astype(np.float32)[fidx][None, :]).astype(np.float32).astype(np.float64)
    return jnp.asarray(np.cos(ang), f32), jnp.asarray(np.sin(ang) * sign[None, :], f32)


CONV_PAD = 16


def _conv_kernel(u_ref, w_ref, b_ref, lg_ref, lb_ref, y_ref, z_s, *, n):
    nc = n // BLOCK
    z_s[0:CONV_PAD, :] = jnp.zeros((CONV_PAD, HW), f32)
    z_s[CONV_PAD + n:2 * CONV_PAD + n, :] = jnp.zeros((CONV_PAD, HW), f32)

    def glu(c, carry):
        r0 = pl.multiple_of(c * BLOCK, BLOCK)
        blk = u_ref[pl.ds(r0, BLOCK), :].astype(f32)
        z_s[pl.ds(pl.multiple_of(r0 + CONV_PAD, SUBLANES), BLOCK), :] = blk[:, 0:HW] * _sigmoid(blk[:, HW:2 * HW])
        return carry

    lax.fori_loop(0, nc, glu, 0)

    first = CONV_PAD - CONV_K // 2
    span = BLOCK + 2 * CONV_PAD

    def tile(c, carry):
        r0 = pl.multiple_of(c * BLOCK, BLOCK)
        win = z_s[pl.ds(r0, span), :]
        acc = jnp.zeros((BLOCK, HW), f32)
        for r in range(SUBLANES):
            taps = [kk for kk in range(CONV_K) if (first + kk) % SUBLANES == r]
            if not taps:
                continue
            shifted = win if r == 0 else pltpu.roll(win, span - r, 0)
            for kk in taps:
                a = (first + kk) // SUBLANES * SUBLANES
                acc = acc + shifted[a:a + BLOCK, :] * w_ref[kk:kk + 1, :]
        zc = acc + b_ref[...]
        mu = jnp.mean(zc, axis=-1, keepdims=True)
        dev = zc - mu
        var = jnp.mean(dev * dev, axis=-1, keepdims=True)
        t = dev * lax.rsqrt(var + EPS) * lg_ref[...] + lb_ref[...]
        y_ref[pl.ds(r0, BLOCK), :] = (t * _sigmoid(t)).astype(y_ref.dtype)
        return carry

    lax.fori_loop(0, nc, tile, 0)


def _conv(cu, w, b, lg, lb, bsz, n):
    return pl.pallas_call(
        functools.partial(_conv_kernel, n=n),
        grid=(bsz,),
        in_specs=[
            pl.BlockSpec((None, n, 2 * HW), lambda bb: (bb, 0, 0)),
            _const_spec((CONV_K, HW)),
            _const_spec((1, HW)),
            _const_spec((1, HW)),
            _const_spec((1, HW)),
        ],
        out_specs=pl.BlockSpec((None, n, HW), lambda bb: (bb, 0, 0)),
        out_shape=jax.ShapeDtypeStruct((bsz, n, HW), bf16),
        scratch_shapes=[pltpu.VMEM((n + 2 * CONV_PAD, HW), f32)],
        compiler_params=_params(1),
        name="conv",
    )(cu.reshape(bsz, n, 2 * HW), w, b, lg, lb)


def _post_kernel(x_ref, ym_ref, ya_ref, yc_ref, yr_ref, gt_ref, mod_ref, ng_ref, wbr_ref, wout_ref, w1_ref, w2_ref,
                 o_ref):
    def rms(a):
        return a * lax.rsqrt(jnp.mean(a * a, axis=-1, keepdims=True) + EPS)

    acc = None
    for i, y_ref in enumerate((ym_ref, ya_ref, yc_ref, yr_ref)):
        br = _dot(y_ref[...], wbr_ref[BR_OFFS[i]:BR_OFFS[i + 1], :])
        gate = _sigmoid(gt_ref[:, D_MODEL * i:D_MODEL * (i + 1)].astype(f32))
        acc = gate * br if acc is None else acc + gate * br
    mix = _dot(acc.astype(bf16), wout_ref[...])
    x = x_ref[...] + mod_ref[2:3, :] * (rms(mix) * ng_ref[1:2, :])
    h = (rms(x) * ng_ref[2:3, :] * (1.0 + mod_ref[4:5, :]) + mod_ref[3:4, :]).astype(bf16)
    chunk = 512
    f = jnp.zeros(x.shape, f32)
    for c in range(0, D_FF, chunk):
        a = jnp.maximum(_dot(h, w1_ref[:, c:c + chunk]), 0.0)
        f = f + _dot((a * a).astype(bf16), w2_ref[c:c + chunk, :])
    o_ref[...] = x + mod_ref[5:6, :] * (rms(f) * ng_ref[3:4, :])


def _post(x2d, ym, ya, yc, yr, gt, mod4, lw, layer, row_of_tile):
    m = x2d.shape[0]
    tm = ROW_TILE
    rows = lambda w: pl.BlockSpec((tm, w), lambda i: (i, 0))
    return pl.pallas_call(
        _post_kernel,
        grid=(m // tm,),
        in_specs=[
            rows(D_MODEL), rows(HW), rows(A_W), rows(HW), rows(HW), rows(N_BRANCH * D_MODEL),
            pl.BlockSpec((None, None, 6, D_MODEL), lambda i: (layer, row_of_tile(i), 0, 0)),
            _const_spec((4, D_MODEL)),
            _const_spec((BR_OFFS[-1], D_MODEL)),
            _const_spec((D_MODEL, D_MODEL)),
            _const_spec((D_MODEL, D_FF)),
            _const_spec((D_FF, D_MODEL)),
        ],
        out_specs=rows(D_MODEL),
        out_shape=jax.ShapeDtypeStruct((m, D_MODEL), f32),
        compiler_params=_params(1),
        name="post",
    )(x2d, ym.reshape(m, HW), ya.reshape(m, A_W), yc.reshape(m, HW), yr.reshape(m, HW), gt, mod4, lw['ng'],
      lw['w_branch'], lw['w_out'], lw['w_mlp1'], lw['w_mlp2'])


PACK_COLS = 256


def _pack_kernel(wt_ref, o_ref):
    s = pl.program_id(1)
    xt = wt_ref[0].T
    gate_step = PK_MG // PACK_COLS

    @pl.when(s != gate_step)
    def _plain():
        o_ref[...] = xt.astype(bf16)

    @pl.when(s == gate_step)
    def _gates():
        lane = lax.broadcasted_iota(jnp.int32, (1, LANES), 1)
        gates = xt[:, 0:LANES]
        o_ref[:, 0:LANES] = jnp.where(lane < N_GATES, gates, 0.0).astype(bf16)
        shifted = pltpu.roll(gates, LANES - N_HEADS, 1)
        o_ref[:, LANES:2 * LANES] = jnp.where(lane < N_GATES - N_HEADS, shifted, 0.0).astype(bf16)


def _pack_w_in(w_in, b_in):
    gate_step = PK_MG // PACK_COLS
    assert PK_MG % PACK_COLS == 0 and 2 * LANES == PACK_COLS and PK_W % PACK_COLS == 0

    def src_col(s):
        col = jnp.where(s <= gate_step, PACK_COLS * s, PK_MG + N_GATES + PACK_COLS * (s - gate_step - 1))
        return pl.multiple_of(col, N_GATES)

    w_pk = pl.pallas_call(
        _pack_kernel,
        grid=(DEPTH, PK_W // PACK_COLS),
        in_specs=[pl.BlockSpec((pl.Element(1), pl.Element(PACK_COLS), pl.Element(D_MODEL)),
                               lambda l, s: (l, src_col(s), 0))],
        out_specs=pl.BlockSpec((None, D_MODEL, PACK_COLS), lambda l, s: (l, 0, s)),
        out_shape=jax.ShapeDtypeStruct((DEPTH, D_MODEL, PK_W), bf16),
        compiler_params=_params(2),
        name="pack_w_in",
    )(jnp.swapaxes(w_in, 1, 2))
    g0 = PK_MG
    pad = lambda cols: jnp.zeros((DEPTH, cols), f32)
    b_pk = jnp.concatenate(
        [b_in[:, :g0 + N_GATES], pad(LANES - N_GATES), b_in[:, g0 + N_HEADS:g0 + N_GATES],
         pad(LANES - N_GATES + N_HEADS), b_in[:, g0 + N_GATES:]], axis=-1)
    return w_pk, b_pk[:, None, :]


def _stack_state(s):
    bsz = s.shape[0]
    s = s.reshape(bsz, 2, HW, DH)
    return jnp.pad(s, ((0, 0), (0, 0), (0, 0), (0, LANES - DH)))


def _layer(x2d, bsz, n, layer, mod4, lw, cache, row_of_tile):
    is_ctx = cache is None
    mq, mg, at, cv, rt, gt = _inproj(x2d, mod4, lw['ng'], lw['w_in'], lw['b_in'], layer, row_of_tile)

    if is_ctx:
        c0 = jnp.zeros((bsz, 2, HW, LANES), f32)
        n0 = jnp.zeros((bsz, 2, 1, HW), f32)
        m0 = jnp.zeros((bsz, 2, 1, LANES), f32)
        s0 = jnp.zeros((bsz, 2, HW, LANES), f32)
    else:
        c0 = _stack_state(cache['C'])
        n0 = cache['n'].reshape(bsz, 2, 1, HW)
        m0 = jnp.stack([jnp.pad(cache['m'][:, d], ((0, 0), (2 * N_HEADS * d, LANES - N_HEADS - 2 * N_HEADS * d)))
                        for d in range(2)], axis=1)[:, :, None, :]
        s0 = _stack_state(cache['S'])
    ym, c_fin, n_fin, m_fin = _mlstm(mq, mg, c0, n0, m0, bsz, n)
    yr, s_fin = _retention(rt, s0, lw['ret_dl'], lw['ret_dr'], bsz, n)

    at3 = at.reshape(bsz, n, A_W + 2 * A_KV_W)
    kcol = A_W // A_KV_W
    if is_ctx:
        dummy = jnp.zeros((SUBLANES, LANES), f32)
        ck_spec = pl.BlockSpec((None, n, A_KV_W), lambda b, i: (b, 0, kcol))
        cv_spec = pl.BlockSpec((None, n, A_KV_W), lambda b, i: (b, 0, kcol + 1))
        ya = _attention(at, at3, at3, ck_spec, cv_spec, dummy, dummy, lw['sink'], bsz, n, n, False)
    else:
        n_ctx = cache['k'].shape[2]
        ck_spec = pl.BlockSpec((None, None, n_ctx, A_KV_W), lambda b, i: (b, layer, 0, 0))
        cos, sin = _rope_tables(n)
        ya = _attention(at, cache['k'], cache['v'], ck_spec, ck_spec, cos, sin, lw['sink'], bsz, n, n_ctx, True)

    yc = _conv(cv, lw['conv_w'], lw['conv_b'], lw['conv_ln_g'], lw['conv_ln_b'], bsz, n)

    x2d = _post(x2d, ym, ya, yc, yr, gt, mod4, lw, layer, row_of_tile)

    ctx = None
    if is_ctx:
        ctx = (at3[:, :, A_W:A_W + A_KV_W].reshape(bsz, n, A_KV_HEADS, DH),
               at3[:, :, A_W + A_KV_W:].reshape(bsz, n, A_KV_HEADS, DH),
               c_fin.reshape(bsz, 2, N_HEADS, DH, DH),
               n_fin.reshape(bsz, 2, N_HEADS, DH),
               jnp.stack([m_fin[:, d, 0, 2 * N_HEADS * d:2 * N_HEADS * d + N_HEADS] for d in range(2)], axis=1),
               s_fin.reshape(bsz, 2, N_HEADS, DH, DH))
    return x2d, ctx


def kernel(x_prompt, x_sample, c, cache_k, cache_v, state_mlstm_C, state_mlstm_n, state_mlstm_m, state_ret,
           c_ctx, w_ada, b_ada, norm_g, w_in, b_in, w_branch, w_out, attn_sink, ret_decay_logit,
           conv_w, conv_b, conv_ln_g, conv_ln_b, w_mlp1, w_mlp2):
    bsz, seq, _ = x_prompt.shape
    dbsz, dseq, _ = x_sample.shape
    past = cache_k.shape[2]

    c_rows = jnp.zeros((SUBLANES, D_MODEL), f32).at[0].set(c_ctx).at[1:1 + dbsz].set(c)
    mod4 = _ada(c_rows, w_ada, b_ada).reshape(DEPTH, SUBLANES, 6, D_MODEL)

    w_pk, b_pk = _pack_w_in(w_in, b_in)
    layers = []
    for l in range(DEPTH):
        layers.append({
            'ng': norm_g[l], 'w_in': w_pk, 'b_in': b_pk,
            'w_branch': w_branch[l].astype(bf16), 'w_out': w_out[l].astype(bf16),
            'w_mlp1': w_mlp1[l].astype(bf16), 'w_mlp2': w_mlp2[l].astype(bf16),
            'sink': jnp.broadcast_to(attn_sink[l][:, None], (A_HEADS, LANES)),
            'ret_dl': jnp.repeat(ret_decay_logit[l], DH, axis=-1).reshape(2, 1, HW),
            'ret_dr': jnp.broadcast_to(jnp.repeat(ret_decay_logit[l], DH, axis=-1)[:, :, None], (2, HW, LANES)),
            'conv_w': conv_w[l], 'conv_b': conv_b[l][None, :],
            'conv_ln_g': conv_ln_g[l][None, :], 'conv_ln_b': conv_ln_b[l][None, :],
        })

    xp = x_prompt.reshape(bsz * seq, D_MODEL)
    ctxs = []
    for l in range(DEPTH):
        xp, ctx = _layer(xp, bsz, seq, l, mod4, layers[l], None, lambda i: 0)
        ctxs.append(ctx)

    xs = x_sample.reshape(dbsz * dseq, D_MODEL)
    tiles_per_seq = dseq // ROW_TILE
    ck = cache_k.reshape(dbsz, DEPTH, past, A_KV_W)
    cv = cache_v.reshape(dbsz, DEPTH, past, A_KV_W)
    for l in range(DEPTH):
        cache = {'k': ck, 'v': cv, 'C': state_mlstm_C[:, l], 'n': state_mlstm_n[:, l],
                 'm': state_mlstm_m[:, l], 'S': state_ret[:, l]}
        xs, _ = _layer(xs, dbsz, dseq, l, mod4, layers[l], cache, lambda i: 1 + i // tiles_per_seq)

    stack = lambda j: jnp.stack([ctxs[l][j] for l in range(DEPTH)], axis=1)
    return (xp.reshape(bsz, seq, D_MODEL), xs.reshape(dbsz, dseq, D_MODEL),
            stack(0), stack(1), stack(2), stack(3), stack(4), stack(5))
```

```python
import functools

import numpy as np
import jax
import jax.numpy as jnp
from jax import lax
from jax.experimental import pallas as pl
from jax.experimental.pallas import tpu as pltpu

f32 = jnp.float32
bf16 = jnp.bfloat16

D_MODEL = 1024
DEPTH = 2
GRID_W = 64
BLOCK = 128
N_HEADS = 4
DH = 64
HW = N_HEADS * DH
A_HEADS = 8
A_KV_HEADS = 2
A_GROUP = A_HEADS // A_KV_HEADS
A_W = A_HEADS * DH
A_KV_W = A_KV_HEADS * DH
WINDOW = 128
ROPE_BASE = 10000.0
CONV_K = 31
D_FF = 4 * D_MODEL
N_BRANCH = 4
EPS = 1e-6
NEG = -1e30
LANES = 128
SUBLANES = 8

N_GATES = 4 * N_HEADS
PK_MQ = 0
PK_MG = PK_MQ + 4 * HW
PK_AT = PK_MG + 2 * LANES
PK_CV = PK_AT + A_W + 2 * A_KV_W
PK_RT = PK_CV + 2 * HW
PK_GT = PK_RT + 4 * HW
PK_W = PK_GT + N_BRANCH * D_MODEL
BR_OFFS = (0, HW, HW + A_W, 2 * HW + A_W, 3 * HW + A_W)

VMEM_LIMIT = 56 * 1024 * 1024
ROW_TILE = 512


def _sigmoid(x):
    return 0.5 * (jnp.tanh(0.5 * x) + 1.0)


def _log_sigmoid(x):
    return jnp.minimum(x, 0.0) - jnp.log1p(jnp.exp(-jnp.abs(x)))


def _dot(a, b):
    return jnp.dot(a, b, preferred_element_type=f32)


def _dot_nt(a, b):
    return lax.dot_general(a, b, (((1,), (1,)), ((), ())), preferred_element_type=f32)


def _dot2_r(t, x):
    x1 = x.astype(bf16)
    x2 = (x - x1.astype(f32)).astype(bf16)
    r = _dot(t, jnp.concatenate([x1, x2], axis=1))
    return r[:, 0:x.shape[1]] + r[:, x.shape[1]:]


def _dot2_l(x, t):
    x1 = x.astype(bf16)
    x2 = (x - x1.astype(f32)).astype(bf16)
    r = _dot(jnp.concatenate([x1, x2], axis=0), t)
    return r[0:x.shape[0]] + r[x.shape[0]:]


def _const_spec(shape):
    nd = len(shape)
    return pl.BlockSpec(shape, lambda *_: (0,) * nd, pipeline_mode=pl.Buffered(1))


def _params(n_axes):
    return pltpu.CompilerParams(dimension_semantics=("arbitrary",) * n_axes, vmem_limit_bytes=VMEM_LIMIT)


def _ada_kernel(c_ref, w_ref, b_ref, o_ref):
    c = c_ref[...]
    s = (c * _sigmoid(c)).astype(bf16)
    o_ref[...] = _dot(s, w_ref[...].astype(bf16)) + b_ref[...]


def _ada(c_rows, w_ada, b_ada):
    tn = 1536
    n_out = 6 * D_MODEL
    return pl.pallas_call(
        _ada_kernel,
        grid=(DEPTH, n_out // tn),
        in_specs=[
            pl.BlockSpec((SUBLANES, D_MODEL), lambda l, j: (0, 0)),
            pl.BlockSpec((None, D_MODEL, tn), lambda l, j: (l, 0, j)),
            pl.BlockSpec((None, 1, tn), lambda l, j: (l, 0, j)),
        ],
        out_specs=pl.BlockSpec((None, SUBLANES, tn), lambda l, j: (l, 0, j)),
        out_shape=jax.ShapeDtypeStruct((DEPTH, SUBLANES, n_out), f32),
        compiler_params=_params(2),
        name="ada",
    )(c_rows, w_ada, b_ada.reshape(DEPTH, 1, n_out))


def _inproj_kernel(x_ref, mod_ref, ng_ref, w_ref, b_ref, mq_ref, mg_ref, at_ref, cv_ref, rt_ref, gt_ref):
    x = x_ref[...]
    y = x * lax.rsqrt(jnp.mean(x * x, axis=-1, keepdims=True) + EPS) * ng_ref[0:1, :]
    h = (y * (1.0 + mod_ref[1:2, :]) + mod_ref[0:1, :]).astype(bf16)
    chunk = 512
    for ref, base in ((mq_ref, PK_MQ), (mg_ref, PK_MG), (at_ref, PK_AT), (cv_ref, PK_CV), (rt_ref, PK_RT),
                      (gt_ref, PK_GT)):
        total = ref.shape[-1]
        for c in range(0, total, chunk):
            w = min(chunk, total - c)
            r = _dot_nt(h, w_ref[base + c:base + c + w, :]) + b_ref[:, base + c:base + c + w]
            ref[:, c:c + w] = r.astype(ref.dtype)


def _inproj(x2d, mod4, ng, w_pk, b_pk, layer, row_of_tile):
    m = x2d.shape[0]
    tm = ROW_TILE
    widths = (4 * HW, 2 * LANES, A_W + 2 * A_KV_W, 2 * HW, 4 * HW, N_BRANCH * D_MODEL)
    dtypes = (bf16, f32, f32, bf16, bf16, bf16)
    return pl.pallas_call(
        _inproj_kernel,
        grid=(m // tm,),
        in_specs=[
            pl.BlockSpec((tm, D_MODEL), lambda i: (i, 0)),
            pl.BlockSpec((None, None, 6, D_MODEL), lambda i: (layer, row_of_tile(i), 0, 0)),
            _const_spec((4, D_MODEL)),
            pl.BlockSpec((None, PK_W, D_MODEL), lambda i: (layer, 0, 0), pipeline_mode=pl.Buffered(1)),
            pl.BlockSpec((None, 1, PK_W), lambda i: (layer, 0, 0), pipeline_mode=pl.Buffered(1)),
        ],
        out_specs=[pl.BlockSpec((tm, w), lambda i: (i, 0)) for w in widths],
        out_shape=[jax.ShapeDtypeStruct((m, w), dt) for w, dt in zip(widths, dtypes)],
        compiler_params=_params(1),
        name="inproj",
    )(x2d, mod4, ng, w_pk, b_pk)


def _head_lane_masks():
    lane_head = lax.broadcasted_iota(jnp.int32, (1, HW), 1) // DH
    return [lane_head == h for h in range(N_HEADS)]


PAIRS = N_HEADS // 2


def _pair_diag_mask():
    r = (lax.broadcasted_iota(jnp.int32, (HW, LANES), 0) // DH) % 2
    c = lax.broadcasted_iota(jnp.int32, (HW, LANES), 1) // DH
    return r == c


def _expand_state(x):
    return jnp.where(_pair_diag_mask(), x + pltpu.roll(x, DH, 1), 0.0)


def _compact_state(c):
    return (c + pltpu.roll(c, DH, 1))[:, 0:DH]


def _pair_qk(q, kt):
    zeros = jnp.zeros((DH, BLOCK), kt.dtype)
    outs = []
    for j in range(PAIRS):
        rows = kt[LANES * j:LANES * (j + 1)]
        blk = jnp.concatenate([jnp.concatenate([rows[0:DH], zeros], axis=0),
                               jnp.concatenate([zeros, rows[DH:2 * DH]], axis=0)], axis=1)
        outs.append(_dot(q[:, LANES * j:LANES * (j + 1)], blk))
    return jnp.concatenate(outs, axis=1)


def _pair_pv(p, v):
    low = lax.broadcasted_iota(jnp.int32, (1, LANES), 1) < DH
    outs = []
    for j in range(PAIRS):
        vp = v[:, LANES * j:LANES * (j + 1)]
        zero = jnp.zeros_like(vp)
        blk = jnp.concatenate([jnp.where(low, vp, zero), jnp.where(low, zero, vp)], axis=0)
        outs.append(_dot(p[:, 2 * BLOCK * j:2 * BLOCK * (j + 1)], blk))
    return jnp.concatenate(outs, axis=1)


def _pair_state_read(q, c):
    return jnp.concatenate([_dot(q[:, LANES * j:LANES * (j + 1)], c[LANES * j:LANES * (j + 1)])
                            for j in range(PAIRS)], axis=1)


def _pair_state_update(a_t, v):
    upd = jnp.concatenate([_dot(a_t[LANES * j:LANES * (j + 1)], v[:, LANES * j:LANES * (j + 1)])
                           for j in range(PAIRS)], axis=0)
    return jnp.where(_pair_diag_mask(), upd, 0.0)


def _mlstm_kernel(x_ref, g_ref, c0_ref, n0_ref, m0_ref, y_ref, cn_ref, nn_ref, mn_ref,
                  h_s, c_s, n_s, m_s, *, nc, nb):
    wide = N_HEADS * BLOCK
    row = lax.broadcasted_iota(jnp.int32, (BLOCK, BLOCK), 0)
    col = lax.broadcasted_iota(jnp.int32, (BLOCK, BLOCK), 1)
    tsum = (jnp.where(row >= col, 1.0, 0.0).astype(bf16), jnp.where(row <= col, 1.0, 0.0).astype(bf16))
    roww = lax.broadcasted_iota(jnp.int32, (BLOCK, wide), 0)
    colw = lax.broadcasted_iota(jnp.int32, (BLOCK, wide), 1) & (BLOCK - 1)
    tri4 = (roww >= colw, roww <= colw)
    masks = _head_lane_masks()
    lane = lax.broadcasted_iota(jnp.int32, (1, LANES), 1)
    row8 = lax.broadcasted_iota(jnp.int32, (SUBLANES, BLOCK), 0)

    def onehot(shape, row_expr, col_expr):
        r = lax.broadcasted_iota(jnp.int32, shape, 0)
        c = lax.broadcasted_iota(jnp.int32, shape, 1)
        return jnp.where(row_expr(r) == col_expr(c), 1.0, 0.0).astype(bf16)

    g0s = (0, 2 * N_HEADS)
    valid = tuple((lane >= g0) & (lane < g0 + N_HEADS) for g0 in g0s)
    hexp = tuple(onehot((LANES, HW), lambda r: r, lambda c, g0=g0: c // DH + g0) for g0 in g0s)
    hselg = tuple(onehot((HW, LANES), lambda r, g0=g0: r // DH + g0, lambda c: c) for g0 in g0s)

    chains = [(bb, d) for bb in range(nb) for d in range(2)]
    for bb, d in chains:
        c_s[bb, d] = _expand_state(c0_ref[bb, d])
        n_s[bb, d] = n0_ref[bb, d]
        m_s[bb, d] = m0_ref[bb, d]

    def load(bb, d, c):
        r0 = pl.multiple_of(c * BLOCK, BLOCK)
        return (r0, x_ref[bb, pl.ds(r0, BLOCK), 0:3 * HW], g_ref[bb, pl.ds(r0, BLOCK), :],
                c_s[bb, d], n_s[bb, d], m_s[bb, d])

    def compute(d, blk, g, cmat, nrow, m_row):
        g0 = g0s[d]
        q = blk[:, 0:HW]
        k = blk[:, HW:2 * HW] * (DH ** -0.5)
        v = blk[:, 2 * HW:3 * HW]
        ga = g[:, 0:LANES]
        gb = g[:, LANES:2 * LANES]
        bc = _dot2_r(tsum[d], _log_sigmoid(gb))
        beta = ga - bc
        beta_t = beta.T
        cmx = beta
        sh = 1
        while sh < BLOCK:
            if d == 0:
                cmx = jnp.maximum(cmx, jnp.where(row >= sh, pltpu.roll(cmx, sh, 0), NEG))
            else:
                cmx = jnp.maximum(cmx, jnp.where(row < BLOCK - sh, pltpu.roll(cmx, BLOCK - sh, 0), NEG))
            sh *= 2
        mx = jnp.maximum(m_row, cmx)
        alpha = jnp.where(valid[d], -mx, 0.0)
        w_int = jnp.where(valid[d], jnp.exp(m_row - mx), 0.0)
        e_nb = jnp.exp(alpha - bc)
        z = jnp.concatenate([jnp.broadcast_to(alpha[:, g0 + h:g0 + h + 1], (BLOCK, BLOCK)) for h in range(N_HEADS)],
                            axis=1)
        beta_w = jnp.concatenate([beta_t[g0 + h:g0 + h + 1, :] for h in range(N_HEADS)], axis=1)
        ktf = k.astype(f32).T
        s_wide = _pair_qk(q, ktf.astype(bf16))
        p = s_wide * jnp.exp(jnp.where(tri4[d], z + beta_w, NEG))
        num = _pair_pv(p.astype(bf16), v)
        den = jnp.zeros((BLOCK, LANES), f32)
        for h in range(N_HEADS):
            den = jnp.where(lane == g0 + h, jnp.sum(p[:, BLOCK * h:BLOCK * (h + 1)], axis=1, keepdims=True), den)
        qc = _pair_state_read(q, cmat.astype(bf16))
        qn = _dot((q.astype(f32) * nrow).astype(bf16), hselg[d])
        den = den + w_int * qn
        inv = jnp.where(valid[d], 1.0 / jnp.maximum(jnp.abs(den), e_nb), 0.0)
        fac = _dot(jnp.concatenate([w_int, inv], axis=0).astype(bf16), hexp[d])
        hout = (num + fac[0:BLOCK] * qc) * fac[BLOCK:2 * BLOCK]
        last = BLOCK - 1 if d == 0 else 0
        a_last = alpha[last:last + 1, :]
        m_new = jnp.where(valid[d], bc[last:last + 1, :] - a_last, 0.0)
        dec = jnp.exp(m_row + a_last)
        wr = jnp.zeros((SUBLANES, BLOCK), f32)
        kws = []
        for h in range(N_HEADS):
            w_row = jnp.exp(beta_t[g0 + h:g0 + h + 1, :] + a_last[:, g0 + h:g0 + h + 1])
            kws.append(ktf[DH * h:DH * (h + 1), :] * w_row)
            wr = jnp.where(row8 == h, w_row, wr)
        upd = _pair_state_update(jnp.concatenate(kws, axis=0).astype(bf16), v)
        wk = _dot(wr.astype(bf16), k)
        nnew = jnp.zeros((1, HW), f32)
        cnew = []
        for h in range(N_HEADS):
            rs = slice(DH * h, DH * (h + 1))
            dh = dec[:, g0 + h:g0 + h + 1]
            cnew.append(dh * cmat[rs, :] + upd[rs, :])
            nnew = jnp.where(masks[h], dh * nrow + wk[h:h + 1, :], nnew)
        return hout, jnp.concatenate(cnew, axis=0), nnew, m_new

    def step(i, second):
        loaded = [load(bb, d, i if d == 0 else nc - 1 - i) for bb, d in chains]
        if second:
            other = [h_s[bb, pl.ds(l[0], BLOCK), :] for (bb, d), l in zip(chains, loaded)]
            ogate = [x_ref[bb, pl.ds(l[0], BLOCK), 3 * HW:4 * HW] for (bb, d), l in zip(chains, loaded)]
        outs = [compute(d, *l[1:]) for (bb, d), l in zip(chains, loaded)]
        for idx, ((bb, d), l) in enumerate(zip(chains, loaded)):
            hout, cnew, nnew, m_new = outs[idx]
            if second:
                y = _sigmoid(ogate[idx].astype(f32)) * (other[idx] + hout)
                y_ref[bb, pl.ds(l[0], BLOCK), :] = y.astype(y_ref.dtype)
            else:
                h_s[bb, pl.ds(l[0], BLOCK), :] = hout
            c_s[bb, d] = cnew
            n_s[bb, d] = nnew
            m_s[bb, d] = m_new

    def first_half(i, carry):
        step(i, False)
        return carry

    def second_half(i, carry):
        step(i, True)
        return carry

    lax.fori_loop(0, nc // 2, first_half, 0)
    lax.fori_loop(nc // 2, nc, second_half, 0)

    for bb, d in chains:
        cn_ref[bb, d] = _compact_state(c_s[bb, d])
        nn_ref[bb, d] = n_s[bb, d]
        mn_ref[bb, d] = m_s[bb, d]


SCAN_SEQS = 2


def _mlstm(mq, mg, c0, n0, m0, bsz, n):
    nc = n // BLOCK
    nb = SCAN_SEQS
    assert nc % 2 == 0 and bsz % nb == 0
    spec = lambda shape: pl.BlockSpec((nb,) + shape, lambda b: (b,) + (0,) * len(shape))
    seq_in = (lambda shape: pl.BlockSpec((nb,) + shape, lambda b: (b,) + (0,) * len(shape),
                                         pipeline_mode=pl.Buffered(1))) if bsz == nb else spec
    return pl.pallas_call(
        functools.partial(_mlstm_kernel, nc=nc, nb=nb),
        grid=(bsz // nb,),
        in_specs=[
            seq_in((n, 4 * HW)),
            seq_in((n, 2 * LANES)),
            spec((2, HW, LANES)),
            spec((2, 1, HW)),
            spec((2, 1, LANES)),
        ],
        out_specs=[
            spec((n, HW)),
            spec((2, HW, DH)),
            spec((2, 1, HW)),
            spec((2, 1, LANES)),
        ],
        out_shape=[
            jax.ShapeDtypeStruct((bsz, n, HW), bf16),
            jax.ShapeDtypeStruct((bsz, 2, HW, DH), f32),
            jax.ShapeDtypeStruct((bsz, 2, 1, HW), f32),
            jax.ShapeDtypeStruct((bsz, 2, 1, LANES), f32),
        ],
        scratch_shapes=[
            pltpu.VMEM((nb, n, HW), f32),
            pltpu.VMEM((nb, 2, HW, LANES), f32),
            pltpu.VMEM((nb, 2, 1, HW), f32),
            pltpu.VMEM((nb, 2, 1, LANES), f32),
        ],
        compiler_params=_params(1),
        name="mlstm",
    )(mq.reshape(bsz, n, 4 * HW), mg.reshape(bsz, n, 2 * LANES), c0, n0, m0)


def _ret_kernel(x_ref, s0_ref, dl_ref, dr_ref, y_ref, sn_ref, o_s, s_s, qd_s, kd_s, cd_s, dm_s, *, nc, nb):
    avg = jnp.where(lax.broadcasted_iota(jnp.int32, (HW, HW), 0) // DH ==
                    lax.broadcasted_iota(jnp.int32, (HW, HW), 1) // DH, 1.0 / DH, 0.0).astype(bf16)

    @pl.when(pl.program_id(0) == 0)
    def _decay_tables():
        rowf = lax.broadcasted_iota(jnp.int32, (BLOCK, HW), 0).astype(f32)
        rel = (lax.broadcasted_iota(jnp.int32, (BLOCK, BLOCK), 0) -
               lax.broadcasted_iota(jnp.int32, (BLOCK, BLOCK), 1)).astype(f32)
        for d in range(2):
            lg = _log_sigmoid(dl_ref[d])
            if d == 0:
                qd_s[d] = jnp.exp((rowf + 1.0) * lg)
                kd_s[d] = jnp.exp((BLOCK - 1.0 - rowf) * lg)
                reld = rel
            else:
                qd_s[d] = jnp.exp((BLOCK - rowf) * lg)
                kd_s[d] = jnp.exp(rowf * lg)
                reld = -rel
            cd_s[d] = jnp.exp(float(BLOCK) * _log_sigmoid(dr_ref[d]))
            for h in range(N_HEADS):
                lgh = lg[:, DH * h:DH * h + 1]
                dm_s[d, :, BLOCK * h:BLOCK * (h + 1)] = jnp.where(reld >= 0.0,
                                                                  jnp.exp(jnp.maximum(reld, 0.0) * lgh), 0.0)

    chains = [(bb, d) for bb in range(nb) for d in range(2)]
    for bb, d in chains:
        s_s[bb, d] = _expand_state(s0_ref[bb, d])

    def compute(d, blk, sm):
        q = blk[:, 0:HW]
        k = blk[:, HW:2 * HW] * (DH ** -0.5)
        v = blk[:, 2 * HW:3 * HW]
        kf = k.astype(f32)
        p = _pair_qk(q, kf.T.astype(bf16)) * dm_s[d]
        o = _pair_pv(p.astype(bf16), v) + qd_s[d] * _pair_state_read(q, sm.astype(bf16))
        kdt = (kf * kd_s[d]).T.astype(bf16)
        return o, cd_s[d] * sm + _pair_state_update(kdt, v)

    def step(i, second):
        r0s = [pl.multiple_of((i if d == 0 else nc - 1 - i) * BLOCK, BLOCK) for bb, d in chains]
        loaded = [(x_ref[bb, pl.ds(r0, BLOCK), 0:3 * HW], s_s[bb, d]) for (bb, d), r0 in zip(chains, r0s)]
        if second:
            other = [o_s[bb, pl.ds(r0, BLOCK), :] for (bb, d), r0 in zip(chains, r0s)]
            gates = [x_ref[bb, pl.ds(r0, BLOCK), 3 * HW:4 * HW] for (bb, d), r0 in zip(chains, r0s)]
        outs = [compute(d, *l) for (bb, d), l in zip(chains, loaded)]
        for idx, ((bb, d), r0) in enumerate(zip(chains, r0s)):
            o, snew = outs[idx]
            if second:
                o = o + other[idx]
                gate = gates[idx].astype(f32)
                dev = o - _dot2_l(o, avg)
                var = _dot((dev * dev).astype(bf16), avg)
                y = gate * _sigmoid(gate) * (dev * lax.rsqrt(var + EPS))
                y_ref[bb, pl.ds(r0, BLOCK), :] = y.astype(y_ref.dtype)
            else:
                o_s[bb, pl.ds(r0, BLOCK), :] = o
            s_s[bb, d] = snew

    def first_half(i, carry):
        step(i, False)
        return carry

    def second_half(i, carry):
        step(i, True)
        return carry

    lax.fori_loop(0, nc // 2, first_half, 0)
    lax.fori_loop(nc // 2, nc, second_half, 0)

    for bb, d in chains:
        sn_ref[bb, d] = _compact_state(s_s[bb, d])


def _retention(rt, s0, dl, dr, bsz, n):
    nc = n // BLOCK
    nb = SCAN_SEQS
    assert nc % 2 == 0 and bsz % nb == 0
    spec = lambda shape: pl.BlockSpec((nb,) + shape, lambda b: (b,) + (0,) * len(shape))
    seq_in = (lambda shape: pl.BlockSpec((nb,) + shape, lambda b: (b,) + (0,) * len(shape),
                                         pipeline_mode=pl.Buffered(1))) if bsz == nb else spec
    return pl.pallas_call(
        functools.partial(_ret_kernel, nc=nc, nb=nb),
        grid=(bsz // nb,),
        in_specs=[
            seq_in((n, 4 * HW)),
            spec((2, HW, LANES)),
            _const_spec((2, 1, HW)),
            _const_spec((2, HW, LANES)),
        ],
        out_specs=[
            spec((n, HW)),
            spec((2, HW, DH)),
        ],
        out_shape=[
            jax.ShapeDtypeStruct((bsz, n, HW), bf16),
            jax.ShapeDtypeStruct((bsz, 2, HW, DH), f32),
        ],
        scratch_shapes=[
            pltpu.VMEM((nb, n, HW), f32),
            pltpu.VMEM((nb, 2, HW, LANES), f32),
            pltpu.VMEM((2, BLOCK, HW), f32),
            pltpu.VMEM((2, BLOCK, HW), f32),
            pltpu.VMEM((2, HW, LANES), f32),
            pltpu.VMEM((2, BLOCK, N_HEADS * BLOCK), f32),
        ],
        compiler_params=_params(1),
        name="retention",
    )(rt.reshape(bsz, n, 4 * HW), s0, dl, dr)


def _attn_kernel(q_ref, k_ref, v_ref, ck_ref, cv_ref, cos_ref, sin_ref, sk_ref, o_ref, kt_s, vt_s,
                 *, n_ctx, n, band, qb):
    i = pl.program_id(1)
    lane = lax.broadcasted_iota(jnp.int32, (1, LANES), 1)
    lo = lane < DH
    first = (lane % (DH // 2)) < (DH // 4)
    gmasks = _head_lane_masks()
    n_cblk = n_ctx // BLOCK
    wide = A_GROUP * qb

    def tile4(x):
        xr = pltpu.roll(x, DH, 1)
        a2 = jnp.where(lo, x, xr).astype(bf16)
        b2 = jnp.where(lo, xr, x).astype(bf16)
        return jnp.concatenate([a2, a2], axis=1), jnp.concatenate([b2, b2], axis=1)

    def rope(x, cos, sin):
        sw = jnp.where(first, pltpu.roll(x, LANES - DH // 4, 1), pltpu.roll(x, DH // 4, 1))
        return x * cos + sw * sin

    def put_block(blk, r0, kk, vv):
        k0, k1 = tile4(kk)
        kt_s[0, pl.ds(r0, BLOCK), :] = k0
        kt_s[1, pl.ds(r0, BLOCK), :] = k1
        vt = vv.T
        for j in range(A_KV_HEADS):
            vj = vt[DH * j:DH * (j + 1)]
            vt_s[j, blk] = jnp.concatenate([vj, vj], axis=0).astype(bf16)

    @pl.when(i == 0)
    def _prepare_keys():
        for c in range(n_cblk):
            put_block(c, c * BLOCK, ck_ref[c * BLOCK:(c + 1) * BLOCK, :], cv_ref[c * BLOCK:(c + 1) * BLOCK, :])
        if band:
            nb = n // BLOCK
            for blk in (n_cblk, n_cblk + 1 + nb):
                for j in range(A_KV_HEADS):
                    kt_s[j, blk * BLOCK:(blk + 1) * BLOCK, :] = jnp.zeros((BLOCK, HW), bf16)
                    vt_s[j, blk] = jnp.zeros((BLOCK, BLOCK), bf16)

            def body(c, carry):
                r0 = pl.multiple_of(c * BLOCK, BLOCK)
                kk = rope(k_ref[pl.ds(r0, BLOCK), :], cos_ref[pl.ds(r0, BLOCK), :], sin_ref[pl.ds(r0, BLOCK), :])
                put_block(n_cblk + 1 + c, pl.multiple_of(n_ctx + BLOCK + r0, BLOCK), kk, v_ref[pl.ds(r0, BLOCK), :])
                return carry

            lax.fori_loop(0, nb, body, 0)

    q = q_ref[...]
    if band:
        q0 = pl.multiple_of(i * BLOCK, BLOCK)
        cos = cos_ref[pl.ds(q0, BLOCK), :]
        sin = sin_ref[pl.ds(q0, BLOCK), :]
        q = jnp.concatenate([rope(q[:, LANES * t:LANES * (t + 1)], cos, sin) for t in range(A_W // LANES)], axis=1)
        kj = lax.broadcasted_iota(jnp.int32, (3 * BLOCK, BLOCK), 0)
        qi = lax.broadcasted_iota(jnp.int32, (3 * BLOCK, BLOCK), 1)
        tok = kj + (i - 1) * BLOCK
        ok = (kj >= qi) & (kj <= qi + 2 * WINDOW) & (tok >= 0) & (tok < n)
        bias = jnp.where(ok, 0.0, NEG)
        bias4 = jnp.concatenate([bias] * A_GROUP, axis=1)
        b0 = pl.multiple_of(n_ctx + i * BLOCK, BLOCK)
        vband = [vt_s[j, pl.ds(n_cblk + i, 3)] for j in range(A_KV_HEADS)]
    q = q * (DH ** -0.5)
    row_lo = lax.broadcasted_iota(jnp.int32, (2 * DH, qb), 0) < DH

    outs = []
    for j in range(A_KV_HEADS):
        q256 = q[:, HW * j:HW * (j + 1)]
        qs = jnp.concatenate([jnp.where(gmasks[g], q256, 0.0) for g in range(A_GROUP)], axis=0).astype(bf16)
        sink = jnp.concatenate(
            [jnp.broadcast_to(sk_ref[A_GROUP * j + g:A_GROUP * j + g + 1, 0:1], (1, qb)) for g in range(A_GROUP)],
            axis=1)
        s1 = _dot_nt(kt_s[j, 0:n_ctx, :], qs)
        m = jnp.maximum(jnp.max(s1, axis=0, keepdims=True), sink)
        if band:
            s2 = _dot_nt(kt_s[j, pl.ds(b0, 3 * BLOCK), :], qs) + bias4
            m = jnp.maximum(m, jnp.max(s2, axis=0, keepdims=True))
        p1 = jnp.exp(s1 - m)
        den = jnp.sum(p1, axis=0, keepdims=True) + jnp.exp(sink - m)
        vctx = jnp.concatenate([vt_s[j, c] for c in range(n_cblk)], axis=1)
        acc = _dot(vctx, p1.astype(bf16))
        if band:
            p2 = jnp.exp(s2 - m)
            den = den + jnp.sum(p2, axis=0, keepdims=True)
            acc = acc + _dot(jnp.concatenate([vband[j][t] for t in range(3)], axis=1), p2.astype(bf16))
        acc = acc * (1.0 / den)
        for g in range(0, A_GROUP, 2):
            pair = jnp.where(row_lo, acc[:, qb * g:qb * (g + 1)], acc[:, qb * (g + 1):qb * (g + 2)])
            outs.append(pair.T)
    o_ref[...] = jnp.concatenate(outs, axis=1).astype(o_ref.dtype)


def _attention(at, ck, cv, ck_spec, cv_spec, cos, sin, sk, bsz, n, n_ctx, band):
    qb = BLOCK if band else min(n, 2 * BLOCK)
    nq = n // qb
    s_tot = n_ctx + (n + 2 * BLOCK if band else 0)
    at3 = at.reshape(bsz, n, A_W + 2 * A_KV_W)
    kcol = A_W // A_KV_W
    return pl.pallas_call(
        functools.partial(_attn_kernel, n_ctx=n_ctx, n=n, band=band, qb=qb),
        grid=(bsz, nq),
        in_specs=[
            pl.BlockSpec((None, qb, A_W), lambda b, i: (b, i, 0)),
            pl.BlockSpec((None, n, A_KV_W), lambda b, i: (b, 0, kcol)),
            pl.BlockSpec((None, n, A_KV_W), lambda b, i: (b, 0, kcol + 1)),
            ck_spec,
            cv_spec,
            _const_spec(cos.shape),
            _const_spec(sin.shape),
            _const_spec((A_HEADS, LANES)),
        ],
        out_specs=pl.BlockSpec((None, qb, A_W), lambda b, i: (b, i, 0)),
        out_shape=jax.ShapeDtypeStruct((bsz, n, A_W), bf16),
        scratch_shapes=[
            pltpu.VMEM((A_KV_HEADS, s_tot, HW), bf16),
            pltpu.VMEM((A_KV_HEADS, s_tot // BLOCK, BLOCK, BLOCK), bf16),
        ],
        compiler_params=_params(2),
        name="attention",
    )(at3, at3, at3, ck, cv, cos, sin, sk)


def _rope_tables(n):
    tok = np.arange(n)
    pos = np.stack([tok // GRID_W, tok % GRID_W], axis=1).astype(np.float32)
    quarter = DH // 4
    freqs = np.power(np.float32(ROPE_BASE), -np.arange(quarter, dtype=np.float32) / np.float32(quarter))
    lane = np.arange(LANES) % DH
    axis = lane // (DH // 2)
    fidx = lane % quarter
    sign = np.where((lane % (DH // 2)) < quarter, -1.0, 1.0)
    ang = (pos[:, axis] * freqs.astype(np.float32)[fidx][None, :]).astype(np.float32).astype(np.float64)
    return jnp.asarray(np.cos(ang), f32), jnp.asarray(np.sin(ang) * sign[None, :], f32)


CONV_PAD = 16


def _conv_kernel(u_ref, w_ref, b_ref, lg_ref, lb_ref, y_ref, z_s, *, n):
    nc = n // BLOCK
    z_s[0:CONV_PAD, :] = jnp.zeros((CONV_PAD, HW), f32)
    z_s[CONV_PAD + n:2 * CONV_PAD + n, :] = jnp.zeros((CONV_PAD, HW), f32)

    def glu(c, carry):
        r0 = pl.multiple_of(c * BLOCK, BLOCK)
        blk = u_ref[pl.ds(r0, BLOCK), :].astype(f32)
        z_s[pl.ds(pl.multiple_of(r0 + CONV_PAD, SUBLANES), BLOCK), :] = blk[:, 0:HW] * _sigmoid(blk[:, HW:2 * HW])
        return carry

    lax.fori_loop(0, nc, glu, 0)

    first = CONV_PAD - CONV_K // 2
    span = BLOCK + 2 * CONV_PAD

    def tile(c, carry):
        r0 = pl.multiple_of(c * BLOCK, BLOCK)
        win = z_s[pl.ds(r0, span), :]
        acc = jnp.zeros((BLOCK, HW), f32)
        for r in range(SUBLANES):
            taps = [kk for kk in range(CONV_K) if (first + kk) % SUBLANES == r]
            if not taps:
                continue
            shifted = win if r == 0 else pltpu.roll(win, span - r, 0)
            for kk in taps:
                a = (first + kk) // SUBLANES * SUBLANES
                acc = acc + shifted[a:a + BLOCK, :] * w_ref[kk:kk + 1, :]
        zc = acc + b_ref[...]
        mu = jnp.mean(zc, axis=-1, keepdims=True)
        dev = zc - mu
        var = jnp.mean(dev * dev, axis=-1, keepdims=True)
        t = dev * lax.rsqrt(var + EPS) * lg_ref[...] + lb_ref[...]
        y_ref[pl.ds(r0, BLOCK), :] = (t * _sigmoid(t)).astype(y_ref.dtype)
        return carry

    lax.fori_loop(0, nc, tile, 0)


def _conv(cu, w, b, lg, lb, bsz, n):
    return pl.pallas_call(
        functools.partial(_conv_kernel, n=n),
        grid=(bsz,),
        in_specs=[
            pl.BlockSpec((None, n, 2 * HW), lambda bb: (bb, 0, 0)),
            _const_spec((CONV_K, HW)),
            _const_spec((1, HW)),
            _const_spec((1, HW)),
            _const_spec((1, HW)),
        ],
        out_specs=pl.BlockSpec((None, n, HW), lambda bb: (bb, 0, 0)),
        out_shape=jax.ShapeDtypeStruct((bsz, n, HW), bf16),
        scratch_shapes=[pltpu.VMEM((n + 2 * CONV_PAD, HW), f32)],
        compiler_params=_params(1),
        name="conv",
    )(cu.reshape(bsz, n, 2 * HW), w, b, lg, lb)


def _post_kernel(x_ref, ym_ref, ya_ref, yc_ref, yr_ref, gt_ref, mod_ref, ng_ref, wbr_ref, wout_ref, w1_ref, w2_ref,
                 o_ref):
    def rms(a):
        return a * lax.rsqrt(jnp.mean(a * a, axis=-1, keepdims=True) + EPS)

    acc = None
    for i, y_ref in enumerate((ym_ref, ya_ref, yc_ref, yr_ref)):
        br = _dot(y_ref[...], wbr_ref[BR_OFFS[i]:BR_OFFS[i + 1], :])
        gate = _sigmoid(gt_ref[:, D_MODEL * i:D_MODEL * (i + 1)].astype(f32))
        acc = gate * br if acc is None else acc + gate * br
    mix = _dot(acc.astype(bf16), wout_ref[...])
    x = x_ref[...] + mod_ref[2:3, :] * (rms(mix) * ng_ref[1:2, :])
    h = (rms(x) * ng_ref[2:3, :] * (1.0 + mod_ref[4:5, :]) + mod_ref[3:4, :]).astype(bf16)
    chunk = 512
    f = jnp.zeros(x.shape, f32)
    for c in range(0, D_FF, chunk):
        a = jnp.maximum(_dot(h, w1_ref[:, c:c + chunk]), 0.0)
        f = f + _dot((a * a).astype(bf16), w2_ref[c:c + chunk, :])
    o_ref[...] = x + mod_ref[5:6, :] * (rms(f) * ng_ref[3:4, :])


def _post(x2d, ym, ya, yc, yr, gt, mod4, lw, layer, row_of_tile):
    m = x2d.shape[0]
    tm = ROW_TILE
    rows = lambda w: pl.BlockSpec((tm, w), lambda i: (i, 0))
    return pl.pallas_call(
        _post_kernel,
        grid=(m // tm,),
        in_specs=[
            rows(D_MODEL), rows(HW), rows(A_W), rows(HW), rows(HW), rows(N_BRANCH * D_MODEL),
            pl.BlockSpec((None, None, 6, D_MODEL), lambda i: (layer, row_of_tile(i), 0, 0)),
            _const_spec((4, D_MODEL)),
            _const_spec((BR_OFFS[-1], D_MODEL)),
            _const_spec((D_MODEL, D_MODEL)),
            _const_spec((D_MODEL, D_FF)),
            _const_spec((D_FF, D_MODEL)),
        ],
        out_specs=rows(D_MODEL),
        out_shape=jax.ShapeDtypeStruct((m, D_MODEL), f32),
        compiler_params=_params(1),
        name="post",
    )(x2d, ym.reshape(m, HW), ya.reshape(m, A_W), yc.reshape(m, HW), yr.reshape(m, HW), gt, mod4, lw['ng'],
      lw['w_branch'], lw['w_out'], lw['w_mlp1'], lw['w_mlp2'])


PACK_ROWS = 512


def _pack_kernel(wt_ref, o_ref):
    s = pl.program_id(1)
    gate_step = PK_MG // PACK_ROWS

    @pl.when(s != gate_step)
    def _plain():
        o_ref[...] = wt_ref[0].astype(bf16)

    @pl.when(s == gate_step)
    def _gates():
        row = lax.broadcasted_iota(jnp.int32, (LANES, 1), 0)
        gates = wt_ref[0, 0:LANES, :]
        o_ref[0:LANES, :] = jnp.where(row < N_GATES, gates, 0.0).astype(bf16)
        shifted = pltpu.roll(gates, LANES - N_HEADS, 0)
        o_ref[LANES:2 * LANES, :] = jnp.where(row < N_GATES - N_HEADS, shifted, 0.0).astype(bf16)
        o_ref[2 * LANES:PACK_ROWS, :] = wt_ref[0, N_GATES:N_GATES + PACK_ROWS - 2 * LANES, :].astype(bf16)


def _pack_w_in(w_in, b_in):
    gate_step = PK_MG // PACK_ROWS
    shift = 2 * LANES - N_GATES
    assert PK_MG % PACK_ROWS == 0 and PK_W % PACK_ROWS == 0 and PACK_ROWS >= 2 * LANES

    def src_row(s):
        return pl.multiple_of(jnp.where(s <= gate_step, PACK_ROWS * s, PACK_ROWS * s - shift), N_GATES)

    w_pk = pl.pallas_call(
        _pack_kernel,
        grid=(DEPTH, PK_W // PACK_ROWS),
        in_specs=[pl.BlockSpec((pl.Element(1), pl.Element(PACK_ROWS), pl.Element(D_MODEL)),
                               lambda l, s: (l, src_row(s), 0))],
        out_specs=pl.BlockSpec((None, PACK_ROWS, D_MODEL), lambda l, s: (l, s, 0)),
        out_shape=jax.ShapeDtypeStruct((DEPTH, PK_W, D_MODEL), bf16),
        compiler_params=_params(2),
        name="pack_w_in",
    )(jnp.swapaxes(w_in, 1, 2))
    g0 = PK_MG
    pad = lambda cols: jnp.zeros((DEPTH, cols), f32)
    b_pk = jnp.concatenate(
        [b_in[:, :g0 + N_GATES], pad(LANES - N_GATES), b_in[:, g0 + N_HEADS:g0 + N_GATES],
         pad(LANES - N_GATES + N_HEADS), b_in[:, g0 + N_GATES:]], axis=-1)
    return w_pk, b_pk[:, None, :]


def _stack_state(s):
    bsz = s.shape[0]
    s = s.reshape(bsz, 2, HW, DH)
    return jnp.pad(s, ((0, 0), (0, 0), (0, 0), (0, LANES - DH)))


def _layer(x2d, bsz, n, layer, mod4, lw, cache, row_of_tile):
    is_ctx = cache is None
    mq, mg, at, cv, rt, gt = _inproj(x2d, mod4, lw['ng'], lw['w_in'], lw['b_in'], layer, row_of_tile)

    if is_ctx:
        c0 = jnp.zeros((bsz, 2, HW, LANES), f32)
        n0 = jnp.zeros((bsz, 2, 1, HW), f32)
        m0 = jnp.zeros((bsz, 2, 1, LANES), f32)
        s0 = jnp.zeros((bsz, 2, HW, LANES), f32)
    else:
        c0 = _stack_state(cache['C'])
        n0 = cache['n'].reshape(bsz, 2, 1, HW)
        m0 = jnp.stack([jnp.pad(cache['m'][:, d], ((0, 0), (2 * N_HEADS * d, LANES - N_HEADS - 2 * N_HEADS * d)))
                        for d in range(2)], axis=1)[:, :, None, :]
        s0 = _stack_state(cache['S'])
    ym, c_fin, n_fin, m_fin = _mlstm(mq, mg, c0, n0, m0, bsz, n)
    yr, s_fin = _retention(rt, s0, lw['ret_dl'], lw['ret_dr'], bsz, n)

    at3 = at.reshape(bsz, n, A_W + 2 * A_KV_W)
    kcol = A_W // A_KV_W
    if is_ctx:
        dummy = jnp.zeros((SUBLANES, LANES), f32)
        ck_spec = pl.BlockSpec((None, n, A_KV_W), lambda b, i: (b, 0, kcol))
        cv_spec = pl.BlockSpec((None, n, A_KV_W), lambda b, i: (b, 0, kcol + 1))
        ya = _attention(at, at3, at3, ck_spec, cv_spec, dummy, dummy, lw['sink'], bsz, n, n, False)
    else:
        n_ctx = cache['k'].shape[2]
        ck_spec = pl.BlockSpec((None, None, n_ctx, A_KV_W), lambda b, i: (b, layer, 0, 0))
        cos, sin = _rope_tables(n)
        ya = _attention(at, cache['k'], cache['v'], ck_spec, ck_spec, cos, sin, lw['sink'], bsz, n, n_ctx, True)

    yc = _conv(cv, lw['conv_w'], lw['conv_b'], lw['conv_ln_g'], lw['conv_ln_b'], bsz, n)

    x2d = _post(x2d, ym, ya, yc, yr, gt, mod4, lw, layer, row_of_tile)

    ctx = None
    if is_ctx:
        ctx = (at3[:, :, A_W:A_W + A_KV_W].reshape(bsz, n, A_KV_HEADS, DH),
               at3[:, :, A_W + A_KV_W:].reshape(bsz, n, A_KV_HEADS, DH),
               c_fin.reshape(bsz, 2, N_HEADS, DH, DH),
               n_fin.reshape(bsz, 2, N_HEADS, DH),
               jnp.stack([m_fin[:, d, 0, 2 * N_HEADS * d:2 * N_HEADS * d + N_HEADS] for d in range(2)], axis=1),
               s_fin.reshape(bsz, 2, N_HEADS, DH, DH))
    return x2d, ctx


def kernel(x_prompt, x_sample, c, cache_k, cache_v, state_mlstm_C, state_mlstm_n, state_mlstm_m, state_ret,
           c_ctx, w_ada, b_ada, norm_g, w_in, b_in, w_branch, w_out, attn_sink, ret_decay_logit,
           conv_w, conv_b, conv_ln_g, conv_ln_b, w_mlp1, w_mlp2):
    bsz, seq, _ = x_prompt.shape
    dbsz, dseq, _ = x_sample.shape
    past = cache_k.shape[2]

    c_rows = jnp.zeros((SUBLANES, D_MODEL), f32).at[0].set(c_ctx).at[1:1 + dbsz].set(c)
    mod4 = _ada(c_rows, w_ada, b_ada).reshape(DEPTH, SUBLANES, 6, D_MODEL)

    w_pk, b_pk = _pack_w_in(w_in, b_in)
    layers = []
    for l in range(DEPTH):
        layers.append({
            'ng': norm_g[l], 'w_in': w_pk, 'b_in': b_pk,
            'w_branch': w_branch[l].astype(bf16), 'w_out': w_out[l].astype(bf16),
            'w_mlp1': w_mlp1[l].astype(bf16), 'w_mlp2': w_mlp2[l].astype(bf16),
            'sink': jnp.broadcast_to(attn_sink[l][:, None], (A_HEADS, LANES)),
            'ret_dl': jnp.repeat(ret_decay_logit[l], DH, axis=-1).reshape(2, 1, HW),
            'ret_dr': jnp.broadcast_to(jnp.repeat(ret_decay_logit[l], DH, axis=-1)[:, :, None], (2, HW, LANES)),
            'conv_w': conv_w[l], 'conv_b': conv_b[l][None, :],
            'conv_ln_g': conv_ln_g[l][None, :], 'conv_ln_b': conv_ln_b[l][None, :],
        })

    xp = x_prompt.reshape(bsz * seq, D_MODEL)
    ctxs = []
    for l in range(DEPTH):
        xp, ctx = _layer(xp, bsz, seq, l, mod4, layers[l], None, lambda i: 0)
        ctxs.append(ctx)

    xs = x_sample.reshape(dbsz * dseq, D_MODEL)
    tiles_per_seq = dseq // ROW_TILE
    ck = cache_k.reshape(dbsz, DEPTH, past, A_KV_W)
    cv = cache_v.reshape(dbsz, DEPTH, past, A_KV_W)
    for l in range(DEPTH):
        cache = {'k': ck, 'v': cv, 'C': state_mlstm_C[:, l], 'n': state_mlstm_n[:, l],
                 'm': state_mlstm_m[:, l], 'S': state_ret[:, l]}
        xs, _ = _layer(xs, dbsz, dseq, l, mod4, layers[l], cache, lambda i: 1 + i // tiles_per_seq)

    stack = lambda j: jnp.stack([ctxs[l][j] for l in range(DEPTH)], axis=1)
    return (xp.reshape(bsz, seq, D_MODEL), xs.reshape(dbsz, dseq, D_MODEL),
            stack(0), stack(1), stack(2), stack(3), stack(4), stack(5))
```

```python
import functools

import numpy as np
import jax
import jax.numpy as jnp
from jax import lax
from jax.experimental import pallas as pl
from jax.experimental.pallas import tpu as pltpu

f32 = jnp.float32
bf16 = jnp.bfloat16

D_MODEL = 1024
DEPTH = 2
GRID_W = 64
BLOCK = 128
N_HEADS = 4
DH = 64
HW = N_HEADS * DH
A_HEADS = 8
A_KV_HEADS = 2
A_GROUP = A_HEADS // A_KV_HEADS
A_W = A_HEADS * DH
A_KV_W = A_KV_HEADS * DH
WINDOW = 128
ROPE_BASE = 10000.0
CONV_K = 31
D_FF = 4 * D_MODEL
N_BRANCH = 4
EPS = 1e-6
NEG = -1e30
LOG2E = 1.4426950408889634
LANES = 128
SUBLANES = 8

N_GATES = 4 * N_HEADS
PK_MQ = 0
PK_MG = PK_MQ + 4 * HW
PK_AT = PK_MG + 2 * LANES
PK_CV = PK_AT + A_W + 2 * A_KV_W
PK_RT = PK_CV + 2 * HW
PK_GT = PK_RT + 4 * HW
PK_W = PK_GT + N_BRANCH * D_MODEL
BR_OFFS = (0, HW, HW + A_W, 2 * HW + A_W, 3 * HW + A_W)

VMEM_LIMIT = 56 * 1024 * 1024
ROW_TILE = 512


def _sigmoid(x):
    return 0.5 * (jnp.tanh(0.5 * x) + 1.0)


def _log_sigmoid(x):
    return jnp.minimum(x, 0.0) - jnp.log1p(jnp.exp(-jnp.abs(x)))


def _dot(a, b):
    return jnp.dot(a, b, preferred_element_type=f32)


def _dot_nt(a, b):
    return lax.dot_general(a, b, (((1,), (1,)), ((), ())), preferred_element_type=f32)


def _dot2_r(t, x):
    x1 = x.astype(bf16)
    x2 = (x - x1.astype(f32)).astype(bf16)
    r = _dot(t, jnp.concatenate([x1, x2], axis=1))
    return r[:, 0:x.shape[1]] + r[:, x.shape[1]:]


def _dot2_l(x, t):
    x1 = x.astype(bf16)
    x2 = (x - x1.astype(f32)).astype(bf16)
    r = _dot(jnp.concatenate([x1, x2], axis=0), t)
    return r[0:x.shape[0]] + r[x.shape[0]:]


def _const_spec(shape):
    nd = len(shape)
    return pl.BlockSpec(shape, lambda *_: (0,) * nd, pipeline_mode=pl.Buffered(1))


def _params(n_axes):
    return pltpu.CompilerParams(dimension_semantics=("arbitrary",) * n_axes, vmem_limit_bytes=VMEM_LIMIT)


def _ada_kernel(c_ref, w_ref, b_ref, o_ref):
    c = c_ref[...]
    s = (c * _sigmoid(c)).astype(bf16)
    o_ref[...] = _dot(s, w_ref[...].astype(bf16)) + b_ref[...]


def _ada(c_rows, w_ada, b_ada):
    tn = 1536
    n_out = 6 * D_MODEL
    return pl.pallas_call(
        _ada_kernel,
        grid=(DEPTH, n_out // tn),
        in_specs=[
            pl.BlockSpec((SUBLANES, D_MODEL), lambda l, j: (0, 0)),
            pl.BlockSpec((None, D_MODEL, tn), lambda l, j: (l, 0, j)),
            pl.BlockSpec((None, 1, tn), lambda l, j: (l, 0, j)),
        ],
        out_specs=pl.BlockSpec((None, SUBLANES, tn), lambda l, j: (l, 0, j)),
        out_shape=jax.ShapeDtypeStruct((DEPTH, SUBLANES, n_out), f32),
        compiler_params=_params(2),
        name="ada",
    )(c_rows, w_ada, b_ada.reshape(DEPTH, 1, n_out))


def _inproj_kernel(x_ref, mod_ref, ng_ref, w_ref, b_ref, mq_ref, mg_ref, at_ref, cv_ref, rt_ref, gt_ref):
    x = x_ref[...]
    y = x * lax.rsqrt(jnp.mean(x * x, axis=-1, keepdims=True) + EPS) * ng_ref[0:1, :]
    h = (y * (1.0 + mod_ref[1:2, :]) + mod_ref[0:1, :]).astype(bf16)
    chunk = 512
    for ref, base in ((mq_ref, PK_MQ), (mg_ref, PK_MG), (at_ref, PK_AT), (cv_ref, PK_CV), (rt_ref, PK_RT),
                      (gt_ref, PK_GT)):
        total = ref.shape[-1]
        for c in range(0, total, chunk):
            w = min(chunk, total - c)
            r = _dot_nt(h, w_ref[base + c:base + c + w, :]) + b_ref[:, base + c:base + c + w]
            ref[:, c:c + w] = r.astype(ref.dtype)


def _inproj(x2d, mod4, ng, w_pk, b_pk, layer, row_of_tile):
    m = x2d.shape[0]
    tm = ROW_TILE
    widths = (4 * HW, 2 * LANES, A_W + 2 * A_KV_W, 2 * HW, 4 * HW, N_BRANCH * D_MODEL)
    dtypes = (bf16, f32, f32, bf16, bf16, bf16)
    return pl.pallas_call(
        _inproj_kernel,
        grid=(m // tm,),
        in_specs=[
            pl.BlockSpec((tm, D_MODEL), lambda i: (i, 0)),
            pl.BlockSpec((None, None, 6, D_MODEL), lambda i: (layer, row_of_tile(i), 0, 0)),
            _const_spec((4, D_MODEL)),
            pl.BlockSpec((None, PK_W, D_MODEL), lambda i: (layer, 0, 0), pipeline_mode=pl.Buffered(1)),
            pl.BlockSpec((None, 1, PK_W), lambda i: (layer, 0, 0), pipeline_mode=pl.Buffered(1)),
        ],
        out_specs=[pl.BlockSpec((tm, w), lambda i: (i, 0)) for w in widths],
        out_shape=[jax.ShapeDtypeStruct((m, w), dt) for w, dt in zip(widths, dtypes)],
        compiler_params=_params(1),
        name="inproj",
    )(x2d, mod4, ng, w_pk, b_pk)


def _head_lane_masks():
    lane_head = lax.broadcasted_iota(jnp.int32, (1, HW), 1) // DH
    return [lane_head == h for h in range(N_HEADS)]


def _run_interleaved(gens):
    results = [None] * len(gens)
    active = list(range(len(gens)))
    while active:
        for idx in list(active):
            try:
                next(gens[idx])
            except StopIteration as stop:
                results[idx] = stop.value
                active.remove(idx)
    return results


PAIRS = N_HEADS // 2


def _pair_diag_mask():
    r = (lax.broadcasted_iota(jnp.int32, (HW, LANES), 0) // DH) % 2
    c = lax.broadcasted_iota(jnp.int32, (HW, LANES), 1) // DH
    return r == c


def _expand_state(x):
    return jnp.where(_pair_diag_mask(), x + pltpu.roll(x, DH, 1), 0.0)


def _compact_state(c):
    return (c + pltpu.roll(c, DH, 1))[:, 0:DH]


def _pair_qk(q, kt):
    zeros = jnp.zeros((DH, BLOCK), kt.dtype)
    outs = []
    for j in range(PAIRS):
        rows = kt[LANES * j:LANES * (j + 1)]
        blk = jnp.concatenate([jnp.concatenate([rows[0:DH], zeros], axis=0),
                               jnp.concatenate([zeros, rows[DH:2 * DH]], axis=0)], axis=1)
        outs.append(_dot(q[:, LANES * j:LANES * (j + 1)], blk))
    return jnp.concatenate(outs, axis=1)


def _pair_pv(p, v):
    low = lax.broadcasted_iota(jnp.int32, (1, LANES), 1) < DH
    outs = []
    for j in range(PAIRS):
        vp = v[:, LANES * j:LANES * (j + 1)]
        zero = jnp.zeros_like(vp)
        blk = jnp.concatenate([jnp.where(low, vp, zero), jnp.where(low, zero, vp)], axis=0)
        outs.append(_dot(p[:, 2 * BLOCK * j:2 * BLOCK * (j + 1)], blk))
    return jnp.concatenate(outs, axis=1)


def _pair_state_read(q, c):
    return jnp.concatenate([_dot(q[:, LANES * j:LANES * (j + 1)], c[LANES * j:LANES * (j + 1)])
                            for j in range(PAIRS)], axis=1)


def _pair_state_update(a_t, v):
    upd = jnp.concatenate([_dot(a_t[LANES * j:LANES * (j + 1)], v[:, LANES * j:LANES * (j + 1)])
                           for j in range(PAIRS)], axis=0)
    return jnp.where(_pair_diag_mask(), upd, 0.0)


def _mlstm_kernel(x_ref, g_ref, c0_ref, n0_ref, m0_ref, y_ref, cn_ref, nn_ref, mn_ref,
                  h_s, c_s, n_s, m_s, *, nc, nb):
    wide = N_HEADS * BLOCK
    row = lax.broadcasted_iota(jnp.int32, (BLOCK, BLOCK), 0)
    col = lax.broadcasted_iota(jnp.int32, (BLOCK, BLOCK), 1)
    tsum = (jnp.where(row >= col, 1.0, 0.0).astype(bf16), jnp.where(row <= col, 1.0, 0.0).astype(bf16))
    roww = lax.broadcasted_iota(jnp.int32, (BLOCK, wide), 0)
    colw = lax.broadcasted_iota(jnp.int32, (BLOCK, wide), 1) & (BLOCK - 1)
    tri4 = (roww >= colw, roww <= colw)
    masks = _head_lane_masks()
    lane = lax.broadcasted_iota(jnp.int32, (1, LANES), 1)
    row8 = lax.broadcasted_iota(jnp.int32, (SUBLANES, BLOCK), 0)

    def onehot(shape, row_expr, col_expr):
        r = lax.broadcasted_iota(jnp.int32, shape, 0)
        c = lax.broadcasted_iota(jnp.int32, shape, 1)
        return jnp.where(row_expr(r) == col_expr(c), 1.0, 0.0).astype(bf16)

    g0s = (0, 2 * N_HEADS)
    valid = tuple((lane >= g0) & (lane < g0 + N_HEADS) for g0 in g0s)
    hexp = tuple(onehot((LANES, HW), lambda r: r, lambda c, g0=g0: c // DH + g0) for g0 in g0s)
    hselg = tuple(onehot((HW, LANES), lambda r, g0=g0: r // DH + g0, lambda c: c) for g0 in g0s)

    chains = [(bb, d) for bb in range(nb) for d in range(2)]
    for bb, d in chains:
        c_s[bb, d] = _expand_state(c0_ref[bb, d])
        n_s[bb, d] = n0_ref[bb, d]
        m_s[bb, d] = m0_ref[bb, d]

    def load(bb, d, c):
        r0 = pl.multiple_of(c * BLOCK, BLOCK)
        return (r0, x_ref[bb, pl.ds(r0, BLOCK), 0:3 * HW], g_ref[bb, pl.ds(r0, BLOCK), :],
                c_s[bb, d], n_s[bb, d], m_s[bb, d])

    def compute(d, blk, g, cmat, nrow, m_row):
        g0 = g0s[d]
        q = blk[:, 0:HW]
        k = blk[:, HW:2 * HW] * (DH ** -0.5)
        v = blk[:, 2 * HW:3 * HW]
        ga = g[:, 0:LANES]
        gb = g[:, LANES:2 * LANES]
        bc = _dot2_r(tsum[d], _log_sigmoid(gb))
        yield
        beta = ga - bc
        beta_t = beta.T
        yield
        cmx = beta
        sh = 1
        while sh < BLOCK:
            if d == 0:
                cmx = jnp.maximum(cmx, jnp.where(row >= sh, pltpu.roll(cmx, sh, 0), NEG))
            else:
                cmx = jnp.maximum(cmx, jnp.where(row < BLOCK - sh, pltpu.roll(cmx, BLOCK - sh, 0), NEG))
            sh *= 2
        mx = jnp.maximum(m_row, cmx)
        alpha = jnp.where(valid[d], -mx, 0.0)
        w_int = jnp.where(valid[d], jnp.exp(m_row - mx), 0.0)
        e_nb = jnp.exp(alpha - bc)
        yield
        z = jnp.concatenate([jnp.broadcast_to(alpha[:, g0 + h:g0 + h + 1], (BLOCK, BLOCK)) for h in range(N_HEADS)],
                            axis=1)
        beta_w = jnp.concatenate([beta_t[g0 + h:g0 + h + 1, :] for h in range(N_HEADS)], axis=1)
        ktf = k.astype(f32).T
        s_wide = _pair_qk(q, ktf.astype(bf16))
        yield
        p = s_wide * jnp.exp(jnp.where(tri4[d], z + beta_w, NEG))
        num = _pair_pv(p.astype(bf16), v)
        yield
        den = jnp.zeros((BLOCK, LANES), f32)
        for h in range(N_HEADS):
            den = jnp.where(lane == g0 + h, jnp.sum(p[:, BLOCK * h:BLOCK * (h + 1)], axis=1, keepdims=True), den)
        qc = _pair_state_read(q, cmat.astype(bf16))
        qn = _dot((q.astype(f32) * nrow).astype(bf16), hselg[d])
        yield
        den = den + w_int * qn
        inv = jnp.where(valid[d], 1.0 / jnp.maximum(jnp.abs(den), e_nb), 0.0)
        fac = _dot(jnp.concatenate([w_int, inv], axis=0).astype(bf16), hexp[d])
        yield
        hout = (num + fac[0:BLOCK] * qc) * fac[BLOCK:2 * BLOCK]
        last = BLOCK - 1 if d == 0 else 0
        a_last = alpha[last:last + 1, :]
        m_new = jnp.where(valid[d], bc[last:last + 1, :] - a_last, 0.0)
        dec = jnp.exp(m_row + a_last)
        wr = jnp.zeros((SUBLANES, BLOCK), f32)
        kws = []
        for h in range(N_HEADS):
            w_row = jnp.exp(beta_t[g0 + h:g0 + h + 1, :] + a_last[:, g0 + h:g0 + h + 1])
            kws.append(ktf[DH * h:DH * (h + 1), :] * w_row)
            wr = jnp.where(row8 == h, w_row, wr)
        upd = _pair_state_update(jnp.concatenate(kws, axis=0).astype(bf16), v)
        wk = _dot(wr.astype(bf16), k)
        yield
        nnew = jnp.zeros((1, HW), f32)
        cnew = []
        for h in range(N_HEADS):
            rs = slice(DH * h, DH * (h + 1))
            dh = dec[:, g0 + h:g0 + h + 1]
            cnew.append(dh * cmat[rs, :] + upd[rs, :])
            nnew = jnp.where(masks[h], dh * nrow + wk[h:h + 1, :], nnew)
        return hout, jnp.concatenate(cnew, axis=0), nnew, m_new

    def step(i, second):
        loaded = [load(bb, d, i if d == 0 else nc - 1 - i) for bb, d in chains]
        if second:
            other = [h_s[bb, pl.ds(l[0], BLOCK), :] for (bb, d), l in zip(chains, loaded)]
            ogate = [x_ref[bb, pl.ds(l[0], BLOCK), 3 * HW:4 * HW] for (bb, d), l in zip(chains, loaded)]
        outs = _run_interleaved([compute(d, *l[1:]) for (bb, d), l in zip(chains, loaded)])
        for idx, ((bb, d), l) in enumerate(zip(chains, loaded)):
            hout, cnew, nnew, m_new = outs[idx]
            if second:
                y = _sigmoid(ogate[idx].astype(f32)) * (other[idx] + hout)
                y_ref[bb, pl.ds(l[0], BLOCK), :] = y.astype(y_ref.dtype)
            else:
                h_s[bb, pl.ds(l[0], BLOCK), :] = hout
            c_s[bb, d] = cnew
            n_s[bb, d] = nnew
            m_s[bb, d] = m_new

    def first_half(i, carry):
        step(i, False)
        return carry

    def second_half(i, carry):
        step(i, True)
        return carry

    lax.fori_loop(0, nc // 2, first_half, 0)
    lax.fori_loop(nc // 2, nc, second_half, 0)

    for bb, d in chains:
        cn_ref[bb, d] = _compact_state(c_s[bb, d])
        nn_ref[bb, d] = n_s[bb, d]
        mn_ref[bb, d] = m_s[bb, d]


SCAN_SEQS = 2


def _mlstm(mq, mg, c0, n0, m0, bsz, n):
    nc = n // BLOCK
    nb = SCAN_SEQS
    assert nc % 2 == 0 and bsz % nb == 0
    spec = lambda shape: pl.BlockSpec((nb,) + shape, lambda b: (b,) + (0,) * len(shape))
    seq_in = (lambda shape: pl.BlockSpec((nb,) + shape, lambda b: (b,) + (0,) * len(shape),
                                         pipeline_mode=pl.Buffered(1))) if bsz == nb else spec
    return pl.pallas_call(
        functools.partial(_mlstm_kernel, nc=nc, nb=nb),
        grid=(bsz // nb,),
        in_specs=[
            seq_in((n, 4 * HW)),
            seq_in((n, 2 * LANES)),
            spec((2, HW, LANES)),
            spec((2, 1, HW)),
            spec((2, 1, LANES)),
        ],
        out_specs=[
            spec((n, HW)),
            spec((2, HW, DH)),
            spec((2, 1, HW)),
            spec((2, 1, LANES)),
        ],
        out_shape=[
            jax.ShapeDtypeStruct((bsz, n, HW), bf16),
            jax.ShapeDtypeStruct((bsz, 2, HW, DH), f32),
            jax.ShapeDtypeStruct((bsz, 2, 1, HW), f32),
            jax.ShapeDtypeStruct((bsz, 2, 1, LANES), f32),
        ],
        scratch_shapes=[
            pltpu.VMEM((nb, n, HW), f32),
            pltpu.VMEM((nb, 2, HW, LANES), f32),
            pltpu.VMEM((nb, 2, 1, HW), f32),
            pltpu.VMEM((nb, 2, 1, LANES), f32),
        ],
        compiler_params=_params(1),
        name="mlstm",
    )(mq.reshape(bsz, n, 4 * HW), mg.reshape(bsz, n, 2 * LANES), c0, n0, m0)


def _ret_kernel(x_ref, s0_ref, dl_ref, dr_ref, y_ref, sn_ref, o_s, s_s, qd_s, kd_s, cd_s, dm_s, *, nc, nb):
    avg = jnp.where(lax.broadcasted_iota(jnp.int32, (HW, HW), 0) // DH ==
                    lax.broadcasted_iota(jnp.int32, (HW, HW), 1) // DH, 1.0 / DH, 0.0).astype(bf16)

    @pl.when(pl.program_id(0) == 0)
    def _decay_tables():
        rowf = lax.broadcasted_iota(jnp.int32, (BLOCK, HW), 0).astype(f32)
        rel = (lax.broadcasted_iota(jnp.int32, (BLOCK, BLOCK), 0) -
               lax.broadcasted_iota(jnp.int32, (BLOCK, BLOCK), 1)).astype(f32)
        for d in range(2):
            lg = _log_sigmoid(dl_ref[d])
            if d == 0:
                qd_s[d] = jnp.exp((rowf + 1.0) * lg)
                kd_s[d] = jnp.exp((BLOCK - 1.0 - rowf) * lg)
                reld = rel
            else:
                qd_s[d] = jnp.exp((BLOCK - rowf) * lg)
                kd_s[d] = jnp.exp(rowf * lg)
                reld = -rel
            cd_s[d] = jnp.exp(float(BLOCK) * _log_sigmoid(dr_ref[d]))
            for h in range(N_HEADS):
                lgh = lg[:, DH * h:DH * h + 1]
                dm_s[d, :, BLOCK * h:BLOCK * (h + 1)] = jnp.where(reld >= 0.0,
                                                                  jnp.exp(jnp.maximum(reld, 0.0) * lgh), 0.0)

    chains = [(bb, d) for bb in range(nb) for d in range(2)]
    for bb, d in chains:
        s_s[bb, d] = _expand_state(s0_ref[bb, d])

    def compute(d, blk, sm, other, gate):
        q = blk[:, 0:HW]
        k = blk[:, HW:2 * HW] * (DH ** -0.5)
        v = blk[:, 2 * HW:3 * HW]
        kf = k.astype(f32)
        p = _pair_qk(q, kf.T.astype(bf16)) * dm_s[d]
        yield
        o = _pair_pv(p.astype(bf16), v) + qd_s[d] * _pair_state_read(q, sm.astype(bf16))
        yield
        kdt = (kf * kd_s[d]).T.astype(bf16)
        snew = cd_s[d] * sm + _pair_state_update(kdt, v)
        yield
        if other is None:
            return o, snew
        o = o + other
        gate = gate.astype(f32)
        dev = o - _dot2_l(o, avg)
        yield
        var = _dot((dev * dev).astype(bf16), avg)
        yield
        return gate * _sigmoid(gate) * (dev * lax.rsqrt(var + EPS)), snew

    def step(i, second):
        r0s = [pl.multiple_of((i if d == 0 else nc - 1 - i) * BLOCK, BLOCK) for bb, d in chains]
        loaded = [(x_ref[bb, pl.ds(r0, BLOCK), 0:3 * HW], s_s[bb, d]) for (bb, d), r0 in zip(chains, r0s)]
        if second:
            other = [o_s[bb, pl.ds(r0, BLOCK), :] for (bb, d), r0 in zip(chains, r0s)]
            gates = [x_ref[bb, pl.ds(r0, BLOCK), 3 * HW:4 * HW] for (bb, d), r0 in zip(chains, r0s)]
        else:
            other = gates = [None] * len(chains)
        outs = _run_interleaved([compute(d, *l, other[idx], gates[idx])
                                 for idx, ((bb, d), l) in enumerate(zip(chains, loaded))])
        for idx, ((bb, d), r0) in enumerate(zip(chains, r0s)):
            o, snew = outs[idx]
            if second:
                y_ref[bb, pl.ds(r0, BLOCK), :] = o.astype(y_ref.dtype)
            else:
                o_s[bb, pl.ds(r0, BLOCK), :] = o
            s_s[bb, d] = snew

    def first_half(i, carry):
        step(i, False)
        return carry

    def second_half(i, carry):
        step(i, True)
        return carry

    lax.fori_loop(0, nc // 2, first_half, 0)
    lax.fori_loop(nc // 2, nc, second_half, 0)

    for bb, d in chains:
        sn_ref[bb, d] = _compact_state(s_s[bb, d])


def _retention(rt, s0, dl, dr, bsz, n):
    nc = n // BLOCK
    nb = SCAN_SEQS
    assert nc % 2 == 0 and bsz % nb == 0
    spec = lambda shape: pl.BlockSpec((nb,) + shape, lambda b: (b,) + (0,) * len(shape))
    seq_in = (lambda shape: pl.BlockSpec((nb,) + shape, lambda b: (b,) + (0,) * len(shape),
                                         pipeline_mode=pl.Buffered(1))) if bsz == nb else spec
    return pl.pallas_call(
        functools.partial(_ret_kernel, nc=nc, nb=nb),
        grid=(bsz // nb,),
        in_specs=[
            seq_in((n, 4 * HW)),
            spec((2, HW, LANES)),
            _const_spec((2, 1, HW)),
            _const_spec((2, HW, LANES)),
        ],
        out_specs=[
            spec((n, HW)),
            spec((2, HW, DH)),
        ],
        out_shape=[
            jax.ShapeDtypeStruct((bsz, n, HW), bf16),
            jax.ShapeDtypeStruct((bsz, 2, HW, DH), f32),
        ],
        scratch_shapes=[
            pltpu.VMEM((nb, n, HW), f32),
            pltpu.VMEM((nb, 2, HW, LANES), f32),
            pltpu.VMEM((2, BLOCK, HW), f32),
            pltpu.VMEM((2, BLOCK, HW), f32),
            pltpu.VMEM((2, HW, LANES), f32),
            pltpu.VMEM((2, BLOCK, N_HEADS * BLOCK), f32),
        ],
        compiler_params=_params(1),
        name="retention",
    )(rt.reshape(bsz, n, 4 * HW), s0, dl, dr)


def _attn_kernel(q_ref, k_ref, v_ref, ck_ref, cv_ref, cos_ref, sin_ref, sk_ref, o_ref, kt_s, vt_s,
                 *, n_ctx, n, band, qb, qps):
    i = pl.program_id(1)
    lane = lax.broadcasted_iota(jnp.int32, (1, LANES), 1)
    lo = lane < DH
    first = (lane % (DH // 2)) < (DH // 4)
    gmasks = _head_lane_masks()
    n_cblk = n_ctx // BLOCK
    wide = A_GROUP * qb

    def tile4(x):
        xr = pltpu.roll(x, DH, 1)
        a2 = jnp.where(lo, x, xr).astype(bf16)
        b2 = jnp.where(lo, xr, x).astype(bf16)
        return jnp.concatenate([a2, a2], axis=1), jnp.concatenate([b2, b2], axis=1)

    def rope(x, cos, sin):
        sw = jnp.where(first, pltpu.roll(x, LANES - DH // 4, 1), pltpu.roll(x, DH // 4, 1))
        return x * cos + sw * sin

    def put_block(blk, r0, kk, vv):
        k0, k1 = tile4(kk)
        kt_s[0, pl.ds(r0, BLOCK), :] = k0
        kt_s[1, pl.ds(r0, BLOCK), :] = k1
        vt = vv.T
        for j in range(A_KV_HEADS):
            vj = vt[DH * j:DH * (j + 1)]
            vt_s[j, blk] = jnp.concatenate([vj, vj], axis=0).astype(bf16)

    @pl.when(i == 0)
    def _prepare_keys():
        for c in range(n_cblk):
            put_block(c, c * BLOCK, ck_ref[c * BLOCK:(c + 1) * BLOCK, :], cv_ref[c * BLOCK:(c + 1) * BLOCK, :])
        if band:
            nb = n // BLOCK
            for blk in (n_cblk, n_cblk + 1 + nb):
                for j in range(A_KV_HEADS):
                    kt_s[j, blk * BLOCK:(blk + 1) * BLOCK, :] = jnp.zeros((BLOCK, HW), bf16)
                    vt_s[j, blk] = jnp.zeros((BLOCK, BLOCK), bf16)

            def body(c, carry):
                r0 = pl.multiple_of(c * BLOCK, BLOCK)
                kk = rope(k_ref[pl.ds(r0, BLOCK), :], cos_ref[pl.ds(r0, BLOCK), :], sin_ref[pl.ds(r0, BLOCK), :])
                put_block(n_cblk + 1 + c, pl.multiple_of(n_ctx + BLOCK + r0, BLOCK), kk, v_ref[pl.ds(r0, BLOCK), :])
                return carry

            lax.fori_loop(0, nb, body, 0)

    row_lo = lax.broadcasted_iota(jnp.int32, (2 * DH, qb), 0) < DH

    def query_block(u):
        q = q_ref[qb * u:qb * (u + 1), :]
        if not band:
            return q * (DH ** -0.5 * LOG2E), None, None
        iq = i * qps + u
        q0 = pl.multiple_of(iq * BLOCK, BLOCK)
        cos = cos_ref[pl.ds(q0, BLOCK), :]
        sin = sin_ref[pl.ds(q0, BLOCK), :]
        q = jnp.concatenate([rope(q[:, LANES * t:LANES * (t + 1)], cos, sin) for t in range(A_W // LANES)], axis=1)
        kj = lax.broadcasted_iota(jnp.int32, (3 * BLOCK, BLOCK), 0)
        qi = lax.broadcasted_iota(jnp.int32, (3 * BLOCK, BLOCK), 1)
        tok = kj + (iq - 1) * BLOCK
        ok = (kj >= qi) & (kj <= qi + 2 * WINDOW) & (tok >= 0) & (tok < n)
        bias = jnp.where(ok, 0.0, NEG)
        return (q * (DH ** -0.5 * LOG2E), jnp.concatenate([bias] * A_GROUP, axis=1),
                (pl.multiple_of(n_ctx + iq * BLOCK, BLOCK), n_cblk + iq))

    def kv_head(j, q, bias4, pos):
        q256 = q[:, HW * j:HW * (j + 1)]
        qs = jnp.concatenate([jnp.where(gmasks[g], q256, 0.0) for g in range(A_GROUP)], axis=0).astype(bf16)
        sink = jnp.concatenate(
            [jnp.broadcast_to(sk_ref[A_GROUP * j + g:A_GROUP * j + g + 1, 0:1], (1, qb)) for g in range(A_GROUP)],
            axis=1) * LOG2E
        s1 = _dot_nt(kt_s[j, 0:n_ctx, :], qs)
        yield
        m = jnp.maximum(jnp.max(s1, axis=0, keepdims=True), sink)
        if band:
            s2 = _dot_nt(kt_s[j, pl.ds(pos[0], 3 * BLOCK), :], qs) + bias4
            vband = vt_s[j, pl.ds(pos[1], 3)]
            yield
            m = jnp.maximum(m, jnp.max(s2, axis=0, keepdims=True))
        p1 = jnp.exp2(s1 - m)
        den = jnp.sum(p1, axis=0, keepdims=True) + jnp.exp2(sink - m)
        vctx = jnp.concatenate([vt_s[j, c] for c in range(n_cblk)], axis=1)
        acc = _dot(vctx, p1.astype(bf16))
        yield
        if band:
            p2 = jnp.exp2(s2 - m)
            den = den + jnp.sum(p2, axis=0, keepdims=True)
            acc = acc + _dot(jnp.concatenate([vband[t] for t in range(3)], axis=1), p2.astype(bf16))
            yield
        acc = acc * (1.0 / den)
        pairs = []
        for g in range(0, A_GROUP, 2):
            pair = jnp.where(row_lo, acc[:, qb * g:qb * (g + 1)], acc[:, qb * (g + 1):qb * (g + 2)])
            pairs.append(pair.T)
        return pairs

    blocks = [query_block(u) for u in range(qps)]
    outs = _run_interleaved([kv_head(j, *blocks[u]) for u in range(qps) for j in range(A_KV_HEADS)])
    for u in range(qps):
        tiles = [t for pairs in outs[A_KV_HEADS * u:A_KV_HEADS * (u + 1)] for t in pairs]
        o_ref[qb * u:qb * (u + 1), :] = jnp.concatenate(tiles, axis=1).astype(o_ref.dtype)


def _attention(at, ck, cv, ck_spec, cv_spec, cos, sin, sk, bsz, n, n_ctx, band):
    qb = BLOCK if band else min(n, 2 * BLOCK)
    qps = 2 if band else 1
    nq = n // (qb * qps)
    s_tot = n_ctx + (n + 2 * BLOCK if band else 0)
    at3 = at.reshape(bsz, n, A_W + 2 * A_KV_W)
    kcol = A_W // A_KV_W
    return pl.pallas_call(
        functools.partial(_attn_kernel, n_ctx=n_ctx, n=n, band=band, qb=qb, qps=qps),
        grid=(bsz, nq),
        in_specs=[
            pl.BlockSpec((None, qb * qps, A_W), lambda b, i: (b, i, 0)),
            pl.BlockSpec((None, n, A_KV_W), lambda b, i: (b, 0, kcol)),
            pl.BlockSpec((None, n, A_KV_W), lambda b, i: (b, 0, kcol + 1)),
            ck_spec,
            cv_spec,
            _const_spec(cos.shape),
            _const_spec(sin.shape),
            _const_spec((A_HEADS, LANES)),
        ],
        out_specs=pl.BlockSpec((None, qb * qps, A_W), lambda b, i: (b, i, 0)),
        out_shape=jax.ShapeDtypeStruct((bsz, n, A_W), bf16),
        scratch_shapes=[
            pltpu.VMEM((A_KV_HEADS, s_tot, HW), bf16),
            pltpu.VMEM((A_KV_HEADS, s_tot // BLOCK, BLOCK, BLOCK), bf16),
        ],
        compiler_params=_params(2),
        name="attention",
    )(at3, at3, at3, ck, cv, cos, sin, sk)


def _rope_tables(n):
    tok = np.arange(n)
    pos = np.stack([tok // GRID_W, tok % GRID_W], axis=1).astype(np.float32)
    quarter = DH // 4
    freqs = np.power(np.float32(ROPE_BASE), -np.arange(quarter, dtype=np.float32) / np.float32(quarter))
    lane = np.arange(LANES) % DH
    axis = lane // (DH // 2)
    fidx = lane % quarter
    sign = np.where((lane % (DH // 2)) < quarter, -1.0, 1.0)
    ang = (pos[:, axis] * freqs.astype(np.float32)[fidx][None, :]).astype(np.float32).astype(np.float64)
    return jnp.asarray(np.cos(ang), f32), jnp.asarray(np.sin(ang) * sign[None, :], f32)


CONV_PAD = 16


def _conv_kernel(u_ref, w_ref, b_ref, lg_ref, lb_ref, y_ref, z_s, *, n):
    nc = n // BLOCK
    z_s[0:CONV_PAD, :] = jnp.zeros((CONV_PAD, HW), f32)
    z_s[CONV_PAD + n:2 * CONV_PAD + n, :] = jnp.zeros((CONV_PAD, HW), f32)

    def glu(c, carry):
        r0 = pl.multiple_of(c * BLOCK, BLOCK)
        blk = u_ref[pl.ds(r0, BLOCK), :].astype(f32)
        z_s[pl.ds(pl.multiple_of(r0 + CONV_PAD, SUBLANES), BLOCK), :] = blk[:, 0:HW] * _sigmoid(blk[:, HW:2 * HW])
        return carry

    lax.fori_loop(0, nc, glu, 0)

    first = CONV_PAD - CONV_K // 2
    span = BLOCK + 2 * CONV_PAD

    def tile(c, carry):
        r0 = pl.multiple_of(c * BLOCK, BLOCK)
        win = z_s[pl.ds(r0, span), :]
        acc = jnp.zeros((BLOCK, HW), f32)
        for r in range(SUBLANES):
            taps = [kk for kk in range(CONV_K) if (first + kk) % SUBLANES == r]
            if not taps:
                continue
            shifted = win if r == 0 else pltpu.roll(win, span - r, 0)
            for kk in taps:
                a = (first + kk) // SUBLANES * SUBLANES
                acc = acc + shifted[a:a + BLOCK, :] * w_ref[kk:kk + 1, :]
        zc = acc + b_ref[...]
        mu = jnp.mean(zc, axis=-1, keepdims=True)
        dev = zc - mu
        var = jnp.mean(dev * dev, axis=-1, keepdims=True)
        t = dev * lax.rsqrt(var + EPS) * lg_ref[...] + lb_ref[...]
        y_ref[pl.ds(r0, BLOCK), :] = (t * _sigmoid(t)).astype(y_ref.dtype)
        return carry

    lax.fori_loop(0, nc, tile, 0)


def _conv(cu, w, b, lg, lb, bsz, n):
    return pl.pallas_call(
        functools.partial(_conv_kernel, n=n),
        grid=(bsz,),
        in_specs=[
            pl.BlockSpec((None, n, 2 * HW), lambda bb: (bb, 0, 0)),
            _const_spec((CONV_K, HW)),
            _const_spec((1, HW)),
            _const_spec((1, HW)),
            _const_spec((1, HW)),
        ],
        out_specs=pl.BlockSpec((None, n, HW), lambda bb: (bb, 0, 0)),
        out_shape=jax.ShapeDtypeStruct((bsz, n, HW), bf16),
        scratch_shapes=[pltpu.VMEM((n + 2 * CONV_PAD, HW), f32)],
        compiler_params=_params(1),
        name="conv",
    )(cu.reshape(bsz, n, 2 * HW), w, b, lg, lb)


def _post_kernel(x_ref, ym_ref, ya_ref, yc_ref, yr_ref, gt_ref, mod_ref, ng_ref, wbr_ref, wout_ref, w1_ref, w2_ref,
                 o_ref):
    def rms(a):
        return a * lax.rsqrt(jnp.mean(a * a, axis=-1, keepdims=True) + EPS)

    acc = None
    for i, y_ref in enumerate((ym_ref, ya_ref, yc_ref, yr_ref)):
        br = _dot(y_ref[...], wbr_ref[BR_OFFS[i]:BR_OFFS[i + 1], :])
        gate = _sigmoid(gt_ref[:, D_MODEL * i:D_MODEL * (i + 1)].astype(f32))
        acc = gate * br if acc is None else acc + gate * br
    mix = _dot(acc.astype(bf16), wout_ref[...])
    x = x_ref[...] + mod_ref[2:3, :] * (rms(mix) * ng_ref[1:2, :])
    h = (rms(x) * ng_ref[2:3, :] * (1.0 + mod_ref[4:5, :]) + mod_ref[3:4, :]).astype(bf16)
    chunk = 512
    f = jnp.zeros(x.shape, f32)
    for c in range(0, D_FF, chunk):
        a = jnp.maximum(_dot(h, w1_ref[:, c:c + chunk]), 0.0)
        f = f + _dot((a * a).astype(bf16), w2_ref[c:c + chunk, :])
    o_ref[...] = x + mod_ref[5:6, :] * (rms(f) * ng_ref[3:4, :])


def _post(x2d, ym, ya, yc, yr, gt, mod4, lw, layer, row_of_tile):
    m = x2d.shape[0]
    tm = ROW_TILE
    rows = lambda w: pl.BlockSpec((tm, w), lambda i: (i, 0))
    return pl.pallas_call(
        _post_kernel,
        grid=(m // tm,),
        in_specs=[
            rows(D_MODEL), rows(HW), rows(A_W), rows(HW), rows(HW), rows(N_BRANCH * D_MODEL),
            pl.BlockSpec((None, None, 6, D_MODEL), lambda i: (layer, row_of_tile(i), 0, 0)),
            _const_spec((4, D_MODEL)),
            _const_spec((BR_OFFS[-1], D_MODEL)),
            _const_spec((D_MODEL, D_MODEL)),
            _const_spec((D_MODEL, D_FF)),
            _const_spec((D_FF, D_MODEL)),
        ],
        out_specs=rows(D_MODEL),
        out_shape=jax.ShapeDtypeStruct((m, D_MODEL), f32),
        compiler_params=_params(1),
        name="post",
    )(x2d, ym.reshape(m, HW), ya.reshape(m, A_W), yc.reshape(m, HW), yr.reshape(m, HW), gt, mod4, lw['ng'],
      lw['w_branch'], lw['w_out'], lw['w_mlp1'], lw['w_mlp2'])


PACK_ROWS = 512


def _pack_kernel(wt_ref, o_ref):
    s = pl.program_id(1)
    gate_step = PK_MG // PACK_ROWS

    @pl.when(s != gate_step)
    def _plain():
        o_ref[...] = wt_ref[0].astype(bf16)

    @pl.when(s == gate_step)
    def _gates():
        row = lax.broadcasted_iota(jnp.int32, (LANES, 1), 0)
        gates = wt_ref[0, 0:LANES, :]
        o_ref[0:LANES, :] = jnp.where(row < N_GATES, gates, 0.0).astype(bf16)
        shifted = pltpu.roll(gates, LANES - N_HEADS, 0)
        o_ref[LANES:2 * LANES, :] = jnp.where(row < N_GATES - N_HEADS, shifted, 0.0).astype(bf16)
        o_ref[2 * LANES:PACK_ROWS, :] = wt_ref[0, N_GATES:N_GATES + PACK_ROWS - 2 * LANES, :].astype(bf16)


def _pack_w_in(w_in, b_in):
    gate_step = PK_MG // PACK_ROWS
    shift = 2 * LANES - N_GATES
    assert PK_MG % PACK_ROWS == 0 and PK_W % PACK_ROWS == 0 and PACK_ROWS >= 2 * LANES

    def src_row(s):
        return pl.multiple_of(jnp.where(s <= gate_step, PACK_ROWS * s, PACK_ROWS * s - shift), N_GATES)

    w_pk = pl.pallas_call(
        _pack_kernel,
        grid=(DEPTH, PK_W // PACK_ROWS),
        in_specs=[pl.BlockSpec((pl.Element(1), pl.Element(PACK_ROWS), pl.Element(D_MODEL)),
                               lambda l, s: (l, src_row(s), 0))],
        out_specs=pl.BlockSpec((None, PACK_ROWS, D_MODEL), lambda l, s: (l, s, 0)),
        out_shape=jax.ShapeDtypeStruct((DEPTH, PK_W, D_MODEL), bf16),
        compiler_params=_params(2),
        name="pack_w_in",
    )(jnp.swapaxes(w_in, 1, 2))
    g0 = PK_MG
    pad = lambda cols: jnp.zeros((DEPTH, cols), f32)
    b_pk = jnp.concatenate(
        [b_in[:, :g0 + N_GATES], pad(LANES - N_GATES), b_in[:, g0 + N_HEADS:g0 + N_GATES],
         pad(LANES - N_GATES + N_HEADS), b_in[:, g0 + N_GATES:]], axis=-1)
    return w_pk, b_pk[:, None, :]


def _stack_state(s):
    bsz = s.shape[0]
    s = s.reshape(bsz, 2, HW, DH)
    return jnp.pad(s, ((0, 0), (0, 0), (0, 0), (0, LANES - DH)))


def _layer(x2d, bsz, n, layer, mod4, lw, cache, row_of_tile):
    is_ctx = cache is None
    mq, mg, at, cv, rt, gt = _inproj(x2d, mod4, lw['ng'], lw['w_in'], lw['b_in'], layer, row_of_tile)

    if is_ctx:
        c0 = jnp.zeros((bsz, 2, HW, LANES), f32)
        n0 = jnp.zeros((bsz, 2, 1, HW), f32)
        m0 = jnp.zeros((bsz, 2, 1, LANES), f32)
        s0 = jnp.zeros((bsz, 2, HW, LANES), f32)
    else:
        c0 = _stack_state(cache['C'])
        n0 = cache['n'].reshape(bsz, 2, 1, HW)
        m0 = jnp.stack([jnp.pad(cache['m'][:, d], ((0, 0), (2 * N_HEADS * d, LANES - N_HEADS - 2 * N_HEADS * d)))
                        for d in range(2)], axis=1)[:, :, None, :]
        s0 = _stack_state(cache['S'])
    ym, c_fin, n_fin, m_fin = _mlstm(mq, mg, c0, n0, m0, bsz, n)
    yr, s_fin = _retention(rt, s0, lw['ret_dl'], lw['ret_dr'], bsz, n)

    at3 = at.reshape(bsz, n, A_W + 2 * A_KV_W)
    kcol = A_W // A_KV_W
    if is_ctx:
        dummy = jnp.zeros((SUBLANES, LANES), f32)
        ck_spec = pl.BlockSpec((None, n, A_KV_W), lambda b, i: (b, 0, kcol))
        cv_spec = pl.BlockSpec((None, n, A_KV_W), lambda b, i: (b, 0, kcol + 1))
        ya = _attention(at, at3, at3, ck_spec, cv_spec, dummy, dummy, lw['sink'], bsz, n, n, False)
    else:
        n_ctx = cache['k'].shape[2]
        ck_spec = pl.BlockSpec((None, None, n_ctx, A_KV_W), lambda b, i: (b, layer, 0, 0))
        cos, sin = _rope_tables(n)
        ya = _attention(at, cache['k'], cache['v'], ck_spec, ck_spec, cos, sin, lw['sink'], bsz, n, n_ctx, True)

    yc = _conv(cv, lw['conv_w'], lw['conv_b'], lw['conv_ln_g'], lw['conv_ln_b'], bsz, n)

    x2d = _post(x2d, ym, ya, yc, yr, gt, mod4, lw, layer, row_of_tile)

    ctx = None
    if is_ctx:
        ctx = (at3[:, :, A_W:A_W + A_KV_W].reshape(bsz, n, A_KV_HEADS, DH),
               at3[:, :, A_W + A_KV_W:].reshape(bsz, n, A_KV_HEADS, DH),
               c_fin.reshape(bsz, 2, N_HEADS, DH, DH),
               n_fin.reshape(bsz, 2, N_HEADS, DH),
               jnp.stack([m_fin[:, d, 0, 2 * N_HEADS * d:2 * N_HEADS * d + N_HEADS] for d in range(2)], axis=1),
               s_fin.reshape(bsz, 2, N_HEADS, DH, DH))
    return x2d, ctx


def kernel(x_prompt, x_sample, c, cache_k, cache_v, state_mlstm_C, state_mlstm_n, state_mlstm_m, state_ret,
           c_ctx, w_ada, b_ada, norm_g, w_in, b_in, w_branch, w_out, attn_sink, ret_decay_logit,
           conv_w, conv_b, conv_ln_g, conv_ln_b, w_mlp1, w_mlp2):
    bsz, seq, _ = x_prompt.shape
    dbsz, dseq, _ = x_sample.shape
    past = cache_k.shape[2]

    c_rows = jnp.zeros((SUBLANES, D_MODEL), f32).at[0].set(c_ctx).at[1:1 + dbsz].set(c)
    mod4 = _ada(c_rows, w_ada, b_ada).reshape(DEPTH, SUBLANES, 6, D_MODEL)

    w_pk, b_pk = _pack_w_in(w_in, b_in)
    layers = []
    for l in range(DEPTH):
        layers.append({
            'ng': norm_g[l], 'w_in': w_pk, 'b_in': b_pk,
            'w_branch': w_branch[l].astype(bf16), 'w_out': w_out[l].astype(bf16),
            'w_mlp1': w_mlp1[l].astype(bf16), 'w_mlp2': w_mlp2[l].astype(bf16),
            'sink': jnp.broadcast_to(attn_sink[l][:, None], (A_HEADS, LANES)),
            'ret_dl': jnp.repeat(ret_decay_logit[l], DH, axis=-1).reshape(2, 1, HW),
            'ret_dr': jnp.broadcast_to(jnp.repeat(ret_decay_logit[l], DH, axis=-1)[:, :, None], (2, HW, LANES)),
            'conv_w': conv_w[l], 'conv_b': conv_b[l][None, :],
            'conv_ln_g': conv_ln_g[l][None, :], 'conv_ln_b': conv_ln_b[l][None, :],
        })

    xp = x_prompt.reshape(bsz * seq, D_MODEL)
    ctxs = []
    for l in range(DEPTH):
        xp, ctx = _layer(xp, bsz, seq, l, mod4, layers[l], None, lambda i: 0)
        ctxs.append(ctx)

    xs = x_sample.reshape(dbsz * dseq, D_MODEL)
    tiles_per_seq = dseq // ROW_TILE
    ck = cache_k.reshape(dbsz, DEPTH, past, A_KV_W)
    cv = cache_v.reshape(dbsz, DEPTH, past, A_KV_W)
    for l in range(DEPTH):
        cache = {'k': ck, 'v': cv, 'C': state_mlstm_C[:, l], 'n': state_mlstm_n[:, l],
                 'm': state_mlstm_m[:, l], 'S': state_ret[:, l]}
        xs, _ = _layer(xs, dbsz, dseq, l, mod4, layers[l], cache, lambda i: 1 + i // tiles_per_seq)

    stack = lambda j: jnp.stack([ctxs[l][j] for l in range(DEPTH)], axis=1)
    return (xp.reshape(bsz, seq, D_MODEL), xs.reshape(dbsz, dseq, D_MODEL),
            stack(0), stack(1), stack(2), stack(3), stack(4), stack(5))
```

```python
import functools

import numpy as np
import jax
import jax.numpy as jnp
from jax import lax
from jax.experimental import pallas as pl
from jax.experimental.pallas import tpu as pltpu

f32 = jnp.float32
bf16 = jnp.bfloat16

D_MODEL = 1024
DEPTH = 2
GRID_W = 64
BLOCK = 128
N_HEADS = 4
DH = 64
HW = N_HEADS * DH
A_HEADS = 8
A_KV_HEADS = 2
A_GROUP = A_HEADS // A_KV_HEADS
A_W = A_HEADS * DH
A_KV_W = A_KV_HEADS * DH
WINDOW = 128
ROPE_BASE = 10000.0
CONV_K = 31
D_FF = 4 * D_MODEL
N_BRANCH = 4
EPS = 1e-6
NEG = -1e30
LOG2E = 1.4426950408889634
LANES = 128
SUBLANES = 8

N_GATES = 4 * N_HEADS
PK_MQ = 0
PK_MG = PK_MQ + 4 * HW
PK_AT = PK_MG + 2 * LANES
PK_CV = PK_AT + A_W + 2 * A_KV_W
PK_RT = PK_CV + 2 * HW
PK_GT = PK_RT + 4 * HW
PK_W = PK_GT + N_BRANCH * D_MODEL
BR_OFFS = (0, HW, HW + A_W, 2 * HW + A_W, 3 * HW + A_W)

VMEM_LIMIT = 56 * 1024 * 1024
ROW_TILE = 512


def _sigmoid(x):
    return 0.5 * (jnp.tanh(0.5 * x) + 1.0)


def _log_sigmoid(x):
    return jnp.minimum(x, 0.0) - jnp.log(1.0 + jnp.exp(-jnp.abs(x)))


def _dot(a, b):
    return jnp.dot(a, b, preferred_element_type=f32)


def _dot_nt(a, b):
    return lax.dot_general(a, b, (((1,), (1,)), ((), ())), preferred_element_type=f32)


def _dot2_r(t, x):
    x1 = x.astype(bf16)
    x2 = (x - x1.astype(f32)).astype(bf16)
    r = _dot(t, jnp.concatenate([x1, x2], axis=1))
    return r[:, 0:x.shape[1]] + r[:, x.shape[1]:]


def _dot2_l(x, t):
    x1 = x.astype(bf16)
    x2 = (x - x1.astype(f32)).astype(bf16)
    r = _dot(jnp.concatenate([x1, x2], axis=0), t)
    return r[0:x.shape[0]] + r[x.shape[0]:]


def _const_spec(shape):
    nd = len(shape)
    return pl.BlockSpec(shape, lambda *_: (0,) * nd, pipeline_mode=pl.Buffered(1))


def _params(n_axes):
    return pltpu.CompilerParams(dimension_semantics=("arbitrary",) * n_axes, vmem_limit_bytes=VMEM_LIMIT)


def _ada_kernel(c_ref, w_ref, b_ref, o_ref):
    c = c_ref[...]
    s = (c * _sigmoid(c)).astype(bf16)
    o_ref[...] = _dot(s, w_ref[...].astype(bf16)) + b_ref[...]


def _ada(c_rows, w_ada, b_ada):
    tn = 1536
    n_out = 6 * D_MODEL
    return pl.pallas_call(
        _ada_kernel,
        grid=(DEPTH, n_out // tn),
        in_specs=[
            pl.BlockSpec((SUBLANES, D_MODEL), lambda l, j: (0, 0)),
            pl.BlockSpec((None, D_MODEL, tn), lambda l, j: (l, 0, j)),
            pl.BlockSpec((None, 1, tn), lambda l, j: (l, 0, j)),
        ],
        out_specs=pl.BlockSpec((None, SUBLANES, tn), lambda l, j: (l, 0, j)),
        out_shape=jax.ShapeDtypeStruct((DEPTH, SUBLANES, n_out), f32),
        compiler_params=_params(2),
        name="ada",
    )(c_rows, w_ada, b_ada.reshape(DEPTH, 1, n_out))


def _inproj_kernel(x_ref, mod_ref, ng_ref, w_ref, b_ref, mq_ref, mg_ref, at_ref, cv_ref, rt_ref, gt_ref):
    x = x_ref[...]
    y = x * lax.rsqrt(jnp.mean(x * x, axis=-1, keepdims=True) + EPS) * ng_ref[0:1, :]
    h = (y * (1.0 + mod_ref[1:2, :]) + mod_ref[0:1, :]).astype(bf16)
    chunk = 512
    for ref, base in ((mq_ref, PK_MQ), (mg_ref, PK_MG), (at_ref, PK_AT), (cv_ref, PK_CV), (rt_ref, PK_RT),
                      (gt_ref, PK_GT)):
        total = ref.shape[-1]
        for c in range(0, total, chunk):
            w = min(chunk, total - c)
            r = _dot_nt(h, w_ref[base + c:base + c + w, :]) + b_ref[:, base + c:base + c + w]
            ref[:, c:c + w] = r.astype(ref.dtype)


def _inproj(x2d, mod4, ng, w_pk, b_pk, layer, row_of_tile):
    m = x2d.shape[0]
    tm = ROW_TILE
    widths = (4 * HW, 2 * LANES, A_W + 2 * A_KV_W, 2 * HW, 4 * HW, N_BRANCH * D_MODEL)
    dtypes = (bf16, f32, f32, bf16, bf16, bf16)
    return pl.pallas_call(
        _inproj_kernel,
        grid=(m // tm,),
        in_specs=[
            pl.BlockSpec((tm, D_MODEL), lambda i: (i, 0)),
            pl.BlockSpec((None, None, 6, D_MODEL), lambda i: (layer, row_of_tile(i), 0, 0)),
            _const_spec((4, D_MODEL)),
            pl.BlockSpec((None, PK_W, D_MODEL), lambda i: (layer, 0, 0), pipeline_mode=pl.Buffered(1)),
            pl.BlockSpec((None, 1, PK_W), lambda i: (layer, 0, 0), pipeline_mode=pl.Buffered(1)),
        ],
        out_specs=[pl.BlockSpec((tm, w), lambda i: (i, 0)) for w in widths],
        out_shape=[jax.ShapeDtypeStruct((m, w), dt) for w, dt in zip(widths, dtypes)],
        compiler_params=_params(1),
        name="inproj",
    )(x2d, mod4, ng, w_pk, b_pk)


def _head_lane_masks():
    lane_head = lax.broadcasted_iota(jnp.int32, (1, HW), 1) // DH
    return [lane_head == h for h in range(N_HEADS)]


def _run_interleaved(gens):
    results = [None] * len(gens)
    active = list(range(len(gens)))
    while active:
        for idx in list(active):
            try:
                next(gens[idx])
            except StopIteration as stop:
                results[idx] = stop.value
                active.remove(idx)
    return results


PAIRS = N_HEADS // 2


def _pair_diag_mask():
    r = (lax.broadcasted_iota(jnp.int32, (HW, LANES), 0) // DH) % 2
    c = lax.broadcasted_iota(jnp.int32, (HW, LANES), 1) // DH
    return r == c


def _expand_state(x):
    return jnp.where(_pair_diag_mask(), x + pltpu.roll(x, DH, 1), 0.0)


def _compact_state(c):
    return (c + pltpu.roll(c, DH, 1))[:, 0:DH]


def _pair_qk(q, kt):
    zeros = jnp.zeros((DH, BLOCK), kt.dtype)
    outs = []
    for j in range(PAIRS):
        rows = kt[LANES * j:LANES * (j + 1)]
        blk = jnp.concatenate([jnp.concatenate([rows[0:DH], zeros], axis=0),
                               jnp.concatenate([zeros, rows[DH:2 * DH]], axis=0)], axis=1)
        outs.append(_dot(q[:, LANES * j:LANES * (j + 1)], blk))
    return jnp.concatenate(outs, axis=1)


def _pair_pv(p, v):
    low = lax.broadcasted_iota(jnp.int32, (1, LANES), 1) < DH
    outs = []
    for j in range(PAIRS):
        vp = v[:, LANES * j:LANES * (j + 1)]
        zero = jnp.zeros_like(vp)
        blk = jnp.concatenate([jnp.where(low, vp, zero), jnp.where(low, zero, vp)], axis=0)
        outs.append(_dot(p[:, 2 * BLOCK * j:2 * BLOCK * (j + 1)], blk))
    return jnp.concatenate(outs, axis=1)


def _pair_state_read(q, c):
    return jnp.concatenate([_dot(q[:, LANES * j:LANES * (j + 1)], c[LANES * j:LANES * (j + 1)])
                            for j in range(PAIRS)], axis=1)


def _pair_state_update(a_t, v):
    upd = jnp.concatenate([_dot(a_t[LANES * j:LANES * (j + 1)], v[:, LANES * j:LANES * (j + 1)])
                           for j in range(PAIRS)], axis=0)
    return jnp.where(_pair_diag_mask(), upd, 0.0)


def _mlstm_kernel(x_ref, g_ref, c0_ref, n0_ref, m0_ref, y_ref, cn_ref, nn_ref, mn_ref,
                  h_s, c_s, n_s, m_s, *, nc, nb):
    wide = N_HEADS * BLOCK
    row = lax.broadcasted_iota(jnp.int32, (BLOCK, BLOCK), 0)
    col = lax.broadcasted_iota(jnp.int32, (BLOCK, BLOCK), 1)
    tsum = (jnp.where(row >= col, 1.0, 0.0).astype(bf16), jnp.where(row <= col, 1.0, 0.0).astype(bf16))
    roww = lax.broadcasted_iota(jnp.int32, (BLOCK, wide), 0)
    colw = lax.broadcasted_iota(jnp.int32, (BLOCK, wide), 1) & (BLOCK - 1)
    tri4 = (roww >= colw, roww <= colw)
    masks = _head_lane_masks()
    lane = lax.broadcasted_iota(jnp.int32, (1, LANES), 1)
    row8 = lax.broadcasted_iota(jnp.int32, (SUBLANES, BLOCK), 0)

    def onehot(shape, row_expr, col_expr):
        r = lax.broadcasted_iota(jnp.int32, shape, 0)
        c = lax.broadcasted_iota(jnp.int32, shape, 1)
        return jnp.where(row_expr(r) == col_expr(c), 1.0, 0.0).astype(bf16)

    g0s = (0, 2 * N_HEADS)
    valid = tuple((lane >= g0) & (lane < g0 + N_HEADS) for g0 in g0s)
    hexp = tuple(onehot((LANES, HW), lambda r: r, lambda c, g0=g0: c // DH + g0) for g0 in g0s)
    hselg = tuple(onehot((HW, LANES), lambda r, g0=g0: r // DH + g0, lambda c: c) for g0 in g0s)

    chains = [(bb, d) for bb in range(nb) for d in range(2)]
    for bb, d in chains:
        c_s[bb, d] = _expand_state(c0_ref[bb, d])
        n_s[bb, d] = n0_ref[bb, d]
        m_s[bb, d] = m0_ref[bb, d]

    def load(bb, d, c):
        r0 = pl.multiple_of(c * BLOCK, BLOCK)
        return (r0, x_ref[bb, pl.ds(r0, BLOCK), 0:3 * HW], g_ref[bb, pl.ds(r0, BLOCK), :],
                c_s[bb, d], n_s[bb, d], m_s[bb, d])

    def compute(d, blk, g, cmat, nrow, m_row):
        g0 = g0s[d]
        q = blk[:, 0:HW]
        k = blk[:, HW:2 * HW] * (DH ** -0.5)
        v = blk[:, 2 * HW:3 * HW]
        ga = g[:, 0:LANES]
        gb = g[:, LANES:2 * LANES]
        bc = _dot2_r(tsum[d], _log_sigmoid(gb))
        yield
        beta = ga - bc
        beta_t = beta.T
        yield
        cmx = beta
        sh = 1
        while sh < BLOCK:
            if d == 0:
                cmx = jnp.maximum(cmx, jnp.where(row >= sh, pltpu.roll(cmx, sh, 0), NEG))
            else:
                cmx = jnp.maximum(cmx, jnp.where(row < BLOCK - sh, pltpu.roll(cmx, BLOCK - sh, 0), NEG))
            sh *= 2
        mx = jnp.maximum(m_row, cmx)
        alpha = jnp.where(valid[d], -mx, 0.0)
        w_int = jnp.where(valid[d], jnp.exp(m_row - mx), 0.0)
        e_nb = jnp.exp(alpha - bc)
        yield
        z = jnp.concatenate([jnp.broadcast_to(alpha[:, g0 + h:g0 + h + 1], (BLOCK, BLOCK)) for h in range(N_HEADS)],
                            axis=1)
        beta_w = jnp.concatenate([beta_t[g0 + h:g0 + h + 1, :] for h in range(N_HEADS)], axis=1)
        ktf = k.astype(f32).T
        s_wide = _pair_qk(q, ktf.astype(bf16))
        yield
        p = s_wide * jnp.exp(jnp.where(tri4[d], z + beta_w, NEG))
        num = _pair_pv(p.astype(bf16), v)
        yield
        den = jnp.zeros((BLOCK, LANES), f32)
        for h in range(N_HEADS):
            den = jnp.where(lane == g0 + h, jnp.sum(p[:, BLOCK * h:BLOCK * (h + 1)], axis=1, keepdims=True), den)
        qc = _pair_state_read(q, cmat.astype(bf16))
        qn = _dot(q * nrow.astype(bf16), hselg[d])
        yield
        den = den + w_int * qn
        inv = jnp.where(valid[d], 1.0 / jnp.maximum(jnp.abs(den), e_nb), 0.0)
        fac = _dot(jnp.concatenate([w_int, inv], axis=0).astype(bf16), hexp[d])
        yield
        hout = (num + fac[0:BLOCK] * qc) * fac[BLOCK:2 * BLOCK]
        last = BLOCK - 1 if d == 0 else 0
        a_last = alpha[last:last + 1, :]
        m_new = jnp.where(valid[d], bc[last:last + 1, :] - a_last, 0.0)
        dec = jnp.exp(m_row + a_last)
        wr = jnp.zeros((SUBLANES, BLOCK), f32)
        kws = []
        for h in range(N_HEADS):
            w_row = jnp.exp(beta_t[g0 + h:g0 + h + 1, :] + a_last[:, g0 + h:g0 + h + 1])
            kws.append(ktf[DH * h:DH * (h + 1), :] * w_row)
            wr = jnp.where(row8 == h, w_row, wr)
        upd = _pair_state_update(jnp.concatenate(kws, axis=0).astype(bf16), v)
        wk = _dot(wr.astype(bf16), k)
        yield
        nnew = jnp.zeros((1, HW), f32)
        cnew = []
        for h in range(N_HEADS):
            rs = slice(DH * h, DH * (h + 1))
            dh = dec[:, g0 + h:g0 + h + 1]
            cnew.append(dh * cmat[rs, :] + upd[rs, :])
            nnew = jnp.where(masks[h], dh * nrow + wk[h:h + 1, :], nnew)
        return hout, jnp.concatenate(cnew, axis=0), nnew, m_new

    def step(i, second):
        loaded = [load(bb, d, i if d == 0 else nc - 1 - i) for bb, d in chains]
        if second:
            other = [h_s[bb, pl.ds(l[0], BLOCK), :] for (bb, d), l in zip(chains, loaded)]
            ogate = [x_ref[bb, pl.ds(l[0], BLOCK), 3 * HW:4 * HW] for (bb, d), l in zip(chains, loaded)]
        outs = _run_interleaved([compute(d, *l[1:]) for (bb, d), l in zip(chains, loaded)])
        for idx, ((bb, d), l) in enumerate(zip(chains, loaded)):
            hout, cnew, nnew, m_new = outs[idx]
            if second:
                y = _sigmoid(ogate[idx].astype(f32)) * (other[idx] + hout)
                y_ref[bb, pl.ds(l[0], BLOCK), :] = y.astype(y_ref.dtype)
            else:
                h_s[bb, pl.ds(l[0], BLOCK), :] = hout
            c_s[bb, d] = cnew
            n_s[bb, d] = nnew
            m_s[bb, d] = m_new

    def first_half(i, carry):
        step(i, False)
        return carry

    def second_half(i, carry):
        step(i, True)
        return carry

    lax.fori_loop(0, nc // 2, first_half, 0)
    lax.fori_loop(nc // 2, nc, second_half, 0)

    for bb, d in chains:
        cn_ref[bb, d] = _compact_state(c_s[bb, d])
        nn_ref[bb, d] = n_s[bb, d]
        mn_ref[bb, d] = m_s[bb, d]


SCAN_SEQS = 2


def _mlstm(mq, mg, c0, n0, m0, bsz, n):
    nc = n // BLOCK
    nb = SCAN_SEQS
    assert nc % 2 == 0 and bsz % nb == 0
    spec = lambda shape: pl.BlockSpec((nb,) + shape, lambda b: (b,) + (0,) * len(shape))
    seq_in = (lambda shape: pl.BlockSpec((nb,) + shape, lambda b: (b,) + (0,) * len(shape),
                                         pipeline_mode=pl.Buffered(1))) if bsz == nb else spec
    return pl.pallas_call(
        functools.partial(_mlstm_kernel, nc=nc, nb=nb),
        grid=(bsz // nb,),
        in_specs=[
            seq_in((n, 4 * HW)),
            seq_in((n, 2 * LANES)),
            spec((2, HW, LANES)),
            spec((2, 1, HW)),
            spec((2, 1, LANES)),
        ],
        out_specs=[
            spec((n, HW)),
            spec((2, HW, DH)),
            spec((2, 1, HW)),
            spec((2, 1, LANES)),
        ],
        out_shape=[
            jax.ShapeDtypeStruct((bsz, n, HW), bf16),
            jax.ShapeDtypeStruct((bsz, 2, HW, DH), f32),
            jax.ShapeDtypeStruct((bsz, 2, 1, HW), f32),
            jax.ShapeDtypeStruct((bsz, 2, 1, LANES), f32),
        ],
        scratch_shapes=[
            pltpu.VMEM((nb, n, HW), f32),
            pltpu.VMEM((nb, 2, HW, LANES), f32),
            pltpu.VMEM((nb, 2, 1, HW), f32),
            pltpu.VMEM((nb, 2, 1, LANES), f32),
        ],
        compiler_params=_params(1),
        name="mlstm",
    )(mq.reshape(bsz, n, 4 * HW), mg.reshape(bsz, n, 2 * LANES), c0, n0, m0)


def _ret_kernel(x_ref, s0_ref, dl_ref, dr_ref, y_ref, sn_ref, o_s, s_s, qd_s, kd_s, cd_s, dm_s, *, nc, nb):
    avg = jnp.where(lax.broadcasted_iota(jnp.int32, (HW, HW), 0) // DH ==
                    lax.broadcasted_iota(jnp.int32, (HW, HW), 1) // DH, 1.0 / DH, 0.0).astype(bf16)

    @pl.when(pl.program_id(0) == 0)
    def _decay_tables():
        rowf = lax.broadcasted_iota(jnp.int32, (BLOCK, HW), 0).astype(f32)
        rel = (lax.broadcasted_iota(jnp.int32, (BLOCK, BLOCK), 0) -
               lax.broadcasted_iota(jnp.int32, (BLOCK, BLOCK), 1)).astype(f32)
        for d in range(2):
            lg = _log_sigmoid(dl_ref[d])
            if d == 0:
                qd_s[d] = jnp.exp((rowf + 1.0) * lg)
                kd_s[d] = jnp.exp((BLOCK - 1.0 - rowf) * lg)
                reld = rel
            else:
                qd_s[d] = jnp.exp((BLOCK - rowf) * lg)
                kd_s[d] = jnp.exp(rowf * lg)
                reld = -rel
            cd_s[d] = jnp.exp(float(BLOCK) * _log_sigmoid(dr_ref[d]))
            for h in range(N_HEADS):
                lgh = lg[:, DH * h:DH * h + 1]
                dm_s[d, :, BLOCK * h:BLOCK * (h + 1)] = jnp.where(reld >= 0.0,
                                                                  jnp.exp(jnp.maximum(reld, 0.0) * lgh), 0.0)

    chains = [(bb, d) for bb in range(nb) for d in range(2)]
    for bb, d in chains:
        s_s[bb, d] = _expand_state(s0_ref[bb, d])

    def compute(d, blk, sm, other, gate):
        q = blk[:, 0:HW]
        k = blk[:, HW:2 * HW] * (DH ** -0.5)
        v = blk[:, 2 * HW:3 * HW]
        kf = k.astype(f32)
        p = _pair_qk(q, kf.T.astype(bf16)) * dm_s[d]
        yield
        o = _pair_pv(p.astype(bf16), v) + qd_s[d] * _pair_state_read(q, sm.astype(bf16))
        yield
        kdt = (kf * kd_s[d]).T.astype(bf16)
        snew = cd_s[d] * sm + _pair_state_update(kdt, v)
        yield
        if other is None:
            return o, snew
        o = o + other
        gate = gate.astype(f32)
        dev = o - _dot2_l(o, avg)
        yield
        var = _dot((dev * dev).astype(bf16), avg)
        yield
        return gate * _sigmoid(gate) * (dev * lax.rsqrt(var + EPS)), snew

    def step(i, second):
        r0s = [pl.multiple_of((i if d == 0 else nc - 1 - i) * BLOCK, BLOCK) for bb, d in chains]
        loaded = [(x_ref[bb, pl.ds(r0, BLOCK), 0:3 * HW], s_s[bb, d]) for (bb, d), r0 in zip(chains, r0s)]
        if second:
            other = [o_s[bb, pl.ds(r0, BLOCK), :] for (bb, d), r0 in zip(chains, r0s)]
            gates = [x_ref[bb, pl.ds(r0, BLOCK), 3 * HW:4 * HW] for (bb, d), r0 in zip(chains, r0s)]
        else:
            other = gates = [None] * len(chains)
        outs = _run_interleaved([compute(d, *l, other[idx], gates[idx])
                                 for idx, ((bb, d), l) in enumerate(zip(chains, loaded))])
        for idx, ((bb, d), r0) in enumerate(zip(chains, r0s)):
            o, snew = outs[idx]
            if second:
                y_ref[bb, pl.ds(r0, BLOCK), :] = o.astype(y_ref.dtype)
            else:
                o_s[bb, pl.ds(r0, BLOCK), :] = o
            s_s[bb, d] = snew

    def first_half(i, carry):
        step(i, False)
        return carry

    def second_half(i, carry):
        step(i, True)
        return carry

    lax.fori_loop(0, nc // 2, first_half, 0)
    lax.fori_loop(nc // 2, nc, second_half, 0)

    for bb, d in chains:
        sn_ref[bb, d] = _compact_state(s_s[bb, d])


def _retention(rt, s0, dl, dr, bsz, n):
    nc = n // BLOCK
    nb = SCAN_SEQS
    assert nc % 2 == 0 and bsz % nb == 0
    spec = lambda shape: pl.BlockSpec((nb,) + shape, lambda b: (b,) + (0,) * len(shape))
    seq_in = (lambda shape: pl.BlockSpec((nb,) + shape, lambda b: (b,) + (0,) * len(shape),
                                         pipeline_mode=pl.Buffered(1))) if bsz == nb else spec
    return pl.pallas_call(
        functools.partial(_ret_kernel, nc=nc, nb=nb),
        grid=(bsz // nb,),
        in_specs=[
            seq_in((n, 4 * HW)),
            spec((2, HW, LANES)),
            _const_spec((2, 1, HW)),
            _const_spec((2, HW, LANES)),
        ],
        out_specs=[
            spec((n, HW)),
            spec((2, HW, DH)),
        ],
        out_shape=[
            jax.ShapeDtypeStruct((bsz, n, HW), bf16),
            jax.ShapeDtypeStruct((bsz, 2, HW, DH), f32),
        ],
        scratch_shapes=[
            pltpu.VMEM((nb, n, HW), f32),
            pltpu.VMEM((nb, 2, HW, LANES), f32),
            pltpu.VMEM((2, BLOCK, HW), f32),
            pltpu.VMEM((2, BLOCK, HW), f32),
            pltpu.VMEM((2, HW, LANES), f32),
            pltpu.VMEM((2, BLOCK, N_HEADS * BLOCK), f32),
        ],
        compiler_params=_params(1),
        name="retention",
    )(rt.reshape(bsz, n, 4 * HW), s0, dl, dr)


def _attn_kernel(q_ref, k_ref, v_ref, ck_ref, cv_ref, cos_ref, sin_ref, sk_ref, o_ref, kt_s, vt_s,
                 *, n_ctx, n, band, qb, qps):
    i = pl.program_id(1)
    lane = lax.broadcasted_iota(jnp.int32, (1, LANES), 1)
    lo = lane < DH
    first = (lane % (DH // 2)) < (DH // 4)
    gmasks = _head_lane_masks()
    n_cblk = n_ctx // BLOCK
    wide = A_GROUP * qb

    def tile4(x):
        xr = pltpu.roll(x, DH, 1)
        a2 = jnp.where(lo, x, xr).astype(bf16)
        b2 = jnp.where(lo, xr, x).astype(bf16)
        return jnp.concatenate([a2, a2], axis=1), jnp.concatenate([b2, b2], axis=1)

    def rope(x, cos, sin):
        sw = jnp.where(first, pltpu.roll(x, LANES - DH // 4, 1), pltpu.roll(x, DH // 4, 1))
        return x * cos + sw * sin

    def put_block(blk, r0, kk, vv):
        k0, k1 = tile4(kk)
        kt_s[0, pl.ds(r0, BLOCK), :] = k0
        kt_s[1, pl.ds(r0, BLOCK), :] = k1
        vt = vv.T
        for j in range(A_KV_HEADS):
            vj = vt[DH * j:DH * (j + 1)]
            vt_s[j, blk] = jnp.concatenate([vj, vj], axis=0).astype(bf16)

    @pl.when(i == 0)
    def _prepare_keys():
        for c in range(n_cblk):
            put_block(c, c * BLOCK, ck_ref[c * BLOCK:(c + 1) * BLOCK, :], cv_ref[c * BLOCK:(c + 1) * BLOCK, :])
        if band:
            nb = n // BLOCK
            for blk in (n_cblk, n_cblk + 1 + nb):
                for j in range(A_KV_HEADS):
                    kt_s[j, blk * BLOCK:(blk + 1) * BLOCK, :] = jnp.zeros((BLOCK, HW), bf16)
                    vt_s[j, blk] = jnp.zeros((BLOCK, BLOCK), bf16)

            def body(c, carry):
                r0 = pl.multiple_of(c * BLOCK, BLOCK)
                kk = rope(k_ref[pl.ds(r0, BLOCK), :], cos_ref[pl.ds(r0, BLOCK), :], sin_ref[pl.ds(r0, BLOCK), :])
                put_block(n_cblk + 1 + c, pl.multiple_of(n_ctx + BLOCK + r0, BLOCK), kk, v_ref[pl.ds(r0, BLOCK), :])
                return carry

            lax.fori_loop(0, nb, body, 0)

    row_lo = lax.broadcasted_iota(jnp.int32, (2 * DH, qb), 0) < DH

    def query_block(u):
        q = q_ref[qb * u:qb * (u + 1), :]
        if not band:
            return q * (DH ** -0.5 * LOG2E), None, None
        iq = i * qps + u
        q0 = pl.multiple_of(iq * BLOCK, BLOCK)
        cos = cos_ref[pl.ds(q0, BLOCK), :]
        sin = sin_ref[pl.ds(q0, BLOCK), :]
        q = jnp.concatenate([rope(q[:, LANES * t:LANES * (t + 1)], cos, sin) for t in range(A_W // LANES)], axis=1)
        kj = lax.broadcasted_iota(jnp.int32, (3 * BLOCK, BLOCK), 0)
        qi = lax.broadcasted_iota(jnp.int32, (3 * BLOCK, BLOCK), 1)
        tok = kj + (iq - 1) * BLOCK
        ok = (kj >= qi) & (kj <= qi + 2 * WINDOW) & (tok >= 0) & (tok < n)
        bias = jnp.where(ok, 0.0, NEG)
        return (q * (DH ** -0.5 * LOG2E), jnp.concatenate([bias] * A_GROUP, axis=1),
                (pl.multiple_of(n_ctx + iq * BLOCK, BLOCK), n_cblk + iq))

    def kv_head(j, q, bias4, pos):
        q256 = q[:, HW * j:HW * (j + 1)]
        qs = jnp.concatenate([jnp.where(gmasks[g], q256, 0.0) for g in range(A_GROUP)], axis=0).astype(bf16)
        sink = jnp.concatenate(
            [jnp.broadcast_to(sk_ref[A_GROUP * j + g:A_GROUP * j + g + 1, 0:1], (1, qb)) for g in range(A_GROUP)],
            axis=1) * LOG2E
        s1 = _dot_nt(kt_s[j, 0:n_ctx, :], qs)
        yield
        m = jnp.maximum(jnp.max(s1, axis=0, keepdims=True), sink)
        if band:
            s2 = _dot_nt(kt_s[j, pl.ds(pos[0], 3 * BLOCK), :], qs) + bias4
            vband = vt_s[j, pl.ds(pos[1], 3)]
            yield
            m = jnp.maximum(m, jnp.max(s2, axis=0, keepdims=True))
        p1 = jnp.exp2(s1 - m)
        den = jnp.sum(p1, axis=0, keepdims=True) + jnp.exp2(sink - m)
        vctx = jnp.concatenate([vt_s[j, c] for c in range(n_cblk)], axis=1)
        acc = _dot(vctx, p1.astype(bf16))
        yield
        if band:
            p2 = jnp.exp2(s2 - m)
            den = den + jnp.sum(p2, axis=0, keepdims=True)
            acc = acc + _dot(jnp.concatenate([vband[t] for t in range(3)], axis=1), p2.astype(bf16))
            yield
        acc = acc * (1.0 / den)
        pairs = []
        for g in range(0, A_GROUP, 2):
            pair = jnp.where(row_lo, acc[:, qb * g:qb * (g + 1)], acc[:, qb * (g + 1):qb * (g + 2)])
            pairs.append(pair.T)
        return pairs

    blocks = [query_block(u) for u in range(qps)]
    outs = _run_interleaved([kv_head(j, *blocks[u]) for u in range(qps) for j in range(A_KV_HEADS)])
    for u in range(qps):
        tiles = [t for pairs in outs[A_KV_HEADS * u:A_KV_HEADS * (u + 1)] for t in pairs]
        o_ref[qb * u:qb * (u + 1), :] = jnp.concatenate(tiles, axis=1).astype(o_ref.dtype)


def _attention(at, ck, cv, ck_spec, cv_spec, cos, sin, sk, bsz, n, n_ctx, band):
    qb = BLOCK if band else min(n, 2 * BLOCK)
    qps = 2 if band else 1
    nq = n // (qb * qps)
    s_tot = n_ctx + (n + 2 * BLOCK if band else 0)
    at3 = at.reshape(bsz, n, A_W + 2 * A_KV_W)
    kcol = A_W // A_KV_W
    return pl.pallas_call(
        functools.partial(_attn_kernel, n_ctx=n_ctx, n=n, band=band, qb=qb, qps=qps),
        grid=(bsz, nq),
        in_specs=[
            pl.BlockSpec((None, qb * qps, A_W), lambda b, i: (b, i, 0)),
            pl.BlockSpec((None, n, A_KV_W), lambda b, i: (b, 0, kcol)),
            pl.BlockSpec((None, n, A_KV_W), lambda b, i: (b, 0, kcol + 1)),
            ck_spec,
            cv_spec,
            _const_spec(cos.shape),
            _const_spec(sin.shape),
            _const_spec((A_HEADS, LANES)),
        ],
        out_specs=pl.BlockSpec((None, qb * qps, A_W), lambda b, i: (b, i, 0)),
        out_shape=jax.ShapeDtypeStruct((bsz, n, A_W), bf16),
        scratch_shapes=[
            pltpu.VMEM((A_KV_HEADS, s_tot, HW), bf16),
            pltpu.VMEM((A_KV_HEADS, s_tot // BLOCK, BLOCK, BLOCK), bf16),
        ],
        compiler_params=_params(2),
        name="attention",
    )(at3, at3, at3, ck, cv, cos, sin, sk)


def _rope_tables(n):
    tok = np.arange(n)
    pos = np.stack([tok // GRID_W, tok % GRID_W], axis=1).astype(np.float32)
    quarter = DH // 4
    freqs = np.power(np.float32(ROPE_BASE), -np.arange(quarter, dtype=np.float32) / np.float32(quarter))
    lane = np.arange(LANES) % DH
    axis = lane // (DH // 2)
    fidx = lane % quarter
    sign = np.where((lane % (DH // 2)) < quarter, -1.0, 1.0)
    ang = (pos[:, axis] * freqs.astype(np.float32)[fidx][None, :]).astype(np.float32).astype(np.float64)
    return jnp.asarray(np.cos(ang), f32), jnp.asarray(np.sin(ang) * sign[None, :], f32)


CONV_PAD = 16


def _conv_kernel(u_ref, w_ref, b_ref, lg_ref, lb_ref, y_ref, z_s, *, n):
    nc = n // BLOCK
    z_s[0:CONV_PAD, :] = jnp.zeros((CONV_PAD, HW), f32)
    z_s[CONV_PAD + n:2 * CONV_PAD + n, :] = jnp.zeros((CONV_PAD, HW), f32)

    def glu(c, carry):
        r0 = pl.multiple_of(c * BLOCK, BLOCK)
        blk = u_ref[pl.ds(r0, BLOCK), :].astype(f32)
        z_s[pl.ds(pl.multiple_of(r0 + CONV_PAD, SUBLANES), BLOCK), :] = blk[:, 0:HW] * _sigmoid(blk[:, HW:2 * HW])
        return carry

    lax.fori_loop(0, nc, glu, 0)

    first = CONV_PAD - CONV_K // 2
    span = BLOCK + 2 * CONV_PAD

    def tile(c, carry):
        r0 = pl.multiple_of(c * BLOCK, BLOCK)
        win = z_s[pl.ds(r0, span), :]
        acc = jnp.zeros((BLOCK, HW), f32)
        for r in range(SUBLANES):
            taps = [kk for kk in range(CONV_K) if (first + kk) % SUBLANES == r]
            if not taps:
                continue
            shifted = win if r == 0 else pltpu.roll(win, span - r, 0)
            for kk in taps:
                a = (first + kk) // SUBLANES * SUBLANES
                acc = acc + shifted[a:a + BLOCK, :] * w_ref[kk:kk + 1, :]
        zc = acc + b_ref[...]
        mu = jnp.mean(zc, axis=-1, keepdims=True)
        dev = zc - mu
        var = jnp.mean(dev * dev, axis=-1, keepdims=True)
        t = dev * lax.rsqrt(var + EPS) * lg_ref[...] + lb_ref[...]
        y_ref[pl.ds(r0, BLOCK), :] = (t * _sigmoid(t)).astype(y_ref.dtype)
        return carry

    lax.fori_loop(0, nc, tile, 0)


def _conv(cu, w, b, lg, lb, bsz, n):
    return pl.pallas_call(
        functools.partial(_conv_kernel, n=n),
        grid=(bsz,),
        in_specs=[
            pl.BlockSpec((None, n, 2 * HW), lambda bb: (bb, 0, 0)),
            _const_spec((CONV_K, HW)),
            _const_spec((1, HW)),
            _const_spec((1, HW)),
            _const_spec((1, HW)),
        ],
        out_specs=pl.BlockSpec((None, n, HW), lambda bb: (bb, 0, 0)),
        out_shape=jax.ShapeDtypeStruct((bsz, n, HW), bf16),
        scratch_shapes=[pltpu.VMEM((n + 2 * CONV_PAD, HW), f32)],
        compiler_params=_params(1),
        name="conv",
    )(cu.reshape(bsz, n, 2 * HW), w, b, lg, lb)


def _post_kernel(x_ref, ym_ref, ya_ref, yc_ref, yr_ref, gt_ref, mod_ref, ng_ref, wbr_ref, wout_ref, w1_ref, w2_ref,
                 o_ref):
    def rms(a):
        return a * lax.rsqrt(jnp.mean(a * a, axis=-1, keepdims=True) + EPS)

    acc = None
    for i, y_ref in enumerate((ym_ref, ya_ref, yc_ref, yr_ref)):
        br = _dot(y_ref[...], wbr_ref[BR_OFFS[i]:BR_OFFS[i + 1], :]).astype(bf16)
        term = jnp.tanh(gt_ref[:, D_MODEL * i:D_MODEL * (i + 1)] * 0.5) * br + br
        acc = term if acc is None else acc + term
    mix = _dot(acc * 0.5, wout_ref[...])
    x = x_ref[...] + mod_ref[2:3, :] * (rms(mix) * ng_ref[1:2, :])
    h = (rms(x) * ng_ref[2:3, :] * (1.0 + mod_ref[4:5, :]) + mod_ref[3:4, :]).astype(bf16)
    chunk = 512
    f = jnp.zeros(x.shape, f32)
    for c in range(0, D_FF, chunk):
        a = jnp.maximum(_dot(h, w1_ref[:, c:c + chunk]), 0.0)
        f = f + _dot((a * a).astype(bf16), w2_ref[c:c + chunk, :])
    o_ref[...] = x + mod_ref[5:6, :] * (rms(f) * ng_ref[3:4, :])


def _post(x2d, ym, ya, yc, yr, gt, mod4, lw, layer, row_of_tile):
    m = x2d.shape[0]
    tm = ROW_TILE
    rows = lambda w: pl.BlockSpec((tm, w), lambda i: (i, 0))
    return pl.pallas_call(
        _post_kernel,
        grid=(m // tm,),
        in_specs=[
            rows(D_MODEL), rows(HW), rows(A_W), rows(HW), rows(HW), rows(N_BRANCH * D_MODEL),
            pl.BlockSpec((None, None, 6, D_MODEL), lambda i: (layer, row_of_tile(i), 0, 0)),
            _const_spec((4, D_MODEL)),
            _const_spec((BR_OFFS[-1], D_MODEL)),
            _const_spec((D_MODEL, D_MODEL)),
            _const_spec((D_MODEL, D_FF)),
            _const_spec((D_FF, D_MODEL)),
        ],
        out_specs=rows(D_MODEL),
        out_shape=jax.ShapeDtypeStruct((m, D_MODEL), f32),
        compiler_params=_params(1),
        name="post",
    )(x2d, ym.reshape(m, HW), ya.reshape(m, A_W), yc.reshape(m, HW), yr.reshape(m, HW), gt, mod4, lw['ng'],
      lw['w_branch'], lw['w_out'], lw['w_mlp1'], lw['w_mlp2'])


PACK_ROWS = 512


def _pack_kernel(wt_ref, o_ref):
    s = pl.program_id(1)
    gate_step = PK_MG // PACK_ROWS

    @pl.when(s != gate_step)
    def _plain():
        o_ref[...] = wt_ref[0].astype(bf16)

    @pl.when(s == gate_step)
    def _gates():
        row = lax.broadcasted_iota(jnp.int32, (LANES, 1), 0)
        gates = wt_ref[0, 0:LANES, :]
        o_ref[0:LANES, :] = jnp.where(row < N_GATES, gates, 0.0).astype(bf16)
        shifted = pltpu.roll(gates, LANES - N_HEADS, 0)
        o_ref[LANES:2 * LANES, :] = jnp.where(row < N_GATES - N_HEADS, shifted, 0.0).astype(bf16)
        o_ref[2 * LANES:PACK_ROWS, :] = wt_ref[0, N_GATES:N_GATES + PACK_ROWS - 2 * LANES, :].astype(bf16)


def _pack_w_in(w_in, b_in):
    gate_step = PK_MG // PACK_ROWS
    shift = 2 * LANES - N_GATES
    assert PK_MG % PACK_ROWS == 0 and PK_W % PACK_ROWS == 0 and PACK_ROWS >= 2 * LANES

    def src_row(s):
        return pl.multiple_of(jnp.where(s <= gate_step, PACK_ROWS * s, PACK_ROWS * s - shift), N_GATES)

    w_pk = pl.pallas_call(
        _pack_kernel,
        grid=(DEPTH, PK_W // PACK_ROWS),
        in_specs=[pl.BlockSpec((pl.Element(1), pl.Element(PACK_ROWS), pl.Element(D_MODEL)),
                               lambda l, s: (l, src_row(s), 0))],
        out_specs=pl.BlockSpec((None, PACK_ROWS, D_MODEL), lambda l, s: (l, s, 0)),
        out_shape=jax.ShapeDtypeStruct((DEPTH, PK_W, D_MODEL), bf16),
        compiler_params=_params(2),
        name="pack_w_in",
    )(jnp.swapaxes(w_in, 1, 2))
    g0 = PK_MG
    pad = lambda cols: jnp.zeros((DEPTH, cols), f32)
    b_pk = jnp.concatenate(
        [b_in[:, :g0 + N_GATES], pad(LANES - N_GATES), b_in[:, g0 + N_HEADS:g0 + N_GATES],
         pad(LANES - N_GATES + N_HEADS), b_in[:, g0 + N_GATES:]], axis=-1)
    return w_pk, b_pk[:, None, :]


def _stack_state(s):
    bsz = s.shape[0]
    s = s.reshape(bsz, 2, HW, DH)
    return jnp.pad(s, ((0, 0), (0, 0), (0, 0), (0, LANES - DH)))


def _layer(x2d, bsz, n, layer, mod4, lw, cache, row_of_tile):
    is_ctx = cache is None
    mq, mg, at, cv, rt, gt = _inproj(x2d, mod4, lw['ng'], lw['w_in'], lw['b_in'], layer, row_of_tile)

    if is_ctx:
        c0 = jnp.zeros((bsz, 2, HW, LANES), f32)
        n0 = jnp.zeros((bsz, 2, 1, HW), f32)
        m0 = jnp.zeros((bsz, 2, 1, LANES), f32)
        s0 = jnp.zeros((bsz, 2, HW, LANES), f32)
    else:
        c0 = _stack_state(cache['C'])
        n0 = cache['n'].reshape(bsz, 2, 1, HW)
        m0 = jnp.stack([jnp.pad(cache['m'][:, d], ((0, 0), (2 * N_HEADS * d, LANES - N_HEADS - 2 * N_HEADS * d)))
                        for d in range(2)], axis=1)[:, :, None, :]
        s0 = _stack_state(cache['S'])
    ym, c_fin, n_fin, m_fin = _mlstm(mq, mg, c0, n0, m0, bsz, n)
    yr, s_fin = _retention(rt, s0, lw['ret_dl'], lw['ret_dr'], bsz, n)

    at3 = at.reshape(bsz, n, A_W + 2 * A_KV_W)
    kcol = A_W // A_KV_W
    if is_ctx:
        dummy = jnp.zeros((SUBLANES, LANES), f32)
        ck_spec = pl.BlockSpec((None, n, A_KV_W), lambda b, i: (b, 0, kcol))
        cv_spec = pl.BlockSpec((None, n, A_KV_W), lambda b, i: (b, 0, kcol + 1))
        ya = _attention(at, at3, at3, ck_spec, cv_spec, dummy, dummy, lw['sink'], bsz, n, n, False)
    else:
        n_ctx = cache['k'].shape[2]
        ck_spec = pl.BlockSpec((None, None, n_ctx, A_KV_W), lambda b, i: (b, layer, 0, 0))
        cos, sin = _rope_tables(n)
        ya = _attention(at, cache['k'], cache['v'], ck_spec, ck_spec, cos, sin, lw['sink'], bsz, n, n_ctx, True)

    yc = _conv(cv, lw['conv_w'], lw['conv_b'], lw['conv_ln_g'], lw['conv_ln_b'], bsz, n)

    x2d = _post(x2d, ym, ya, yc, yr, gt, mod4, lw, layer, row_of_tile)

    ctx = None
    if is_ctx:
        ctx = (at3[:, :, A_W:A_W + A_KV_W].reshape(bsz, n, A_KV_HEADS, DH),
               at3[:, :, A_W + A_KV_W:].reshape(bsz, n, A_KV_HEADS, DH),
               c_fin.reshape(bsz, 2, N_HEADS, DH, DH),
               n_fin.reshape(bsz, 2, N_HEADS, DH),
               jnp.stack([m_fin[:, d, 0, 2 * N_HEADS * d:2 * N_HEADS * d + N_HEADS] for d in range(2)], axis=1),
               s_fin.reshape(bsz, 2, N_HEADS, DH, DH))
    return x2d, ctx


def kernel(x_prompt, x_sample, c, cache_k, cache_v, state_mlstm_C, state_mlstm_n, state_mlstm_m, state_ret,
           c_ctx, w_ada, b_ada, norm_g, w_in, b_in, w_branch, w_out, attn_sink, ret_decay_logit,
           conv_w, conv_b, conv_ln_g, conv_ln_b, w_mlp1, w_mlp2):
    bsz, seq, _ = x_prompt.shape
    dbsz, dseq, _ = x_sample.shape
    past = cache_k.shape[2]

    c_rows = jnp.zeros((SUBLANES, D_MODEL), f32).at[0].set(c_ctx).at[1:1 + dbsz].set(c)
    mod4 = _ada(c_rows, w_ada, b_ada).reshape(DEPTH, SUBLANES, 6, D_MODEL)

    w_pk, b_pk = _pack_w_in(w_in, b_in)
    layers = []
    for l in range(DEPTH):
        layers.append({
            'ng': norm_g[l], 'w_in': w_pk, 'b_in': b_pk,
            'w_branch': w_branch[l].astype(bf16), 'w_out': w_out[l].astype(bf16),
            'w_mlp1': w_mlp1[l].astype(bf16), 'w_mlp2': w_mlp2[l].astype(bf16),
            'sink': jnp.broadcast_to(attn_sink[l][:, None], (A_HEADS, LANES)),
            'ret_dl': jnp.repeat(ret_decay_logit[l], DH, axis=-1).reshape(2, 1, HW),
            'ret_dr': jnp.broadcast_to(jnp.repeat(ret_decay_logit[l], DH, axis=-1)[:, :, None], (2, HW, LANES)),
            'conv_w': conv_w[l], 'conv_b': conv_b[l][None, :],
            'conv_ln_g': conv_ln_g[l][None, :], 'conv_ln_b': conv_ln_b[l][None, :],
        })

    xp = x_prompt.reshape(bsz * seq, D_MODEL)
    ctxs = []
    for l in range(DEPTH):
        xp, ctx = _layer(xp, bsz, seq, l, mod4, layers[l], None, lambda i: 0)
        ctxs.append(ctx)

    xs = x_sample.reshape(dbsz * dseq, D_MODEL)
    tiles_per_seq = dseq // ROW_TILE
    ck = cache_k.reshape(dbsz, DEPTH, past, A_KV_W)
    cv = cache_v.reshape(dbsz, DEPTH, past, A_KV_W)
    for l in range(DEPTH):
        cache = {'k': ck, 'v': cv, 'C': state_mlstm_C[:, l], 'n': state_mlstm_n[:, l],
                 'm': state_mlstm_m[:, l], 'S': state_ret[:, l]}
        xs, _ = _layer(xs, dbsz, dseq, l, mod4, layers[l], cache, lambda i: 1 + i // tiles_per_seq)

    stack = lambda j: jnp.stack([ctxs[l][j] for l in range(DEPTH)], axis=1)
    return (xp.reshape(bsz, seq, D_MODEL), xs.reshape(dbsz, dseq, D_MODEL),
            stack(0), stack(1), stack(2), stack(3), stack(4), stack(5))
```

```python
import functools

import numpy as np
import jax
import jax.numpy as jnp
from jax import lax
from jax.experimental import pallas as pl
from jax.experimental.pallas import tpu as pltpu

f32 = jnp.float32
bf16 = jnp.bfloat16

D_MODEL = 1024
DEPTH = 2
GRID_W = 64
BLOCK = 128
N_HEADS = 4
DH = 64
HW = N_HEADS * DH
A_HEADS = 8
A_KV_HEADS = 2
A_GROUP = A_HEADS // A_KV_HEADS
A_W = A_HEADS * DH
A_KV_W = A_KV_HEADS * DH
WINDOW = 128
ROPE_BASE = 10000.0
CONV_K = 31
D_FF = 4 * D_MODEL
N_BRANCH = 4
EPS = 1e-6
NEG = -1e30
LOG2E = 1.4426950408889634
LANES = 128
SUBLANES = 8

N_GATES = 4 * N_HEADS
PK_MQ = 0
PK_MG = PK_MQ + 4 * HW
PK_AT = PK_MG + 2 * LANES
PK_CV = PK_AT + A_W + 2 * A_KV_W
PK_RT = PK_CV + 2 * HW
PK_GT = PK_RT + 4 * HW
PK_W = PK_GT + N_BRANCH * D_MODEL
BR_OFFS = (0, HW, HW + A_W, 2 * HW + A_W, 3 * HW + A_W)

VMEM_LIMIT = 56 * 1024 * 1024
ROW_TILE = 512


def _sigmoid(x):
    return 0.5 * (jnp.tanh(0.5 * x) + 1.0)


def _log_sigmoid(x):
    return jnp.minimum(x, 0.0) - jnp.log(1.0 + jnp.exp(-jnp.abs(x)))


def _dot(a, b):
    return jnp.dot(a, b, preferred_element_type=f32)


def _dot_nt(a, b):
    return lax.dot_general(a, b, (((1,), (1,)), ((), ())), preferred_element_type=f32)


def _dot2_r(t, x):
    x1 = x.astype(bf16)
    x2 = (x - x1.astype(f32)).astype(bf16)
    r = _dot(t, jnp.concatenate([x1, x2], axis=1))
    return r[:, 0:x.shape[1]] + r[:, x.shape[1]:]


def _dot2_l(x, t):
    x1 = x.astype(bf16)
    x2 = (x - x1.astype(f32)).astype(bf16)
    r = _dot(jnp.concatenate([x1, x2], axis=0), t)
    return r[0:x.shape[0]] + r[x.shape[0]:]


def _const_spec(shape):
    nd = len(shape)
    return pl.BlockSpec(shape, lambda *_: (0,) * nd, pipeline_mode=pl.Buffered(1))


def _params(n_axes):
    return pltpu.CompilerParams(dimension_semantics=("arbitrary",) * n_axes, vmem_limit_bytes=VMEM_LIMIT)


def _ada_kernel(c_ref, w_ref, b_ref, o_ref):
    c = c_ref[...]
    s = (c * _sigmoid(c)).astype(bf16)
    o_ref[...] = _dot(s, w_ref[...].astype(bf16)) + b_ref[...]


def _ada(c_rows, w_ada, b_ada):
    tn = 1536
    n_out = 6 * D_MODEL
    return pl.pallas_call(
        _ada_kernel,
        grid=(DEPTH, n_out // tn),
        in_specs=[
            pl.BlockSpec((SUBLANES, D_MODEL), lambda l, j: (0, 0)),
            pl.BlockSpec((None, D_MODEL, tn), lambda l, j: (l, 0, j)),
            pl.BlockSpec((None, 1, tn), lambda l, j: (l, 0, j)),
        ],
        out_specs=pl.BlockSpec((None, SUBLANES, tn), lambda l, j: (l, 0, j)),
        out_shape=jax.ShapeDtypeStruct((DEPTH, SUBLANES, n_out), f32),
        compiler_params=_params(2),
        name="ada",
    )(c_rows, w_ada, b_ada.reshape(DEPTH, 1, n_out))


def _inproj_kernel(x_ref, mod_ref, ng_ref, w_ref, b_ref, mq_ref, mg_ref, at_ref, cv_ref, rt_ref, gt_ref):
    x = x_ref[...]
    y = x * lax.rsqrt(jnp.mean(x * x, axis=-1, keepdims=True) + EPS) * ng_ref[0:1, :]
    h = (y * (1.0 + mod_ref[1:2, :]) + mod_ref[0:1, :]).astype(bf16)
    chunk = 512
    for ref, base in ((mq_ref, PK_MQ), (mg_ref, PK_MG), (at_ref, PK_AT), (cv_ref, PK_CV), (rt_ref, PK_RT),
                      (gt_ref, PK_GT)):
        total = ref.shape[-1]
        for c in range(0, total, chunk):
            w = min(chunk, total - c)
            r = _dot_nt(h, w_ref[base + c:base + c + w, :]) + b_ref[:, base + c:base + c + w]
            ref[:, c:c + w] = r.astype(ref.dtype)


def _inproj(x2d, mod4, ng, w_pk, b_pk, layer, row_of_tile):
    m = x2d.shape[0]
    tm = ROW_TILE
    widths = (4 * HW, 2 * LANES, A_W + 2 * A_KV_W, 2 * HW, 4 * HW, N_BRANCH * D_MODEL)
    dtypes = (bf16, f32, f32, bf16, bf16, bf16)
    return pl.pallas_call(
        _inproj_kernel,
        grid=(m // tm,),
        in_specs=[
            pl.BlockSpec((tm, D_MODEL), lambda i: (i, 0)),
            pl.BlockSpec((None, None, 6, D_MODEL), lambda i: (layer, row_of_tile(i), 0, 0)),
            _const_spec((4, D_MODEL)),
            pl.BlockSpec((None, PK_W, D_MODEL), lambda i: (layer, 0, 0), pipeline_mode=pl.Buffered(1)),
            pl.BlockSpec((None, 1, PK_W), lambda i: (layer, 0, 0), pipeline_mode=pl.Buffered(1)),
        ],
        out_specs=[pl.BlockSpec((tm, w), lambda i: (i, 0)) for w in widths],
        out_shape=[jax.ShapeDtypeStruct((m, w), dt) for w, dt in zip(widths, dtypes)],
        compiler_params=_params(1),
        name="inproj",
    )(x2d, mod4, ng, w_pk, b_pk)


def _head_lane_masks():
    lane_head = lax.broadcasted_iota(jnp.int32, (1, HW), 1) // DH
    return [lane_head == h for h in range(N_HEADS)]


def _run_interleaved(gens):
    results = [None] * len(gens)
    active = list(range(len(gens)))
    while active:
        for idx in list(active):
            try:
                next(gens[idx])
            except StopIteration as stop:
                results[idx] = stop.value
                active.remove(idx)
    return results


PAIRS = N_HEADS // 2


def _pair_diag_mask():
    r = (lax.broadcasted_iota(jnp.int32, (HW, LANES), 0) // DH) % 2
    c = lax.broadcasted_iota(jnp.int32, (HW, LANES), 1) // DH
    return r == c


def _expand_state(x):
    return jnp.where(_pair_diag_mask(), x + pltpu.roll(x, DH, 1), 0.0)


def _compact_state(c):
    return (c + pltpu.roll(c, DH, 1))[:, 0:DH]


def _pair_qk(q, kt):
    zeros = jnp.zeros((DH, BLOCK), kt.dtype)
    outs = []
    for j in range(PAIRS):
        rows = kt[LANES * j:LANES * (j + 1)]
        blk = jnp.concatenate([jnp.concatenate([rows[0:DH], zeros], axis=0),
                               jnp.concatenate([zeros, rows[DH:2 * DH]], axis=0)], axis=1)
        outs.append(_dot(q[:, LANES * j:LANES * (j + 1)], blk))
    return jnp.concatenate(outs, axis=1)


def _pair_pv(p, v):
    low = lax.broadcasted_iota(jnp.int32, (1, LANES), 1) < DH
    outs = []
    for j in range(PAIRS):
        vp = v[:, LANES * j:LANES * (j + 1)]
        zero = jnp.zeros_like(vp)
        blk = jnp.concatenate([jnp.where(low, vp, zero), jnp.where(low, zero, vp)], axis=0)
        outs.append(_dot(p[:, 2 * BLOCK * j:2 * BLOCK * (j + 1)], blk))
    return jnp.concatenate(outs, axis=1)


def _pair_state_read(q, c):
    return jnp.concatenate([_dot(q[:, LANES * j:LANES * (j + 1)], c[LANES * j:LANES * (j + 1)])
                            for j in range(PAIRS)], axis=1)


def _pair_state_update(a_t, v):
    upd = jnp.concatenate([_dot(a_t[LANES * j:LANES * (j + 1)], v[:, LANES * j:LANES * (j + 1)])
                           for j in range(PAIRS)], axis=0)
    return jnp.where(_pair_diag_mask(), upd, 0.0)


def _mlstm_kernel(x_ref, g_ref, c0_ref, n0_ref, m0_ref, y_ref, cn_ref, nn_ref, mn_ref,
                  h_s, c_s, n_s, m_s, *, nc, nb):
    wide = N_HEADS * BLOCK
    row = lax.broadcasted_iota(jnp.int32, (BLOCK, BLOCK), 0)
    col = lax.broadcasted_iota(jnp.int32, (BLOCK, BLOCK), 1)
    tsum = (jnp.where(row >= col, 1.0, 0.0).astype(bf16), jnp.where(row <= col, 1.0, 0.0).astype(bf16))
    roww = lax.broadcasted_iota(jnp.int32, (BLOCK, wide), 0)
    colw = lax.broadcasted_iota(jnp.int32, (BLOCK, wide), 1) & (BLOCK - 1)
    tri4 = (roww >= colw, roww <= colw)
    masks = _head_lane_masks()
    lane = lax.broadcasted_iota(jnp.int32, (1, LANES), 1)
    row8 = lax.broadcasted_iota(jnp.int32, (SUBLANES, BLOCK), 0)

    def onehot(shape, row_expr, col_expr):
        r = lax.broadcasted_iota(jnp.int32, shape, 0)
        c = lax.broadcasted_iota(jnp.int32, shape, 1)
        return jnp.where(row_expr(r) == col_expr(c), 1.0, 0.0).astype(bf16)

    g0s = (0, 2 * N_HEADS)
    valid = tuple((lane >= g0) & (lane < g0 + N_HEADS) for g0 in g0s)
    hexp = tuple(onehot((LANES, HW), lambda r: r, lambda c, g0=g0: c // DH + g0) for g0 in g0s)
    hselg = tuple(onehot((HW, LANES), lambda r, g0=g0: r // DH + g0, lambda c: c) for g0 in g0s)

    chains = [(bb, d) for bb in range(nb) for d in range(2)]
    for bb, d in chains:
        c_s[bb, d] = _expand_state(c0_ref[bb, d])
        n_s[bb, d] = n0_ref[bb, d]
        m_s[bb, d] = m0_ref[bb, d]

    def load(bb, d, c):
        r0 = pl.multiple_of(c * BLOCK, BLOCK)
        return (r0, x_ref[bb, pl.ds(r0, BLOCK), 0:3 * HW], g_ref[bb, pl.ds(r0, BLOCK), :],
                c_s[bb, d], n_s[bb, d], m_s[bb, d])

    def compute(d, blk, g, cmat, nrow, m_row):
        g0 = g0s[d]
        q = blk[:, 0:HW]
        k = blk[:, HW:2 * HW] * (DH ** -0.5)
        v = blk[:, 2 * HW:3 * HW]
        ga = g[:, 0:LANES]
        gb = g[:, LANES:2 * LANES]
        bc = _dot2_r(tsum[d], _log_sigmoid(gb))
        yield
        beta = ga - bc
        beta_t = beta.T
        yield
        cmx = beta
        sh = 1
        while sh < BLOCK:
            if d == 0:
                cmx = jnp.maximum(cmx, jnp.where(row >= sh, pltpu.roll(cmx, sh, 0), NEG))
            else:
                cmx = jnp.maximum(cmx, jnp.where(row < BLOCK - sh, pltpu.roll(cmx, BLOCK - sh, 0), NEG))
            sh *= 2
        mx = jnp.maximum(m_row, cmx)
        alpha = jnp.where(valid[d], -mx, 0.0)
        w_int = jnp.where(valid[d], jnp.exp(m_row - mx), 0.0)
        e_nb = jnp.exp(alpha - bc)
        yield
        z = jnp.concatenate([jnp.broadcast_to(alpha[:, g0 + h:g0 + h + 1], (BLOCK, BLOCK)) for h in range(N_HEADS)],
                            axis=1)
        beta_w = jnp.concatenate([beta_t[g0 + h:g0 + h + 1, :] for h in range(N_HEADS)], axis=1)
        ktf = k.astype(f32).T
        s_wide = _pair_qk(q, ktf.astype(bf16))
        yield
        p = s_wide * jnp.exp(jnp.where(tri4[d], z + beta_w, NEG))
        num = _pair_pv(p.astype(bf16), v)
        yield
        den = jnp.zeros((BLOCK, LANES), f32)
        for h in range(N_HEADS):
            den = jnp.where(lane == g0 + h, jnp.sum(p[:, BLOCK * h:BLOCK * (h + 1)], axis=1, keepdims=True), den)
        qc = _pair_state_read(q, cmat.astype(bf16))
        qn = _dot(q * nrow.astype(bf16), hselg[d])
        yield
        den = den + w_int * qn
        inv = jnp.where(valid[d], 1.0 / jnp.maximum(jnp.abs(den), e_nb), 0.0)
        fac = _dot(jnp.concatenate([w_int, inv], axis=0).astype(bf16), hexp[d])
        yield
        hout = (num + fac[0:BLOCK] * qc) * fac[BLOCK:2 * BLOCK]
        last = BLOCK - 1 if d == 0 else 0
        a_last = alpha[last:last + 1, :]
        m_new = jnp.where(valid[d], bc[last:last + 1, :] - a_last, 0.0)
        dec = jnp.exp(m_row + a_last)
        wr = jnp.zeros((SUBLANES, BLOCK), f32)
        kws = []
        for h in range(N_HEADS):
            w_row = jnp.exp(beta_t[g0 + h:g0 + h + 1, :] + a_last[:, g0 + h:g0 + h + 1])
            kws.append(ktf[DH * h:DH * (h + 1), :] * w_row)
            wr = jnp.where(row8 == h, w_row, wr)
        upd = _pair_state_update(jnp.concatenate(kws, axis=0).astype(bf16), v)
        wk = _dot(wr.astype(bf16), k)
        yield
        nnew = jnp.zeros((1, HW), f32)
        cnew = []
        for h in range(N_HEADS):
            rs = slice(DH * h, DH * (h + 1))
            dh = dec[:, g0 + h:g0 + h + 1]
            cnew.append(dh * cmat[rs, :] + upd[rs, :])
            nnew = jnp.where(masks[h], dh * nrow + wk[h:h + 1, :], nnew)
        return hout, jnp.concatenate(cnew, axis=0), nnew, m_new

    def step(i, second):
        loaded = [load(bb, d, i if d == 0 else nc - 1 - i) for bb, d in chains]
        if second:
            other = [h_s[bb, pl.ds(l[0], BLOCK), :] for (bb, d), l in zip(chains, loaded)]
            ogate = [x_ref[bb, pl.ds(l[0], BLOCK), 3 * HW:4 * HW] for (bb, d), l in zip(chains, loaded)]
        outs = _run_interleaved([compute(d, *l[1:]) for (bb, d), l in zip(chains, loaded)])
        for idx, ((bb, d), l) in enumerate(zip(chains, loaded)):
            hout, cnew, nnew, m_new = outs[idx]
            if second:
                y = _sigmoid(ogate[idx].astype(f32)) * (other[idx] + hout)
                y_ref[bb, pl.ds(l[0], BLOCK), :] = y.astype(y_ref.dtype)
            else:
                h_s[bb, pl.ds(l[0], BLOCK), :] = hout
            c_s[bb, d] = cnew
            n_s[bb, d] = nnew
            m_s[bb, d] = m_new

    def first_half(i, carry):
        step(i, False)
        return carry

    def second_half(i, carry):
        step(i, True)
        return carry

    lax.fori_loop(0, nc // 2, first_half, 0)
    lax.fori_loop(nc // 2, nc, second_half, 0)

    for bb, d in chains:
        cn_ref[bb, d] = _compact_state(c_s[bb, d])
        nn_ref[bb, d] = n_s[bb, d]
        mn_ref[bb, d] = m_s[bb, d]


SCAN_SEQS = 4


def _mlstm(mq, mg, c0, n0, m0, bsz, n):
    nc = n // BLOCK
    nb = min(SCAN_SEQS, bsz)
    assert nc % 2 == 0 and bsz % nb == 0
    spec = lambda shape: pl.BlockSpec((nb,) + shape, lambda b: (b,) + (0,) * len(shape))
    seq_in = (lambda shape: pl.BlockSpec((nb,) + shape, lambda b: (b,) + (0,) * len(shape),
                                         pipeline_mode=pl.Buffered(1))) if bsz == nb else spec
    return pl.pallas_call(
        functools.partial(_mlstm_kernel, nc=nc, nb=nb),
        grid=(bsz // nb,),
        in_specs=[
            seq_in((n, 4 * HW)),
            seq_in((n, 2 * LANES)),
            spec((2, HW, LANES)),
            spec((2, 1, HW)),
            spec((2, 1, LANES)),
        ],
        out_specs=[
            spec((n, HW)),
            spec((2, HW, DH)),
            spec((2, 1, HW)),
            spec((2, 1, LANES)),
        ],
        out_shape=[
            jax.ShapeDtypeStruct((bsz, n, HW), bf16),
            jax.ShapeDtypeStruct((bsz, 2, HW, DH), f32),
            jax.ShapeDtypeStruct((bsz, 2, 1, HW), f32),
            jax.ShapeDtypeStruct((bsz, 2, 1, LANES), f32),
        ],
        scratch_shapes=[
            pltpu.VMEM((nb, n, HW), f32),
            pltpu.VMEM((nb, 2, HW, LANES), f32),
            pltpu.VMEM((nb, 2, 1, HW), f32),
            pltpu.VMEM((nb, 2, 1, LANES), f32),
        ],
        compiler_params=_params(1),
        name="mlstm",
    )(mq.reshape(bsz, n, 4 * HW), mg.reshape(bsz, n, 2 * LANES), c0, n0, m0)


def _ret_kernel(x_ref, s0_ref, dl_ref, dr_ref, y_ref, sn_ref, o_s, s_s, qd_s, kd_s, cd_s, dm_s, *, nc, nb):
    avg = jnp.where(lax.broadcasted_iota(jnp.int32, (HW, HW), 0) // DH ==
                    lax.broadcasted_iota(jnp.int32, (HW, HW), 1) // DH, 1.0 / DH, 0.0).astype(bf16)

    @pl.when(pl.program_id(0) == 0)
    def _decay_tables():
        rowf = lax.broadcasted_iota(jnp.int32, (BLOCK, HW), 0).astype(f32)
        rel = (lax.broadcasted_iota(jnp.int32, (BLOCK, BLOCK), 0) -
               lax.broadcasted_iota(jnp.int32, (BLOCK, BLOCK), 1)).astype(f32)
        for d in range(2):
            lg = _log_sigmoid(dl_ref[d])
            if d == 0:
                qd_s[d] = jnp.exp((rowf + 1.0) * lg)
                kd_s[d] = jnp.exp((BLOCK - 1.0 - rowf) * lg)
                reld = rel
            else:
                qd_s[d] = jnp.exp((BLOCK - rowf) * lg)
                kd_s[d] = jnp.exp(rowf * lg)
                reld = -rel
            cd_s[d] = jnp.exp(float(BLOCK) * _log_sigmoid(dr_ref[d]))
            for h in range(N_HEADS):
                lgh = lg[:, DH * h:DH * h + 1]
                dm_s[d, :, BLOCK * h:BLOCK * (h + 1)] = jnp.where(reld >= 0.0,
                                                                  jnp.exp(jnp.maximum(reld, 0.0) * lgh), 0.0)

    chains = [(bb, d) for bb in range(nb) for d in range(2)]
    for bb, d in chains:
        s_s[bb, d] = _expand_state(s0_ref[bb, d])

    def compute(d, blk, sm, other, gate):
        q = blk[:, 0:HW]
        k = blk[:, HW:2 * HW] * (DH ** -0.5)
        v = blk[:, 2 * HW:3 * HW]
        kf = k.astype(f32)
        p = _pair_qk(q, kf.T.astype(bf16)) * dm_s[d]
        yield
        o = _pair_pv(p.astype(bf16), v) + qd_s[d] * _pair_state_read(q, sm.astype(bf16))
        yield
        kdt = (kf * kd_s[d]).T.astype(bf16)
        snew = cd_s[d] * sm + _pair_state_update(kdt, v)
        yield
        if other is None:
            return o, snew
        o = o + other
        gate = gate.astype(f32)
        dev = o - _dot2_l(o, avg)
        yield
        var = _dot((dev * dev).astype(bf16), avg)
        yield
        return gate * _sigmoid(gate) * (dev * lax.rsqrt(var + EPS)), snew

    def step(i, second):
        r0s = [pl.multiple_of((i if d == 0 else nc - 1 - i) * BLOCK, BLOCK) for bb, d in chains]
        loaded = [(x_ref[bb, pl.ds(r0, BLOCK), 0:3 * HW], s_s[bb, d]) for (bb, d), r0 in zip(chains, r0s)]
        if second:
            other = [o_s[bb, pl.ds(r0, BLOCK), :] for (bb, d), r0 in zip(chains, r0s)]
            gates = [x_ref[bb, pl.ds(r0, BLOCK), 3 * HW:4 * HW] for (bb, d), r0 in zip(chains, r0s)]
        else:
            other = gates = [None] * len(chains)
        outs = _run_interleaved([compute(d, *l, other[idx], gates[idx])
                                 for idx, ((bb, d), l) in enumerate(zip(chains, loaded))])
        for idx, ((bb, d), r0) in enumerate(zip(chains, r0s)):
            o, snew = outs[idx]
            if second:
                y_ref[bb, pl.ds(r0, BLOCK), :] = o.astype(y_ref.dtype)
            else:
                o_s[bb, pl.ds(r0, BLOCK), :] = o
            s_s[bb, d] = snew

    def first_half(i, carry):
        step(i, False)
        return carry

    def second_half(i, carry):
        step(i, True)
        return carry

    lax.fori_loop(0, nc // 2, first_half, 0)
    lax.fori_loop(nc // 2, nc, second_half, 0)

    for bb, d in chains:
        sn_ref[bb, d] = _compact_state(s_s[bb, d])


def _retention(rt, s0, dl, dr, bsz, n):
    nc = n // BLOCK
    nb = min(SCAN_SEQS, bsz)
    assert nc % 2 == 0 and bsz % nb == 0
    spec = lambda shape: pl.BlockSpec((nb,) + shape, lambda b: (b,) + (0,) * len(shape))
    seq_in = (lambda shape: pl.BlockSpec((nb,) + shape, lambda b: (b,) + (0,) * len(shape),
                                         pipeline_mode=pl.Buffered(1))) if bsz == nb else spec
    return pl.pallas_call(
        functools.partial(_ret_kernel, nc=nc, nb=nb),
        grid=(bsz // nb,),
        in_specs=[
            seq_in((n, 4 * HW)),
            spec((2, HW, LANES)),
            _const_spec((2, 1, HW)),
            _const_spec((2, HW, LANES)),
        ],
        out_specs=[
            spec((n, HW)),
            spec((2, HW, DH)),
        ],
        out_shape=[
            jax.ShapeDtypeStruct((bsz, n, HW), bf16),
            jax.ShapeDtypeStruct((bsz, 2, HW, DH), f32),
        ],
        scratch_shapes=[
            pltpu.VMEM((nb, n, HW), f32),
            pltpu.VMEM((nb, 2, HW, LANES), f32),
            pltpu.VMEM((2, BLOCK, HW), f32),
            pltpu.VMEM((2, BLOCK, HW), f32),
            pltpu.VMEM((2, HW, LANES), f32),
            pltpu.VMEM((2, BLOCK, N_HEADS * BLOCK), f32),
        ],
        compiler_params=_params(1),
        name="retention",
    )(rt.reshape(bsz, n, 4 * HW), s0, dl, dr)


def _attn_kernel(q_ref, k_ref, v_ref, ck_ref, cv_ref, cos_ref, sin_ref, sk_ref, o_ref, kt_s, vt_s,
                 *, n_ctx, n, band, qb, qps):
    i = pl.program_id(1)
    lane = lax.broadcasted_iota(jnp.int32, (1, LANES), 1)
    lo = lane < DH
    first = (lane % (DH // 2)) < (DH // 4)
    gmasks = _head_lane_masks()
    n_cblk = n_ctx // BLOCK
    wide = A_GROUP * qb

    def tile4(x):
        xr = pltpu.roll(x, DH, 1)
        a2 = jnp.where(lo, x, xr).astype(bf16)
        b2 = jnp.where(lo, xr, x).astype(bf16)
        return jnp.concatenate([a2, a2], axis=1), jnp.concatenate([b2, b2], axis=1)

    def rope(x, cos, sin):
        sw = jnp.where(first, pltpu.roll(x, LANES - DH // 4, 1), pltpu.roll(x, DH // 4, 1))
        return x * cos + sw * sin

    def put_block(blk, r0, kk, vv):
        k0, k1 = tile4(kk)
        kt_s[0, pl.ds(r0, BLOCK), :] = k0
        kt_s[1, pl.ds(r0, BLOCK), :] = k1
        vt = vv.T
        for j in range(A_KV_HEADS):
            vj = vt[DH * j:DH * (j + 1)]
            vt_s[j, blk] = jnp.concatenate([vj, vj], axis=0).astype(bf16)

    @pl.when(i == 0)
    def _prepare_keys():
        for c in range(n_cblk):
            put_block(c, c * BLOCK, ck_ref[c * BLOCK:(c + 1) * BLOCK, :], cv_ref[c * BLOCK:(c + 1) * BLOCK, :])
        if band:
            nb = n // BLOCK
            for blk in (n_cblk, n_cblk + 1 + nb):
                for j in range(A_KV_HEADS):
                    kt_s[j, blk * BLOCK:(blk + 1) * BLOCK, :] = jnp.zeros((BLOCK, HW), bf16)
                    vt_s[j, blk] = jnp.zeros((BLOCK, BLOCK), bf16)

            def body(c, carry):
                r0 = pl.multiple_of(c * BLOCK, BLOCK)
                kk = rope(k_ref[pl.ds(r0, BLOCK), :], cos_ref[pl.ds(r0, BLOCK), :], sin_ref[pl.ds(r0, BLOCK), :])
                put_block(n_cblk + 1 + c, pl.multiple_of(n_ctx + BLOCK + r0, BLOCK), kk, v_ref[pl.ds(r0, BLOCK), :])
                return carry

            lax.fori_loop(0, nb, body, 0)

    row_lo = lax.broadcasted_iota(jnp.int32, (2 * DH, qb), 0) < DH

    def query_block(u):
        q = q_ref[qb * u:qb * (u + 1), :]
        if not band:
            return q * (DH ** -0.5 * LOG2E), None, None
        iq = i * qps + u
        q0 = pl.multiple_of(iq * BLOCK, BLOCK)
        cos = cos_ref[pl.ds(q0, BLOCK), :]
        sin = sin_ref[pl.ds(q0, BLOCK), :]
        q = jnp.concatenate([rope(q[:, LANES * t:LANES * (t + 1)], cos, sin) for t in range(A_W // LANES)], axis=1)
        kj = lax.broadcasted_iota(jnp.int32, (3 * BLOCK, BLOCK), 0)
        qi = lax.broadcasted_iota(jnp.int32, (3 * BLOCK, BLOCK), 1)
        tok = kj + (iq - 1) * BLOCK
        ok = (kj >= qi) & (kj <= qi + 2 * WINDOW) & (tok >= 0) & (tok < n)
        bias = jnp.where(ok, 0.0, NEG)
        return (q * (DH ** -0.5 * LOG2E), jnp.concatenate([bias] * A_GROUP, axis=1),
                (pl.multiple_of(n_ctx + iq * BLOCK, BLOCK), n_cblk + iq))

    def kv_head(j, q, bias4, pos):
        q256 = q[:, HW * j:HW * (j + 1)]
        qs = jnp.concatenate([jnp.where(gmasks[g], q256, 0.0) for g in range(A_GROUP)], axis=0).astype(bf16)
        sink = jnp.concatenate(
            [jnp.broadcast_to(sk_ref[A_GROUP * j + g:A_GROUP * j + g + 1, 0:1], (1, qb)) for g in range(A_GROUP)],
            axis=1) * LOG2E
        s1 = _dot_nt(kt_s[j, 0:n_ctx, :], qs)
        yield
        m = jnp.maximum(jnp.max(s1, axis=0, keepdims=True), sink)
        if band:
            s2 = _dot_nt(kt_s[j, pl.ds(pos[0], 3 * BLOCK), :], qs) + bias4
            vband = vt_s[j, pl.ds(pos[1], 3)]
            yield
            m = jnp.maximum(m, jnp.max(s2, axis=0, keepdims=True))
        p1 = jnp.exp2(s1 - m)
        den = jnp.sum(p1, axis=0, keepdims=True) + jnp.exp2(sink - m)
        vctx = jnp.concatenate([vt_s[j, c] for c in range(n_cblk)], axis=1)
        acc = _dot(vctx, p1.astype(bf16))
        yield
        if band:
            p2 = jnp.exp2(s2 - m)
            den = den + jnp.sum(p2, axis=0, keepdims=True)
            acc = acc + _dot(jnp.concatenate([vband[t] for t in range(3)], axis=1), p2.astype(bf16))
            yield
        acc = acc * (1.0 / den)
        pairs = []
        for g in range(0, A_GROUP, 2):
            pair = jnp.where(row_lo, acc[:, qb * g:qb * (g + 1)], acc[:, qb * (g + 1):qb * (g + 2)])
            pairs.append(pair.T)
        return pairs

    blocks = [query_block(u) for u in range(qps)]
    outs = _run_interleaved([kv_head(j, *blocks[u]) for u in range(qps) for j in range(A_KV_HEADS)])
    for u in range(qps):
        tiles = [t for pairs in outs[A_KV_HEADS * u:A_KV_HEADS * (u + 1)] for t in pairs]
        o_ref[qb * u:qb * (u + 1), :] = jnp.concatenate(tiles, axis=1).astype(o_ref.dtype)


def _attention(at, ck, cv, ck_spec, cv_spec, cos, sin, sk, bsz, n, n_ctx, band):
    qb = BLOCK if band else min(n, 2 * BLOCK)
    qps = 2 if band else 1
    nq = n // (qb * qps)
    s_tot = n_ctx + (n + 2 * BLOCK if band else 0)
    at3 = at.reshape(bsz, n, A_W + 2 * A_KV_W)
    kcol = A_W // A_KV_W
    return pl.pallas_call(
        functools.partial(_attn_kernel, n_ctx=n_ctx, n=n, band=band, qb=qb, qps=qps),
        grid=(bsz, nq),
        in_specs=[
            pl.BlockSpec((None, qb * qps, A_W), lambda b, i: (b, i, 0)),
            pl.BlockSpec((None, n, A_KV_W), lambda b, i: (b, 0, kcol)),
            pl.BlockSpec((None, n, A_KV_W), lambda b, i: (b, 0, kcol + 1)),
            ck_spec,
            cv_spec,
            _const_spec(cos.shape),
            _const_spec(sin.shape),
            _const_spec((A_HEADS, LANES)),
        ],
        out_specs=pl.BlockSpec((None, qb * qps, A_W), lambda b, i: (b, i, 0)),
        out_shape=jax.ShapeDtypeStruct((bsz, n, A_W), bf16),
        scratch_shapes=[
            pltpu.VMEM((A_KV_HEADS, s_tot, HW), bf16),
            pltpu.VMEM((A_KV_HEADS, s_tot // BLOCK, BLOCK, BLOCK), bf16),
        ],
        compiler_params=_params(2),
        name="attention",
    )(at3, at3, at3, ck, cv, cos, sin, sk)


def _rope_tables(n):
    tok = np.arange(n)
    pos = np.stack([tok // GRID_W, tok % GRID_W], axis=1).astype(np.float32)
    quarter = DH // 4
    freqs = np.power(np.float32(ROPE_BASE), -np.arange(quarter, dtype=np.float32) / np.float32(quarter))
    lane = np.arange(LANES) % DH
    axis = lane // (DH // 2)
    fidx = lane % quarter
    sign = np.where((lane % (DH // 2)) < quarter, -1.0, 1.0)
    ang = (pos[:, axis] * freqs.astype(np.float32)[fidx][None, :]).astype(np.float32).astype(np.float64)
    return jnp.asarray(np.cos(ang), f32), jnp.asarray(np.sin(ang) * sign[None, :], f32)


CONV_PAD = 16


def _conv_kernel(u_ref, w_ref, b_ref, lg_ref, lb_ref, y_ref, z_s, *, n):
    nc = n // BLOCK
    z_s[0:CONV_PAD, :] = jnp.zeros((CONV_PAD, HW), f32)
    z_s[CONV_PAD + n:2 * CONV_PAD + n, :] = jnp.zeros((CONV_PAD, HW), f32)

    def glu(c, carry):
        r0 = pl.multiple_of(c * BLOCK, BLOCK)
        blk = u_ref[pl.ds(r0, BLOCK), :].astype(f32)
        z_s[pl.ds(pl.multiple_of(r0 + CONV_PAD, SUBLANES), BLOCK), :] = blk[:, 0:HW] * _sigmoid(blk[:, HW:2 * HW])
        return carry

    lax.fori_loop(0, nc, glu, 0)

    first = CONV_PAD - CONV_K // 2
    span = BLOCK + 2 * CONV_PAD

    def tile(r0):
        win = z_s[pl.ds(r0, span), :]
        acc = jnp.zeros((BLOCK, HW), f32)
        for r in range(SUBLANES):
            taps = [kk for kk in range(CONV_K) if (first + kk) % SUBLANES == r]
            if not taps:
                continue
            shifted = win if r == 0 else pltpu.roll(win, span - r, 0)
            for kk in taps:
                a = (first + kk) // SUBLANES * SUBLANES
                acc = acc + shifted[a:a + BLOCK, :] * w_ref[kk:kk + 1, :]
            yield
        zc = acc + b_ref[...]
        mu = jnp.mean(zc, axis=-1, keepdims=True)
        dev = zc - mu
        var = jnp.mean(dev * dev, axis=-1, keepdims=True)
        t = dev * lax.rsqrt(var + EPS) * lg_ref[...] + lb_ref[...]
        return (t * _sigmoid(t)).astype(y_ref.dtype)

    def two_tiles(c, carry):
        r0s = [pl.multiple_of((2 * c + u) * BLOCK, BLOCK) for u in range(2)]
        outs = _run_interleaved([tile(r0) for r0 in r0s])
        for r0, out in zip(r0s, outs):
            y_ref[pl.ds(r0, BLOCK), :] = out
        return carry

    lax.fori_loop(0, nc // 2, two_tiles, 0)


def _conv(cu, w, b, lg, lb, bsz, n):
    return pl.pallas_call(
        functools.partial(_conv_kernel, n=n),
        grid=(bsz,),
        in_specs=[
            pl.BlockSpec((None, n, 2 * HW), lambda bb: (bb, 0, 0)),
            _const_spec((CONV_K, HW)),
            _const_spec((1, HW)),
            _const_spec((1, HW)),
            _const_spec((1, HW)),
        ],
        out_specs=pl.BlockSpec((None, n, HW), lambda bb: (bb, 0, 0)),
        out_shape=jax.ShapeDtypeStruct((bsz, n, HW), bf16),
        scratch_shapes=[pltpu.VMEM((n + 2 * CONV_PAD, HW), f32)],
        compiler_params=_params(1),
        name="conv",
    )(cu.reshape(bsz, n, 2 * HW), w, b, lg, lb)


def _post_kernel(x_ref, ym_ref, ya_ref, yc_ref, yr_ref, gt_ref, mod_ref, ng_ref, wbr_ref, wout_ref, w1_ref, w2_ref,
                 o_ref):
    def rms(a):
        return a * lax.rsqrt(jnp.mean(a * a, axis=-1, keepdims=True) + EPS)

    acc = None
    for i, y_ref in enumerate((ym_ref, ya_ref, yc_ref, yr_ref)):
        br = _dot(y_ref[...], wbr_ref[BR_OFFS[i]:BR_OFFS[i + 1], :]).astype(bf16)
        term = jnp.tanh(gt_ref[:, D_MODEL * i:D_MODEL * (i + 1)] * 0.5) * br + br
        acc = term if acc is None else acc + term
    mix = _dot(acc * 0.5, wout_ref[...])
    x = x_ref[...] + mod_ref[2:3, :] * (rms(mix) * ng_ref[1:2, :])
    h = (rms(x) * ng_ref[2:3, :] * (1.0 + mod_ref[4:5, :]) + mod_ref[3:4, :]).astype(bf16)
    chunk = 512
    f = jnp.zeros(x.shape, f32)
    for c in range(0, D_FF, chunk):
        a = jnp.maximum(_dot(h, w1_ref[:, c:c + chunk]), 0.0)
        f = f + _dot((a * a).astype(bf16), w2_ref[c:c + chunk, :])
    o_ref[...] = x + mod_ref[5:6, :] * (rms(f) * ng_ref[3:4, :])


def _post(x2d, ym, ya, yc, yr, gt, mod4, lw, layer, row_of_tile):
    m = x2d.shape[0]
    tm = ROW_TILE
    rows = lambda w: pl.BlockSpec((tm, w), lambda i: (i, 0))
    return pl.pallas_call(
        _post_kernel,
        grid=(m // tm,),
        in_specs=[
            rows(D_MODEL), rows(HW), rows(A_W), rows(HW), rows(HW), rows(N_BRANCH * D_MODEL),
            pl.BlockSpec((None, None, 6, D_MODEL), lambda i: (layer, row_of_tile(i), 0, 0)),
            _const_spec((4, D_MODEL)),
            _const_spec((BR_OFFS[-1], D_MODEL)),
            _const_spec((D_MODEL, D_MODEL)),
            _const_spec((D_MODEL, D_FF)),
            _const_spec((D_FF, D_MODEL)),
        ],
        out_specs=rows(D_MODEL),
        out_shape=jax.ShapeDtypeStruct((m, D_MODEL), f32),
        compiler_params=_params(1),
        name="post",
    )(x2d, ym.reshape(m, HW), ya.reshape(m, A_W), yc.reshape(m, HW), yr.reshape(m, HW), gt, mod4, lw['ng'],
      lw['w_branch'], lw['w_out'], lw['w_mlp1'], lw['w_mlp2'])


PACK_ROWS = 512


def _pack_kernel(wt_ref, o_ref):
    s = pl.program_id(1)
    gate_step = PK_MG // PACK_ROWS

    @pl.when(s != gate_step)
    def _plain():
        o_ref[...] = wt_ref[0].astype(bf16)

    @pl.when(s == gate_step)
    def _gates():
        row = lax.broadcasted_iota(jnp.int32, (LANES, 1), 0)
        gates = wt_ref[0, 0:LANES, :]
        o_ref[0:LANES, :] = jnp.where(row < N_GATES, gates, 0.0).astype(bf16)
        shifted = pltpu.roll(gates, LANES - N_HEADS, 0)
        o_ref[LANES:2 * LANES, :] = jnp.where(row < N_GATES - N_HEADS, shifted, 0.0).astype(bf16)
        o_ref[2 * LANES:PACK_ROWS, :] = wt_ref[0, N_GATES:N_GATES + PACK_ROWS - 2 * LANES, :].astype(bf16)


def _pack_w_in(w_in, b_in):
    gate_step = PK_MG // PACK_ROWS
    shift = 2 * LANES - N_GATES
    assert PK_MG % PACK_ROWS == 0 and PK_W % PACK_ROWS == 0 and PACK_ROWS >= 2 * LANES

    def src_row(s):
        return pl.multiple_of(jnp.where(s <= gate_step, PACK_ROWS * s, PACK_ROWS * s - shift), N_GATES)

    w_pk = pl.pallas_call(
        _pack_kernel,
        grid=(DEPTH, PK_W // PACK_ROWS),
        in_specs=[pl.BlockSpec((pl.Element(1), pl.Element(PACK_ROWS), pl.Element(D_MODEL)),
                               lambda l, s: (l, src_row(s), 0))],
        out_specs=pl.BlockSpec((None, PACK_ROWS, D_MODEL), lambda l, s: (l, s, 0)),
        out_shape=jax.ShapeDtypeStruct((DEPTH, PK_W, D_MODEL), bf16),
        compiler_params=_params(2),
        name="pack_w_in",
    )(jnp.swapaxes(w_in, 1, 2))
    g0 = PK_MG
    pad = lambda cols: jnp.zeros((DEPTH, cols), f32)
    b_pk = jnp.concatenate(
        [b_in[:, :g0 + N_GATES], pad(LANES - N_GATES), b_in[:, g0 + N_HEADS:g0 + N_GATES],
         pad(LANES - N_GATES + N_HEADS), b_in[:, g0 + N_GATES:]], axis=-1)
    return w_pk, b_pk[:, None, :]


def _stack_state(s):
    bsz = s.shape[0]
    s = s.reshape(bsz, 2, HW, DH)
    return jnp.pad(s, ((0, 0), (0, 0), (0, 0), (0, LANES - DH)))


def _layer(x2d, bsz, n, layer, mod4, lw, cache, row_of_tile):
    is_ctx = cache is None
    mq, mg, at, cv, rt, gt = _inproj(x2d, mod4, lw['ng'], lw['w_in'], lw['b_in'], layer, row_of_tile)

    if is_ctx:
        c0 = jnp.zeros((bsz, 2, HW, LANES), f32)
        n0 = jnp.zeros((bsz, 2, 1, HW), f32)
        m0 = jnp.zeros((bsz, 2, 1, LANES), f32)
        s0 = jnp.zeros((bsz, 2, HW, LANES), f32)
    else:
        c0 = _stack_state(cache['C'])
        n0 = cache['n'].reshape(bsz, 2, 1, HW)
        m0 = jnp.stack([jnp.pad(cache['m'][:, d], ((0, 0), (2 * N_HEADS * d, LANES - N_HEADS - 2 * N_HEADS * d)))
                        for d in range(2)], axis=1)[:, :, None, :]
        s0 = _stack_state(cache['S'])
    ym, c_fin, n_fin, m_fin = _mlstm(mq, mg, c0, n0, m0, bsz, n)
    yr, s_fin = _retention(rt, s0, lw['ret_dl'], lw['ret_dr'], bsz, n)

    at3 = at.reshape(bsz, n, A_W + 2 * A_KV_W)
    kcol = A_W // A_KV_W
    if is_ctx:
        dummy = jnp.zeros((SUBLANES, LANES), f32)
        ck_spec = pl.BlockSpec((None, n, A_KV_W), lambda b, i: (b, 0, kcol))
        cv_spec = pl.BlockSpec((None, n, A_KV_W), lambda b, i: (b, 0, kcol + 1))
        ya = _attention(at, at3, at3, ck_spec, cv_spec, dummy, dummy, lw['sink'], bsz, n, n, False)
    else:
        n_ctx = cache['k'].shape[2]
        ck_spec = pl.BlockSpec((None, None, n_ctx, A_KV_W), lambda b, i: (b, layer, 0, 0))
        cos, sin = _rope_tables(n)
        ya = _attention(at, cache['k'], cache['v'], ck_spec, ck_spec, cos, sin, lw['sink'], bsz, n, n_ctx, True)

    yc = _conv(cv, lw['conv_w'], lw['conv_b'], lw['conv_ln_g'], lw['conv_ln_b'], bsz, n)

    x2d = _post(x2d, ym, ya, yc, yr, gt, mod4, lw, layer, row_of_tile)

    ctx = None
    if is_ctx:
        ctx = (at3[:, :, A_W:A_W + A_KV_W].reshape(bsz, n, A_KV_HEADS, DH),
               at3[:, :, A_W + A_KV_W:].reshape(bsz, n, A_KV_HEADS, DH),
               c_fin.reshape(bsz, 2, N_HEADS, DH, DH),
               n_fin.reshape(bsz, 2, N_HEADS, DH),
               jnp.stack([m_fin[:, d, 0, 2 * N_HEADS * d:2 * N_HEADS * d + N_HEADS] for d in range(2)], axis=1),
               s_fin.reshape(bsz, 2, N_HEADS, DH, DH))
    return x2d, ctx


def kernel(x_prompt, x_sample, c, cache_k, cache_v, state_mlstm_C, state_mlstm_n, state_mlstm_m, state_ret,
           c_ctx, w_ada, b_ada, norm_g, w_in, b_in, w_branch, w_out, attn_sink, ret_decay_logit,
           conv_w, conv_b, conv_ln_g, conv_ln_b, w_mlp1, w_mlp2):
    bsz, seq, _ = x_prompt.shape
    dbsz, dseq, _ = x_sample.shape
    past = cache_k.shape[2]

    c_rows = jnp.zeros((SUBLANES, D_MODEL), f32).at[0].set(c_ctx).at[1:1 + dbsz].set(c)
    mod4 = _ada(c_rows, w_ada, b_ada).reshape(DEPTH, SUBLANES, 6, D_MODEL)

    w_pk, b_pk = _pack_w_in(w_in, b_in)
    layers = []
    for l in range(DEPTH):
        layers.append({
            'ng': norm_g[l], 'w_in': w_pk, 'b_in': b_pk,
            'w_branch': w_branch[l].astype(bf16), 'w_out': w_out[l].astype(bf16),
            'w_mlp1': w_mlp1[l].astype(bf16), 'w_mlp2': w_mlp2[l].astype(bf16),
            'sink': jnp.broadcast_to(attn_sink[l][:, None], (A_HEADS, LANES)),
            'ret_dl': jnp.repeat(ret_decay_logit[l], DH, axis=-1).reshape(2, 1, HW),
            'ret_dr': jnp.broadcast_to(jnp.repeat(ret_decay_logit[l], DH, axis=-1)[:, :, None], (2, HW, LANES)),
            'conv_w': conv_w[l], 'conv_b': conv_b[l][None, :],
            'conv_ln_g': conv_ln_g[l][None, :], 'conv_ln_b': conv_ln_b[l][None, :],
        })

    xp = x_prompt.reshape(bsz * seq, D_MODEL)
    ctxs = []
    for l in range(DEPTH):
        xp, ctx = _layer(xp, bsz, seq, l, mod4, layers[l], None, lambda i: 0)
        ctxs.append(ctx)

    xs = x_sample.reshape(dbsz * dseq, D_MODEL)
    tiles_per_seq = dseq // ROW_TILE
    ck = cache_k.reshape(dbsz, DEPTH, past, A_KV_W)
    cv = cache_v.reshape(dbsz, DEPTH, past, A_KV_W)
    for l in range(DEPTH):
        cache = {'k': ck, 'v': cv, 'C': state_mlstm_C[:, l], 'n': state_mlstm_n[:, l],
                 'm': state_mlstm_m[:, l], 'S': state_ret[:, l]}
        xs, _ = _layer(xs, dbsz, dseq, l, mod4, layers[l], cache, lambda i: 1 + i // tiles_per_seq)

    stack = lambda j: jnp.stack([ctxs[l][j] for l in range(DEPTH)], axis=1)
    return (xp.reshape(bsz, seq, D_MODEL), xs.reshape(dbsz, dseq, D_MODEL),
            stack(0), stack(1), stack(2), stack(3), stack(4), stack(5))
```

```python
import functools

import numpy as np
import jax
import jax.numpy as jnp
from jax import lax
from jax.experimental import pallas as pl
from jax.experimental.pallas import tpu as pltpu

f32 = jnp.float32
bf16 = jnp.bfloat16

D_MODEL = 1024
DEPTH = 2
GRID_W = 64
BLOCK = 128
N_HEADS = 4
DH = 64
HW = N_HEADS * DH
A_HEADS = 8
A_KV_HEADS = 2
A_GROUP = A_HEADS // A_KV_HEADS
A_W = A_HEADS * DH
A_KV_W = A_KV_HEADS * DH
WINDOW = 128
ROPE_BASE = 10000.0
CONV_K = 31
D_FF = 4 * D_MODEL
N_BRANCH = 4
EPS = 1e-6
NEG = -1e30
LOG2E = 1.4426950408889634
LANES = 128
SUBLANES = 8

N_GATES = 4 * N_HEADS
PK_MQ = 0
PK_MG = PK_MQ + 4 * HW
PK_AT = PK_MG + 2 * LANES
PK_CV = PK_AT + A_W + 2 * A_KV_W
PK_RT = PK_CV + 2 * HW
PK_GT = PK_RT + 4 * HW
PK_W = PK_GT + N_BRANCH * D_MODEL
BR_OFFS = (0, HW, HW + A_W, 2 * HW + A_W, 3 * HW + A_W)

VMEM_LIMIT = 56 * 1024 * 1024
ROW_TILE = 512


def _sigmoid(x):
    return 0.5 * (jnp.tanh(0.5 * x) + 1.0)


def _log_sigmoid(x):
    return jnp.minimum(x, 0.0) - jnp.log(1.0 + jnp.exp(-jnp.abs(x)))


def _dot(a, b):
    return jnp.dot(a, b, preferred_element_type=f32)


def _dot_nt(a, b):
    return lax.dot_general(a, b, (((1,), (1,)), ((), ())), preferred_element_type=f32)


def _dot2_r(t, x):
    x1 = x.astype(bf16)
    x2 = (x - x1.astype(f32)).astype(bf16)
    r = _dot(t, jnp.concatenate([x1, x2], axis=1))
    return r[:, 0:x.shape[1]] + r[:, x.shape[1]:]


def _dot2_l(x, t):
    x1 = x.astype(bf16)
    x2 = (x - x1.astype(f32)).astype(bf16)
    r = _dot(jnp.concatenate([x1, x2], axis=0), t)
    return r[0:x.shape[0]] + r[x.shape[0]:]


def _const_spec(shape):
    nd = len(shape)
    return pl.BlockSpec(shape, lambda *_: (0,) * nd, pipeline_mode=pl.Buffered(1))


def _params(n_axes):
    return pltpu.CompilerParams(dimension_semantics=("arbitrary",) * n_axes, vmem_limit_bytes=VMEM_LIMIT)


def _ada_kernel(c_ref, w_ref, b_ref, o_ref):
    c = c_ref[...]
    s = (c * _sigmoid(c)).astype(bf16)
    o_ref[...] = _dot(s, w_ref[...].astype(bf16)) + b_ref[...]


def _ada(c_rows, w_ada, b_ada):
    tn = 1536
    n_out = 6 * D_MODEL
    return pl.pallas_call(
        _ada_kernel,
        grid=(DEPTH, n_out // tn),
        in_specs=[
            pl.BlockSpec((SUBLANES, D_MODEL), lambda l, j: (0, 0)),
            pl.BlockSpec((None, D_MODEL, tn), lambda l, j: (l, 0, j)),
            pl.BlockSpec((None, 1, tn), lambda l, j: (l, 0, j)),
        ],
        out_specs=pl.BlockSpec((None, SUBLANES, tn), lambda l, j: (l, 0, j)),
        out_shape=jax.ShapeDtypeStruct((DEPTH, SUBLANES, n_out), f32),
        compiler_params=_params(2),
        name="ada",
    )(c_rows, w_ada, b_ada.reshape(DEPTH, 1, n_out))


def _inproj_kernel(x_ref, mod_ref, ng_ref, w_ref, b_ref, mq_ref, mg_ref, at_ref, cv_ref, rt_ref, gt_ref):
    x = x_ref[...]
    y = x * lax.rsqrt(jnp.mean(x * x, axis=-1, keepdims=True) + EPS) * ng_ref[0:1, :]
    h = (y * (1.0 + mod_ref[1:2, :]) + mod_ref[0:1, :]).astype(bf16)
    chunk = 512
    for ref, base in ((mq_ref, PK_MQ), (mg_ref, PK_MG), (at_ref, PK_AT), (cv_ref, PK_CV), (rt_ref, PK_RT),
                      (gt_ref, PK_GT)):
        total = ref.shape[-1]
        for c in range(0, total, chunk):
            w = min(chunk, total - c)
            r = _dot_nt(h, w_ref[base + c:base + c + w, :]) + b_ref[:, base + c:base + c + w]
            ref[:, c:c + w] = r.astype(ref.dtype)


def _inproj(x2d, mod4, ng, w_pk, b_pk, layer, row_of_tile):
    m = x2d.shape[0]
    tm = ROW_TILE
    widths = (4 * HW, 2 * LANES, A_W + 2 * A_KV_W, 2 * HW, 4 * HW, N_BRANCH * D_MODEL)
    dtypes = (bf16, f32, f32, bf16, bf16, bf16)
    return pl.pallas_call(
        _inproj_kernel,
        grid=(m // tm,),
        in_specs=[
            pl.BlockSpec((tm, D_MODEL), lambda i: (i, 0)),
            pl.BlockSpec((None, None, 6, D_MODEL), lambda i: (layer, row_of_tile(i), 0, 0)),
            _const_spec((4, D_MODEL)),
            pl.BlockSpec((None, PK_W, D_MODEL), lambda i: (layer, 0, 0), pipeline_mode=pl.Buffered(1)),
            pl.BlockSpec((None, 1, PK_W), lambda i: (layer, 0, 0), pipeline_mode=pl.Buffered(1)),
        ],
        out_specs=[pl.BlockSpec((tm, w), lambda i: (i, 0)) for w in widths],
        out_shape=[jax.ShapeDtypeStruct((m, w), dt) for w, dt in zip(widths, dtypes)],
        compiler_params=_params(1),
        name="inproj",
    )(x2d, mod4, ng, w_pk, b_pk)


def _head_lane_masks():
    lane_head = lax.broadcasted_iota(jnp.int32, (1, HW), 1) // DH
    return [lane_head == h for h in range(N_HEADS)]


def _run_interleaved(gens):
    results = [None] * len(gens)
    active = list(range(len(gens)))
    while active:
        for idx in list(active):
            try:
                next(gens[idx])
            except StopIteration as stop:
                results[idx] = stop.value
                active.remove(idx)
    return results


PAIRS = N_HEADS // 2


def _pair_diag_mask():
    r = (lax.broadcasted_iota(jnp.int32, (HW, LANES), 0) // DH) % 2
    c = lax.broadcasted_iota(jnp.int32, (HW, LANES), 1) // DH
    return r == c


def _expand_state(x):
    return jnp.where(_pair_diag_mask(), x + pltpu.roll(x, DH, 1), 0.0)


def _compact_state(c):
    return (c + pltpu.roll(c, DH, 1))[:, 0:DH]


def _pair_qk(q, kt):
    zeros = jnp.zeros((DH, BLOCK), kt.dtype)
    outs = []
    for j in range(PAIRS):
        rows = kt[LANES * j:LANES * (j + 1)]
        blk = jnp.concatenate([jnp.concatenate([rows[0:DH], zeros], axis=0),
                               jnp.concatenate([zeros, rows[DH:2 * DH]], axis=0)], axis=1)
        outs.append(_dot(q[:, LANES * j:LANES * (j + 1)], blk))
    return jnp.concatenate(outs, axis=1)


def _pair_pv(p, v):
    low = lax.broadcasted_iota(jnp.int32, (1, LANES), 1) < DH
    outs = []
    for j in range(PAIRS):
        vp = v[:, LANES * j:LANES * (j + 1)]
        zero = jnp.zeros_like(vp)
        blk = jnp.concatenate([jnp.where(low, vp, zero), jnp.where(low, zero, vp)], axis=0)
        outs.append(_dot(p[:, 2 * BLOCK * j:2 * BLOCK * (j + 1)], blk))
    return jnp.concatenate(outs, axis=1)


def _pair_state_read(q, c):
    return jnp.concatenate([_dot(q[:, LANES * j:LANES * (j + 1)], c[LANES * j:LANES * (j + 1)])
                            for j in range(PAIRS)], axis=1)


def _pair_state_update(a_t, v):
    upd = jnp.concatenate([_dot(a_t[LANES * j:LANES * (j + 1)], v[:, LANES * j:LANES * (j + 1)])
                           for j in range(PAIRS)], axis=0)
    return jnp.where(_pair_diag_mask(), upd, 0.0)


def _mlstm_kernel(x_ref, g_ref, c0_ref, n0_ref, m0_ref, y_ref, cn_ref, nn_ref, mn_ref,
                  h_s, c_s, n_s, m_s, *, nc, nb):
    wide = N_HEADS * BLOCK
    row = lax.broadcasted_iota(jnp.int32, (BLOCK, BLOCK), 0)
    col = lax.broadcasted_iota(jnp.int32, (BLOCK, BLOCK), 1)
    tsum = (jnp.where(row >= col, 1.0, 0.0).astype(bf16), jnp.where(row <= col, 1.0, 0.0).astype(bf16))
    roww = lax.broadcasted_iota(jnp.int32, (BLOCK, wide), 0)
    colw = lax.broadcasted_iota(jnp.int32, (BLOCK, wide), 1) & (BLOCK - 1)
    tri4 = (roww >= colw, roww <= colw)
    masks = _head_lane_masks()
    lane = lax.broadcasted_iota(jnp.int32, (1, LANES), 1)
    row8 = lax.broadcasted_iota(jnp.int32, (SUBLANES, BLOCK), 0)

    def onehot(shape, row_expr, col_expr):
        r = lax.broadcasted_iota(jnp.int32, shape, 0)
        c = lax.broadcasted_iota(jnp.int32, shape, 1)
        return jnp.where(row_expr(r) == col_expr(c), 1.0, 0.0).astype(bf16)

    g0s = (0, 2 * N_HEADS)
    valid = tuple((lane >= g0) & (lane < g0 + N_HEADS) for g0 in g0s)
    hexp = tuple(onehot((LANES, HW), lambda r: r, lambda c, g0=g0: c // DH + g0) for g0 in g0s)
    hselg = tuple(onehot((HW, LANES), lambda r, g0=g0: r // DH + g0, lambda c: c) for g0 in g0s)

    chains = [(bb, d) for bb in range(nb) for d in range(2)]
    for bb, d in chains:
        c_s[bb, d] = _expand_state(c0_ref[bb, d])
        n_s[bb, d] = n0_ref[bb, d]
        m_s[bb, d] = m0_ref[bb, d]

    def load(bb, d, c):
        r0 = pl.multiple_of(c * BLOCK, BLOCK)
        return (r0, x_ref[bb, pl.ds(r0, BLOCK), 0:3 * HW], g_ref[bb, pl.ds(r0, BLOCK), :],
                c_s[bb, d], n_s[bb, d], m_s[bb, d])

    def compute(d, blk, g, cmat, nrow, m_row):
        g0 = g0s[d]
        q = blk[:, 0:HW]
        k = blk[:, HW:2 * HW] * (DH ** -0.5)
        v = blk[:, 2 * HW:3 * HW]
        ga = g[:, 0:LANES]
        gb = g[:, LANES:2 * LANES]
        bc = _dot2_r(tsum[d], _log_sigmoid(gb))
        yield
        beta = ga - bc
        beta_t = beta.T
        yield
        cmx = beta
        sh = 1
        while sh < BLOCK:
            if d == 0:
                cmx = jnp.maximum(cmx, jnp.where(row >= sh, pltpu.roll(cmx, sh, 0), NEG))
            else:
                cmx = jnp.maximum(cmx, jnp.where(row < BLOCK - sh, pltpu.roll(cmx, BLOCK - sh, 0), NEG))
            sh *= 2
        mx = jnp.maximum(m_row, cmx)
        alpha = jnp.where(valid[d], -mx, 0.0)
        w_int = jnp.where(valid[d], jnp.exp(m_row - mx), 0.0)
        e_nb = jnp.exp(alpha - bc)
        yield
        z = jnp.concatenate([jnp.broadcast_to(alpha[:, g0 + h:g0 + h + 1], (BLOCK, BLOCK)) for h in range(N_HEADS)],
                            axis=1)
        beta_w = jnp.concatenate([beta_t[g0 + h:g0 + h + 1, :] for h in range(N_HEADS)], axis=1)
        ktf = k.astype(f32).T
        s_wide = _pair_qk(q, ktf.astype(bf16))
        yield
        p = s_wide * jnp.exp(jnp.where(tri4[d], z + beta_w, NEG))
        num = _pair_pv(p.astype(bf16), v)
        yield
        den = jnp.zeros((BLOCK, LANES), f32)
        for h in range(N_HEADS):
            den = jnp.where(lane == g0 + h, jnp.sum(p[:, BLOCK * h:BLOCK * (h + 1)], axis=1, keepdims=True), den)
        qc = _pair_state_read(q, cmat.astype(bf16))
        qn = _dot(q * nrow.astype(bf16), hselg[d])
        yield
        den = den + w_int * qn
        inv = jnp.where(valid[d], 1.0 / jnp.maximum(jnp.abs(den), e_nb), 0.0)
        fac = _dot(jnp.concatenate([w_int, inv], axis=0).astype(bf16), hexp[d])
        yield
        hout = (num + fac[0:BLOCK] * qc) * fac[BLOCK:2 * BLOCK]
        last = BLOCK - 1 if d == 0 else 0
        a_last = alpha[last:last + 1, :]
        m_new = jnp.where(valid[d], bc[last:last + 1, :] - a_last, 0.0)
        dec = jnp.exp(m_row + a_last)
        wr = jnp.zeros((SUBLANES, BLOCK), f32)
        kws = []
        for h in range(N_HEADS):
            w_row = jnp.exp(beta_t[g0 + h:g0 + h + 1, :] + a_last[:, g0 + h:g0 + h + 1])
            kws.append(ktf[DH * h:DH * (h + 1), :] * w_row)
            wr = jnp.where(row8 == h, w_row, wr)
        upd = _pair_state_update(jnp.concatenate(kws, axis=0).astype(bf16), v)
        wk = _dot(wr.astype(bf16), k)
        yield
        nnew = jnp.zeros((1, HW), f32)
        cnew = []
        for h in range(N_HEADS):
            rs = slice(DH * h, DH * (h + 1))
            dh = dec[:, g0 + h:g0 + h + 1]
            cnew.append(dh * cmat[rs, :] + upd[rs, :])
            nnew = jnp.where(masks[h], dh * nrow + wk[h:h + 1, :], nnew)
        return hout, jnp.concatenate(cnew, axis=0), nnew, m_new

    def step(i, second):
        loaded = [load(bb, d, i if d == 0 else nc - 1 - i) for bb, d in chains]
        if second:
            other = [h_s[bb, pl.ds(l[0], BLOCK), :] for (bb, d), l in zip(chains, loaded)]
            ogate = [x_ref[bb, pl.ds(l[0], BLOCK), 3 * HW:4 * HW] for (bb, d), l in zip(chains, loaded)]
        outs = _run_interleaved([compute(d, *l[1:]) for (bb, d), l in zip(chains, loaded)])
        for idx, ((bb, d), l) in enumerate(zip(chains, loaded)):
            hout, cnew, nnew, m_new = outs[idx]
            if second:
                y = _sigmoid(ogate[idx].astype(f32)) * (other[idx] + hout)
                y_ref[bb, pl.ds(l[0], BLOCK), :] = y.astype(y_ref.dtype)
            else:
                h_s[bb, pl.ds(l[0], BLOCK), :] = hout
            c_s[bb, d] = cnew
            n_s[bb, d] = nnew
            m_s[bb, d] = m_new

    def first_half(i, carry):
        step(i, False)
        return carry

    def second_half(i, carry):
        step(i, True)
        return carry

    lax.fori_loop(0, nc // 2, first_half, 0)
    lax.fori_loop(nc // 2, nc, second_half, 0)

    for bb, d in chains:
        cn_ref[bb, d] = _compact_state(c_s[bb, d])
        nn_ref[bb, d] = n_s[bb, d]
        mn_ref[bb, d] = m_s[bb, d]


SCAN_SEQS = 4


def _mlstm(mq, mg, c0, n0, m0, bsz, n):
    nc = n // BLOCK
    nb = min(SCAN_SEQS, bsz)
    assert nc % 2 == 0 and bsz % nb == 0
    spec = lambda shape: pl.BlockSpec((nb,) + shape, lambda b: (b,) + (0,) * len(shape))
    seq_in = (lambda shape: pl.BlockSpec((nb,) + shape, lambda b: (b,) + (0,) * len(shape),
                                         pipeline_mode=pl.Buffered(1))) if bsz == nb else spec
    return pl.pallas_call(
        functools.partial(_mlstm_kernel, nc=nc, nb=nb),
        grid=(bsz // nb,),
        in_specs=[
            seq_in((n, 4 * HW)),
            seq_in((n, 2 * LANES)),
            spec((2, HW, LANES)),
            spec((2, 1, HW)),
            spec((2, 1, LANES)),
        ],
        out_specs=[
            spec((n, HW)),
            spec((2, HW, DH)),
            spec((2, 1, HW)),
            spec((2, 1, LANES)),
        ],
        out_shape=[
            jax.ShapeDtypeStruct((bsz, n, HW), bf16),
            jax.ShapeDtypeStruct((bsz, 2, HW, DH), f32),
            jax.ShapeDtypeStruct((bsz, 2, 1, HW), f32),
            jax.ShapeDtypeStruct((bsz, 2, 1, LANES), f32),
        ],
        scratch_shapes=[
            pltpu.VMEM((nb, n, HW), f32),
            pltpu.VMEM((nb, 2, HW, LANES), f32),
            pltpu.VMEM((nb, 2, 1, HW), f32),
            pltpu.VMEM((nb, 2, 1, LANES), f32),
        ],
        compiler_params=_params(1),
        name="mlstm",
    )(mq.reshape(bsz, n, 4 * HW), mg.reshape(bsz, n, 2 * LANES), c0, n0, m0)


def _ret_kernel(x_ref, s0_ref, dl_ref, dr_ref, y_ref, sn_ref, o_s, s_s, qd_s, kd_s, cd_s, dm_s, *, nc, nb):
    avg = jnp.where(lax.broadcasted_iota(jnp.int32, (HW, HW), 0) // DH ==
                    lax.broadcasted_iota(jnp.int32, (HW, HW), 1) // DH, 1.0 / DH, 0.0).astype(bf16)

    @pl.when(pl.program_id(0) == 0)
    def _decay_tables():
        rowf = lax.broadcasted_iota(jnp.int32, (BLOCK, HW), 0).astype(f32)
        rel = (lax.broadcasted_iota(jnp.int32, (BLOCK, BLOCK), 0) -
               lax.broadcasted_iota(jnp.int32, (BLOCK, BLOCK), 1)).astype(f32)
        for d in range(2):
            lg = _log_sigmoid(dl_ref[d])
            if d == 0:
                qd_s[d] = jnp.exp((rowf + 1.0) * lg)
                kd_s[d] = jnp.exp((BLOCK - 1.0 - rowf) * lg)
                reld = rel
            else:
                qd_s[d] = jnp.exp((BLOCK - rowf) * lg)
                kd_s[d] = jnp.exp(rowf * lg)
                reld = -rel
            cd_s[d] = jnp.exp(float(BLOCK) * _log_sigmoid(dr_ref[d]))
            for h in range(N_HEADS):
                lgh = lg[:, DH * h:DH * h + 1]
                dm_s[d, :, BLOCK * h:BLOCK * (h + 1)] = jnp.where(reld >= 0.0,
                                                                  jnp.exp(jnp.maximum(reld, 0.0) * lgh), 0.0)

    chains = [(bb, d) for bb in range(nb) for d in range(2)]
    for bb, d in chains:
        s_s[bb, d] = _expand_state(s0_ref[bb, d])

    def compute(d, blk, sm, other, gate):
        q = blk[:, 0:HW]
        k = blk[:, HW:2 * HW] * (DH ** -0.5)
        v = blk[:, 2 * HW:3 * HW]
        kf = k.astype(f32)
        p = _pair_qk(q, kf.T.astype(bf16)) * dm_s[d]
        yield
        o = _pair_pv(p.astype(bf16), v) + qd_s[d] * _pair_state_read(q, sm.astype(bf16))
        yield
        kdt = (kf * kd_s[d]).T.astype(bf16)
        snew = cd_s[d] * sm + _pair_state_update(kdt, v)
        yield
        if other is None:
            return o, snew
        o = o + other
        gate = gate.astype(f32)
        dev = o - _dot2_l(o, avg)
        yield
        var = _dot((dev * dev).astype(bf16), avg)
        yield
        return gate * _sigmoid(gate) * (dev * lax.rsqrt(var + EPS)), snew

    def step(i, second):
        r0s = [pl.multiple_of((i if d == 0 else nc - 1 - i) * BLOCK, BLOCK) for bb, d in chains]
        loaded = [(x_ref[bb, pl.ds(r0, BLOCK), 0:3 * HW], s_s[bb, d]) for (bb, d), r0 in zip(chains, r0s)]
        if second:
            other = [o_s[bb, pl.ds(r0, BLOCK), :] for (bb, d), r0 in zip(chains, r0s)]
            gates = [x_ref[bb, pl.ds(r0, BLOCK), 3 * HW:4 * HW] for (bb, d), r0 in zip(chains, r0s)]
        else:
            other = gates = [None] * len(chains)
        outs = _run_interleaved([compute(d, *l, other[idx], gates[idx])
                                 for idx, ((bb, d), l) in enumerate(zip(chains, loaded))])
        for idx, ((bb, d), r0) in enumerate(zip(chains, r0s)):
            o, snew = outs[idx]
            if second:
                y_ref[bb, pl.ds(r0, BLOCK), :] = o.astype(y_ref.dtype)
            else:
                o_s[bb, pl.ds(r0, BLOCK), :] = o
            s_s[bb, d] = snew

    def first_half(i, carry):
        step(i, False)
        return carry

    def second_half(i, carry):
        step(i, True)
        return carry

    lax.fori_loop(0, nc // 2, first_half, 0)
    lax.fori_loop(nc // 2, nc, second_half, 0)

    for bb, d in chains:
        sn_ref[bb, d] = _compact_state(s_s[bb, d])


def _retention(rt, s0, dl, dr, bsz, n):
    nc = n // BLOCK
    nb = min(SCAN_SEQS, bsz)
    assert nc % 2 == 0 and bsz % nb == 0
    spec = lambda shape: pl.BlockSpec((nb,) + shape, lambda b: (b,) + (0,) * len(shape))
    seq_in = (lambda shape: pl.BlockSpec((nb,) + shape, lambda b: (b,) + (0,) * len(shape),
                                         pipeline_mode=pl.Buffered(1))) if bsz == nb else spec
    return pl.pallas_call(
        functools.partial(_ret_kernel, nc=nc, nb=nb),
        grid=(bsz // nb,),
        in_specs=[
            seq_in((n, 4 * HW)),
            spec((2, HW, LANES)),
            _const_spec((2, 1, HW)),
            _const_spec((2, HW, LANES)),
        ],
        out_specs=[
            spec((n, HW)),
            spec((2, HW, DH)),
        ],
        out_shape=[
            jax.ShapeDtypeStruct((bsz, n, HW), bf16),
            jax.ShapeDtypeStruct((bsz, 2, HW, DH), f32),
        ],
        scratch_shapes=[
            pltpu.VMEM((nb, n, HW), f32),
            pltpu.VMEM((nb, 2, HW, LANES), f32),
            pltpu.VMEM((2, BLOCK, HW), f32),
            pltpu.VMEM((2, BLOCK, HW), f32),
            pltpu.VMEM((2, HW, LANES), f32),
            pltpu.VMEM((2, BLOCK, N_HEADS * BLOCK), f32),
        ],
        compiler_params=_params(1),
        name="retention",
    )(rt.reshape(bsz, n, 4 * HW), s0, dl, dr)


def _attn_kernel(q_ref, k_ref, v_ref, ck_ref, cv_ref, cos_ref, sin_ref, sk_ref, o_ref, kt_s, vt_s,
                 *, n_ctx, n, band, qb, qps):
    i = pl.program_id(1)
    lane = lax.broadcasted_iota(jnp.int32, (1, LANES), 1)
    lo = lane < DH
    first = (lane % (DH // 2)) < (DH // 4)
    gmasks = _head_lane_masks()
    n_cblk = n_ctx // BLOCK

    def tile4(x):
        xr = pltpu.roll(x, DH, 1)
        a2 = jnp.where(lo, x, xr).astype(bf16)
        b2 = jnp.where(lo, xr, x).astype(bf16)
        return jnp.concatenate([a2, a2], axis=1), jnp.concatenate([b2, b2], axis=1)

    def rope(x, cos, sin):
        sw = jnp.where(first, pltpu.roll(x, LANES - DH // 4, 1), pltpu.roll(x, DH // 4, 1))
        return x * cos + sw * sin

    def put_block(blk, r0, kk, vv):
        k0, k1 = tile4(kk)
        kt_s[0, pl.ds(r0, BLOCK), :] = k0
        kt_s[1, pl.ds(r0, BLOCK), :] = k1
        vt = vv.T
        for j in range(A_KV_HEADS):
            vj = vt[DH * j:DH * (j + 1)]
            vt_s[j, blk] = jnp.concatenate([vj, vj], axis=0).astype(bf16)

    @pl.when(i == 0)
    def _prepare_keys():
        for c in range(n_cblk):
            put_block(c, c * BLOCK, ck_ref[c * BLOCK:(c + 1) * BLOCK, :], cv_ref[c * BLOCK:(c + 1) * BLOCK, :])
        if band:
            nb = n // BLOCK
            for blk in (n_cblk, n_cblk + 1 + nb):
                for j in range(A_KV_HEADS):
                    kt_s[j, blk * BLOCK:(blk + 1) * BLOCK, :] = jnp.zeros((BLOCK, HW), bf16)
                    vt_s[j, blk] = jnp.zeros((BLOCK, BLOCK), bf16)

            def body(c, carry):
                r0 = pl.multiple_of(c * BLOCK, BLOCK)
                kk = rope(k_ref[pl.ds(r0, BLOCK), :], cos_ref[pl.ds(r0, BLOCK), :], sin_ref[pl.ds(r0, BLOCK), :])
                put_block(n_cblk + 1 + c, pl.multiple_of(n_ctx + BLOCK + r0, BLOCK), kk, v_ref[pl.ds(r0, BLOCK), :])
                return carry

            lax.fori_loop(0, nb, body, 0, unroll=2)

    row_lo = lax.broadcasted_iota(jnp.int32, (2 * DH, qb), 0) < DH

    def query_block(u):
        q = q_ref[qb * u:qb * (u + 1), :]
        if not band:
            return q * (DH ** -0.5 * LOG2E), None, None
        iq = i * qps + u
        q0 = pl.multiple_of(iq * BLOCK, BLOCK)
        cos = cos_ref[pl.ds(q0, BLOCK), :]
        sin = sin_ref[pl.ds(q0, BLOCK), :]
        q = jnp.concatenate([rope(q[:, LANES * t:LANES * (t + 1)], cos, sin) for t in range(A_W // LANES)], axis=1)
        kj = lax.broadcasted_iota(jnp.int32, (3 * BLOCK, BLOCK), 0)
        qi = lax.broadcasted_iota(jnp.int32, (3 * BLOCK, BLOCK), 1)
        tok = kj + (iq - 1) * BLOCK
        ok = (kj >= qi) & (kj <= qi + 2 * WINDOW) & (tok >= 0) & (tok < n)
        bias = jnp.where(ok, 0.0, NEG)
        return (q * (DH ** -0.5 * LOG2E), jnp.concatenate([bias] * A_GROUP, axis=1),
                (pl.multiple_of(n_ctx + iq * BLOCK, BLOCK), n_cblk + iq))

    def kv_head(j, q, bias4, pos):
        q256 = q[:, HW * j:HW * (j + 1)]
        qs = jnp.concatenate([jnp.where(gmasks[g], q256, 0.0) for g in range(A_GROUP)], axis=0).astype(bf16)
        sink = jnp.concatenate(
            [jnp.broadcast_to(sk_ref[A_GROUP * j + g:A_GROUP * j + g + 1, 0:1], (1, qb)) for g in range(A_GROUP)],
            axis=1) * LOG2E
        s1 = _dot_nt(kt_s[j, 0:n_ctx, :], qs)
        yield
        m = jnp.maximum(jnp.max(s1, axis=0, keepdims=True), sink)
        if band:
            s2 = _dot_nt(kt_s[j, pl.ds(pos[0], 3 * BLOCK), :], qs) + bias4
            vband = vt_s[j, pl.ds(pos[1], 3)]
            yield
            m = jnp.maximum(m, jnp.max(s2, axis=0, keepdims=True))
        p1 = jnp.exp2(s1 - m)
        den = jnp.sum(p1, axis=0, keepdims=True) + jnp.exp2(sink - m)
        vctx = jnp.concatenate([vt_s[j, c] for c in range(n_cblk)], axis=1)
        acc = _dot(vctx, p1.astype(bf16))
        yield
        if band:
            p2 = jnp.exp2(s2 - m)
            den = den + jnp.sum(p2, axis=0, keepdims=True)
            acc = acc + _dot(jnp.concatenate([vband[t] for t in range(3)], axis=1), p2.astype(bf16))
            yield
        acc = acc * (1.0 / den)
        pairs = []
        for g in range(0, A_GROUP, 2):
            pair = jnp.where(row_lo, acc[:, qb * g:qb * (g + 1)], acc[:, qb * (g + 1):qb * (g + 2)])
            pairs.append(pair.T)
        return pairs

    blocks = [query_block(u) for u in range(qps)]
    outs = _run_interleaved([kv_head(j, *blocks[u]) for u in range(qps) for j in range(A_KV_HEADS)])
    for u in range(qps):
        tiles = [t for pairs in outs[A_KV_HEADS * u:A_KV_HEADS * (u + 1)] for t in pairs]
        o_ref[qb * u:qb * (u + 1), :] = jnp.concatenate(tiles, axis=1).astype(o_ref.dtype)


def _attention(at, ck, cv, ck_spec, cv_spec, cos, sin, sk, bsz, n, n_ctx, band):
    qb = BLOCK if band else min(n, 2 * BLOCK)
    qps = 2 if band else 1
    nq = n // (qb * qps)
    s_tot = n_ctx + (n + 2 * BLOCK if band else 0)
    at3 = at.reshape(bsz, n, A_W + 2 * A_KV_W)
    kcol = A_W // A_KV_W
    return pl.pallas_call(
        functools.partial(_attn_kernel, n_ctx=n_ctx, n=n, band=band, qb=qb, qps=qps),
        grid=(bsz, nq),
        in_specs=[
            pl.BlockSpec((None, qb * qps, A_W), lambda b, i: (b, i, 0)),
            pl.BlockSpec((None, n, A_KV_W), lambda b, i: (b, 0, kcol)),
            pl.BlockSpec((None, n, A_KV_W), lambda b, i: (b, 0, kcol + 1)),
            ck_spec,
            cv_spec,
            _const_spec(cos.shape),
            _const_spec(sin.shape),
            _const_spec((A_HEADS, LANES)),
        ],
        out_specs=pl.BlockSpec((None, qb * qps, A_W), lambda b, i: (b, i, 0)),
        out_shape=jax.ShapeDtypeStruct((bsz, n, A_W), bf16),
        scratch_shapes=[
            pltpu.VMEM((A_KV_HEADS, s_tot, HW), bf16),
            pltpu.VMEM((A_KV_HEADS, s_tot // BLOCK, BLOCK, BLOCK), bf16),
        ],
        compiler_params=_params(2),
        name="attention",
    )(at3, at3, at3, ck, cv, cos, sin, sk)


def _rope_tables(n):
    tok = np.arange(n)
    pos = np.stack([tok // GRID_W, tok % GRID_W], axis=1).astype(np.float32)
    quarter = DH // 4
    freqs = np.power(np.float32(ROPE_BASE), -np.arange(quarter, dtype=np.float32) / np.float32(quarter))
    lane = np.arange(LANES) % DH
    axis = lane // (DH // 2)
    fidx = lane % quarter
    sign = np.where((lane % (DH // 2)) < quarter, -1.0, 1.0)
    ang = (pos[:, axis] * freqs.astype(np.float32)[fidx][None, :]).astype(np.float32).astype(np.float64)
    return jnp.asarray(np.cos(ang), f32), jnp.asarray(np.sin(ang) * sign[None, :], f32)


CONV_PAD = 16


def _conv_kernel(u_ref, w_ref, b_ref, lg_ref, lb_ref, y_ref, z_s, *, n):
    nc = n // BLOCK
    z_s[0:CONV_PAD, :] = jnp.zeros((CONV_PAD, HW), f32)
    z_s[CONV_PAD + n:2 * CONV_PAD + n, :] = jnp.zeros((CONV_PAD, HW), f32)

    def glu(c, carry):
        r0 = pl.multiple_of(c * BLOCK, BLOCK)
        blk = u_ref[pl.ds(r0, BLOCK), :].astype(f32)
        z_s[pl.ds(pl.multiple_of(r0 + CONV_PAD, SUBLANES), BLOCK), :] = blk[:, 0:HW] * _sigmoid(blk[:, HW:2 * HW])
        return carry

    lax.fori_loop(0, nc, glu, 0)

    first = CONV_PAD - CONV_K // 2
    span = BLOCK + 2 * CONV_PAD

    def tile(r0):
        win = z_s[pl.ds(r0, span), :]
        acc = jnp.zeros((BLOCK, HW), f32)
        for r in range(SUBLANES):
            taps = [kk for kk in range(CONV_K) if (first + kk) % SUBLANES == r]
            if not taps:
                continue
            shifted = win if r == 0 else pltpu.roll(win, span - r, 0)
            for kk in taps:
                a = (first + kk) // SUBLANES * SUBLANES
                acc = acc + shifted[a:a + BLOCK, :] * w_ref[kk:kk + 1, :]
            yield
        zc = acc + b_ref[...]
        mu = jnp.mean(zc, axis=-1, keepdims=True)
        dev = zc - mu
        var = jnp.mean(dev * dev, axis=-1, keepdims=True)
        t = dev * lax.rsqrt(var + EPS) * lg_ref[...] + lb_ref[...]
        return (t * _sigmoid(t)).astype(y_ref.dtype)

    def two_tiles(c, carry):
        r0s = [pl.multiple_of((2 * c + u) * BLOCK, BLOCK) for u in range(2)]
        outs = _run_interleaved([tile(r0) for r0 in r0s])
        for r0, out in zip(r0s, outs):
            y_ref[pl.ds(r0, BLOCK), :] = out
        return carry

    lax.fori_loop(0, nc // 2, two_tiles, 0)


def _conv(cu, w, b, lg, lb, bsz, n):
    return pl.pallas_call(
        functools.partial(_conv_kernel, n=n),
        grid=(bsz,),
        in_specs=[
            pl.BlockSpec((None, n, 2 * HW), lambda bb: (bb, 0, 0)),
            _const_spec((CONV_K, HW)),
            _const_spec((1, HW)),
            _const_spec((1, HW)),
            _const_spec((1, HW)),
        ],
        out_specs=pl.BlockSpec((None, n, HW), lambda bb: (bb, 0, 0)),
        out_shape=jax.ShapeDtypeStruct((bsz, n, HW), bf16),
        scratch_shapes=[pltpu.VMEM((n + 2 * CONV_PAD, HW), f32)],
        compiler_params=_params(1),
        name="conv",
    )(cu.reshape(bsz, n, 2 * HW), w, b, lg, lb)


def _post_kernel(x_ref, ym_ref, ya_ref, yc_ref, yr_ref, gt_ref, mod_ref, ng_ref, wbr_ref, wout_ref, w1_ref, w2_ref,
                 o_ref):
    def rms(a):
        return a * lax.rsqrt(jnp.mean(a * a, axis=-1, keepdims=True) + EPS)

    acc = None
    for i, y_ref in enumerate((ym_ref, ya_ref, yc_ref, yr_ref)):
        br = _dot(y_ref[...], wbr_ref[BR_OFFS[i]:BR_OFFS[i + 1], :]).astype(bf16)
        term = jnp.tanh(gt_ref[:, D_MODEL * i:D_MODEL * (i + 1)] * 0.5) * br + br
        acc = term if acc is None else acc + term
    mix = _dot(acc * 0.5, wout_ref[...])
    x = x_ref[...] + mod_ref[2:3, :] * (rms(mix) * ng_ref[1:2, :])
    h = (rms(x) * ng_ref[2:3, :] * (1.0 + mod_ref[4:5, :]) + mod_ref[3:4, :]).astype(bf16)
    chunk = 512
    f = jnp.zeros(x.shape, f32)
    for c in range(0, D_FF, chunk):
        a = jnp.maximum(_dot(h, w1_ref[:, c:c + chunk]), 0.0)
        f = f + _dot((a * a).astype(bf16), w2_ref[c:c + chunk, :])
    o_ref[...] = x + mod_ref[5:6, :] * (rms(f) * ng_ref[3:4, :])


def _post(x2d, ym, ya, yc, yr, gt, mod4, lw, layer, row_of_tile):
    m = x2d.shape[0]
    tm = ROW_TILE
    rows = lambda w: pl.BlockSpec((tm, w), lambda i: (i, 0))
    return pl.pallas_call(
        _post_kernel,
        grid=(m // tm,),
        in_specs=[
            rows(D_MODEL), rows(HW), rows(A_W), rows(HW), rows(HW), rows(N_BRANCH * D_MODEL),
            pl.BlockSpec((None, None, 6, D_MODEL), lambda i: (layer, row_of_tile(i), 0, 0)),
            _const_spec((4, D_MODEL)),
            _const_spec((BR_OFFS[-1], D_MODEL)),
            _const_spec((D_MODEL, D_MODEL)),
            _const_spec((D_MODEL, D_FF)),
            _const_spec((D_FF, D_MODEL)),
        ],
        out_specs=rows(D_MODEL),
        out_shape=jax.ShapeDtypeStruct((m, D_MODEL), f32),
        compiler_params=_params(1),
        name="post",
    )(x2d, ym.reshape(m, HW), ya.reshape(m, A_W), yc.reshape(m, HW), yr.reshape(m, HW), gt, mod4, lw['ng'],
      lw['w_branch'], lw['w_out'], lw['w_mlp1'], lw['w_mlp2'])


PACK_ROWS = 512


def _pack_kernel(wt_ref, o_ref):
    s = pl.program_id(1)
    gate_step = PK_MG // PACK_ROWS

    @pl.when(s != gate_step)
    def _plain():
        o_ref[...] = wt_ref[0].astype(bf16)

    @pl.when(s == gate_step)
    def _gates():
        row = lax.broadcasted_iota(jnp.int32, (LANES, 1), 0)
        gates = wt_ref[0, 0:LANES, :]
        o_ref[0:LANES, :] = jnp.where(row < N_GATES, gates, 0.0).astype(bf16)
        shifted = pltpu.roll(gates, LANES - N_HEADS, 0)
        o_ref[LANES:2 * LANES, :] = jnp.where(row < N_GATES - N_HEADS, shifted, 0.0).astype(bf16)
        o_ref[2 * LANES:PACK_ROWS, :] = wt_ref[0, N_GATES:N_GATES + PACK_ROWS - 2 * LANES, :].astype(bf16)


def _pack_w_in(w_in, b_in):
    gate_step = PK_MG // PACK_ROWS
    shift = 2 * LANES - N_GATES
    assert PK_MG % PACK_ROWS == 0 and PK_W % PACK_ROWS == 0 and PACK_ROWS >= 2 * LANES

    def src_row(s):
        return pl.multiple_of(jnp.where(s <= gate_step, PACK_ROWS * s, PACK_ROWS * s - shift), N_GATES)

    w_pk = pl.pallas_call(
        _pack_kernel,
        grid=(DEPTH, PK_W // PACK_ROWS),
        in_specs=[pl.BlockSpec((pl.Element(1), pl.Element(PACK_ROWS), pl.Element(D_MODEL)),
                               lambda l, s: (l, src_row(s), 0))],
        out_specs=pl.BlockSpec((None, PACK_ROWS, D_MODEL), lambda l, s: (l, s, 0)),
        out_shape=jax.ShapeDtypeStruct((DEPTH, PK_W, D_MODEL), bf16),
        compiler_params=_params(2),
        name="pack_w_in",
    )(jnp.swapaxes(w_in, 1, 2))
    g0 = PK_MG
    pad = lambda cols: jnp.zeros((DEPTH, cols), f32)
    b_pk = jnp.concatenate(
        [b_in[:, :g0 + N_GATES], pad(LANES - N_GATES), b_in[:, g0 + N_HEADS:g0 + N_GATES],
         pad(LANES - N_GATES + N_HEADS), b_in[:, g0 + N_GATES:]], axis=-1)
    return w_pk, b_pk[:, None, :]


def _stack_state(s):
    bsz = s.shape[0]
    s = s.reshape(bsz, 2, HW, DH)
    return jnp.pad(s, ((0, 0), (0, 0), (0, 0), (0, LANES - DH)))


def _layer(x2d, bsz, n, layer, mod4, lw, cache, row_of_tile):
    is_ctx = cache is None
    mq, mg, at, cv, rt, gt = _inproj(x2d, mod4, lw['ng'], lw['w_in'], lw['b_in'], layer, row_of_tile)

    if is_ctx:
        c0 = jnp.zeros((bsz, 2, HW, LANES), f32)
        n0 = jnp.zeros((bsz, 2, 1, HW), f32)
        m0 = jnp.zeros((bsz, 2, 1, LANES), f32)
        s0 = jnp.zeros((bsz, 2, HW, LANES), f32)
    else:
        c0 = _stack_state(cache['C'])
        n0 = cache['n'].reshape(bsz, 2, 1, HW)
        m0 = jnp.stack([jnp.pad(cache['m'][:, d], ((0, 0), (2 * N_HEADS * d, LANES - N_HEADS - 2 * N_HEADS * d)))
                        for d in range(2)], axis=1)[:, :, None, :]
        s0 = _stack_state(cache['S'])
    ym, c_fin, n_fin, m_fin = _mlstm(mq, mg, c0, n0, m0, bsz, n)
    yr, s_fin = _retention(rt, s0, lw['ret_dl'], lw['ret_dr'], bsz, n)

    at3 = at.reshape(bsz, n, A_W + 2 * A_KV_W)
    kcol = A_W // A_KV_W
    if is_ctx:
        dummy = jnp.zeros((SUBLANES, LANES), f32)
        ck_spec = pl.BlockSpec((None, n, A_KV_W), lambda b, i: (b, 0, kcol))
        cv_spec = pl.BlockSpec((None, n, A_KV_W), lambda b, i: (b, 0, kcol + 1))
        ya = _attention(at, at3, at3, ck_spec, cv_spec, dummy, dummy, lw['sink'], bsz, n, n, False)
    else:
        n_ctx = cache['k'].shape[2]
        ck_spec = pl.BlockSpec((None, None, n_ctx, A_KV_W), lambda b, i: (b, layer, 0, 0))
        cos, sin = _rope_tables(n)
        ya = _attention(at, cache['k'], cache['v'], ck_spec, ck_spec, cos, sin, lw['sink'], bsz, n, n_ctx, True)

    yc = _conv(cv, lw['conv_w'], lw['conv_b'], lw['conv_ln_g'], lw['conv_ln_b'], bsz, n)

    x2d = _post(x2d, ym, ya, yc, yr, gt, mod4, lw, layer, row_of_tile)

    ctx = None
    if is_ctx:
        ctx = (at3[:, :, A_W:A_W + A_KV_W].reshape(bsz, n, A_KV_HEADS, DH),
               at3[:, :, A_W + A_KV_W:].reshape(bsz, n, A_KV_HEADS, DH),
               c_fin.reshape(bsz, 2, N_HEADS, DH, DH),
               n_fin.reshape(bsz, 2, N_HEADS, DH),
               jnp.stack([m_fin[:, d, 0, 2 * N_HEADS * d:2 * N_HEADS * d + N_HEADS] for d in range(2)], axis=1),
               s_fin.reshape(bsz, 2, N_HEADS, DH, DH))
    return x2d, ctx


def kernel(x_prompt, x_sample, c, cache_k, cache_v, state_mlstm_C, state_mlstm_n, state_mlstm_m, state_ret,
           c_ctx, w_ada, b_ada, norm_g, w_in, b_in, w_branch, w_out, attn_sink, ret_decay_logit,
           conv_w, conv_b, conv_ln_g, conv_ln_b, w_mlp1, w_mlp2):
    bsz, seq, _ = x_prompt.shape
    dbsz, dseq, _ = x_sample.shape
    past = cache_k.shape[2]

    c_rows = jnp.zeros((SUBLANES, D_MODEL), f32).at[0].set(c_ctx).at[1:1 + dbsz].set(c)
    mod4 = _ada(c_rows, w_ada, b_ada).reshape(DEPTH, SUBLANES, 6, D_MODEL)

    w_pk, b_pk = _pack_w_in(w_in, b_in)
    layers = []
    for l in range(DEPTH):
        layers.append({
            'ng': norm_g[l], 'w_in': w_pk, 'b_in': b_pk,
            'w_branch': w_branch[l].astype(bf16), 'w_out': w_out[l].astype(bf16),
            'w_mlp1': w_mlp1[l].astype(bf16), 'w_mlp2': w_mlp2[l].astype(bf16),
            'sink': jnp.broadcast_to(attn_sink[l][:, None], (A_HEADS, LANES)),
            'ret_dl': jnp.repeat(ret_decay_logit[l], DH, axis=-1).reshape(2, 1, HW),
            'ret_dr': jnp.broadcast_to(jnp.repeat(ret_decay_logit[l], DH, axis=-1)[:, :, None], (2, HW, LANES)),
            'conv_w': conv_w[l], 'conv_b': conv_b[l][None, :],
            'conv_ln_g': conv_ln_g[l][None, :], 'conv_ln_b': conv_ln_b[l][None, :],
        })

    xp = x_prompt.reshape(bsz * seq, D_MODEL)
    ctxs = []
    for l in range(DEPTH):
        xp, ctx = _layer(xp, bsz, seq, l, mod4, layers[l], None, lambda i: 0)
        ctxs.append(ctx)

    xs = x_sample.reshape(dbsz * dseq, D_MODEL)
    tiles_per_seq = dseq // ROW_TILE
    ck = cache_k.reshape(dbsz, DEPTH, past, A_KV_W)
    cv = cache_v.reshape(dbsz, DEPTH, past, A_KV_W)
    for l in range(DEPTH):
        cache = {'k': ck, 'v': cv, 'C': state_mlstm_C[:, l], 'n': state_mlstm_n[:, l],
                 'm': state_mlstm_m[:, l], 'S': state_ret[:, l]}
        xs, _ = _layer(xs, dbsz, dseq, l, mod4, layers[l], cache, lambda i: 1 + i // tiles_per_seq)

    stack = lambda j: jnp.stack([ctxs[l][j] for l in range(DEPTH)], axis=1)
    return (xp.reshape(bsz, seq, D_MODEL), xs.reshape(dbsz, dseq, D_MODEL),
            stack(0), stack(1), stack(2), stack(3), stack(4), stack(5))
```

```python
import functools

import numpy as np
import jax
import jax.numpy as jnp
from jax import lax
from jax.experimental import pallas as pl
from jax.experimental.pallas import tpu as pltpu

f32 = jnp.float32
bf16 = jnp.bfloat16

D_MODEL = 1024
DEPTH = 2
GRID_W = 64
BLOCK = 128
N_HEADS = 4
DH = 64
HW = N_HEADS * DH
A_HEADS = 8
A_KV_HEADS = 2
A_GROUP = A_HEADS // A_KV_HEADS
A_W = A_HEADS * DH
A_KV_W = A_KV_HEADS * DH
WINDOW = 128
ROPE_BASE = 10000.0
CONV_K = 31
D_FF = 4 * D_MODEL
N_BRANCH = 4
EPS = 1e-6
NEG = -1e30
LOG2E = 1.4426950408889634
LANES = 128
SUBLANES = 8

N_GATES = 4 * N_HEADS
PK_MQ = 0
PK_MG = PK_MQ + 4 * HW
PK_AT = PK_MG + 2 * LANES
PK_CV = PK_AT + A_W + 2 * A_KV_W
PK_RT = PK_CV + 2 * HW
PK_GT = PK_RT + 4 * HW
PK_W = PK_GT + N_BRANCH * D_MODEL
BR_OFFS = (0, HW, HW + A_W, 2 * HW + A_W, 3 * HW + A_W)

VMEM_LIMIT = 56 * 1024 * 1024
ROW_TILE = 512


def _sigmoid(x):
    return 0.5 * (jnp.tanh(0.5 * x) + 1.0)


def _log_sigmoid(x):
    return jnp.minimum(x, 0.0) - jnp.log(1.0 + jnp.exp(-jnp.abs(x)))


def _dot(a, b):
    return jnp.dot(a, b, preferred_element_type=f32)


def _dot_nt(a, b):
    return lax.dot_general(a, b, (((1,), (1,)), ((), ())), preferred_element_type=f32)


def _dot2_r(t, x):
    x1 = x.astype(bf16)
    x2 = (x - x1.astype(f32)).astype(bf16)
    r = _dot(t, jnp.concatenate([x1, x2], axis=1))
    return r[:, 0:x.shape[1]] + r[:, x.shape[1]:]


def _dot2_l(x, t):
    x1 = x.astype(bf16)
    x2 = (x - x1.astype(f32)).astype(bf16)
    r = _dot(jnp.concatenate([x1, x2], axis=0), t)
    return r[0:x.shape[0]] + r[x.shape[0]:]


def _const_spec(shape):
    nd = len(shape)
    return pl.BlockSpec(shape, lambda *_: (0,) * nd, pipeline_mode=pl.Buffered(1))


def _params(n_axes):
    return pltpu.CompilerParams(dimension_semantics=("arbitrary",) * n_axes, vmem_limit_bytes=VMEM_LIMIT)


def _ada_kernel(c_ref, w_ref, b_ref, o_ref):
    c = c_ref[...]
    s = (c * _sigmoid(c)).astype(bf16)
    o_ref[...] = _dot(s, w_ref[...].astype(bf16)) + b_ref[...]


def _ada(c_rows, w_ada, b_ada):
    tn = 1536
    n_out = 6 * D_MODEL
    return pl.pallas_call(
        _ada_kernel,
        grid=(DEPTH, n_out // tn),
        in_specs=[
            pl.BlockSpec((SUBLANES, D_MODEL), lambda l, j: (0, 0)),
            pl.BlockSpec((None, D_MODEL, tn), lambda l, j: (l, 0, j)),
            pl.BlockSpec((None, 1, tn), lambda l, j: (l, 0, j)),
        ],
        out_specs=pl.BlockSpec((None, SUBLANES, tn), lambda l, j: (l, 0, j)),
        out_shape=jax.ShapeDtypeStruct((DEPTH, SUBLANES, n_out), f32),
        compiler_params=_params(2),
        name="ada",
    )(c_rows, w_ada, b_ada.reshape(DEPTH, 1, n_out))


def _inproj_kernel(x_ref, mod_ref, ng_ref, w_ref, b_ref, mq_ref, mg_ref, aq_ref, ak_ref, av_ref, cv_ref, rt_ref,
                   gt_ref):
    x = x_ref[...]
    y = x * lax.rsqrt(jnp.mean(x * x, axis=-1, keepdims=True) + EPS) * ng_ref[0:1, :]
    h = (y * (1.0 + mod_ref[1:2, :]) + mod_ref[0:1, :]).astype(bf16)
    chunk = 512
    for ref, base in ((mq_ref, PK_MQ), (mg_ref, PK_MG), (aq_ref, PK_AT), (ak_ref, PK_AT + A_W),
                      (av_ref, PK_AT + A_W + A_KV_W), (cv_ref, PK_CV), (rt_ref, PK_RT), (gt_ref, PK_GT)):
        total = ref.shape[-1]
        for c in range(0, total, chunk):
            w = min(chunk, total - c)
            r = _dot_nt(h, w_ref[base + c:base + c + w, :]) + b_ref[:, base + c:base + c + w]
            ref[:, c:c + w] = r.astype(ref.dtype)


def _inproj(x2d, mod4, ng, w_pk, b_pk, layer, row_of_tile):
    m = x2d.shape[0]
    tm = ROW_TILE
    widths = (4 * HW, 2 * LANES, A_W, A_KV_W, A_KV_W, 2 * HW, 4 * HW, N_BRANCH * D_MODEL)
    dtypes = (bf16, f32, f32, f32, f32, bf16, bf16, bf16)
    return pl.pallas_call(
        _inproj_kernel,
        grid=(m // tm,),
        in_specs=[
            pl.BlockSpec((tm, D_MODEL), lambda i: (i, 0)),
            pl.BlockSpec((None, None, 6, D_MODEL), lambda i: (layer, row_of_tile(i), 0, 0)),
            _const_spec((4, D_MODEL)),
            pl.BlockSpec((None, PK_W, D_MODEL), lambda i: (layer, 0, 0), pipeline_mode=pl.Buffered(1)),
            pl.BlockSpec((None, 1, PK_W), lambda i: (layer, 0, 0), pipeline_mode=pl.Buffered(1)),
        ],
        out_specs=[pl.BlockSpec((tm, w), lambda i: (i, 0)) for w in widths],
        out_shape=[jax.ShapeDtypeStruct((m, w), dt) for w, dt in zip(widths, dtypes)],
        compiler_params=_params(1),
        name="inproj",
    )(x2d, mod4, ng, w_pk, b_pk)


def _head_lane_masks():
    lane_head = lax.broadcasted_iota(jnp.int32, (1, HW), 1) // DH
    return [lane_head == h for h in range(N_HEADS)]


def _run_interleaved(gens):
    results = [None] * len(gens)
    active = list(range(len(gens)))
    while active:
        for idx in list(active):
            try:
                next(gens[idx])
            except StopIteration as stop:
                results[idx] = stop.value
                active.remove(idx)
    return results


PAIRS = N_HEADS // 2


def _pair_diag_mask():
    r = (lax.broadcasted_iota(jnp.int32, (HW, LANES), 0) // DH) % 2
    c = lax.broadcasted_iota(jnp.int32, (HW, LANES), 1) // DH
    return r == c


def _expand_state(x):
    return jnp.where(_pair_diag_mask(), x + pltpu.roll(x, DH, 1), 0.0)


def _compact_state(c):
    return (c + pltpu.roll(c, DH, 1))[:, 0:DH]


def _pair_qk(q, kt):
    zeros = jnp.zeros((DH, BLOCK), kt.dtype)
    outs = []
    for j in range(PAIRS):
        rows = kt[LANES * j:LANES * (j + 1)]
        blk = jnp.concatenate([jnp.concatenate([rows[0:DH], zeros], axis=0),
                               jnp.concatenate([zeros, rows[DH:2 * DH]], axis=0)], axis=1)
        outs.append(_dot(q[:, LANES * j:LANES * (j + 1)], blk))
    return jnp.concatenate(outs, axis=1)


def _pair_pv(p, v):
    low = lax.broadcasted_iota(jnp.int32, (1, LANES), 1) < DH
    outs = []
    for j in range(PAIRS):
        vp = v[:, LANES * j:LANES * (j + 1)]
        zero = jnp.zeros_like(vp)
        blk = jnp.concatenate([jnp.where(low, vp, zero), jnp.where(low, zero, vp)], axis=0)
        outs.append(_dot(p[:, 2 * BLOCK * j:2 * BLOCK * (j + 1)], blk))
    return jnp.concatenate(outs, axis=1)


def _pair_state_read(q, c):
    return jnp.concatenate([_dot(q[:, LANES * j:LANES * (j + 1)], c[LANES * j:LANES * (j + 1)])
                            for j in range(PAIRS)], axis=1)


def _pair_state_update(a_t, v):
    upd = jnp.concatenate([_dot(a_t[LANES * j:LANES * (j + 1)], v[:, LANES * j:LANES * (j + 1)])
                           for j in range(PAIRS)], axis=0)
    return jnp.where(_pair_diag_mask(), upd, 0.0)


def _mlstm_kernel(x_ref, g_ref, c0_ref, n0_ref, m0_ref, y_ref, cn_ref, nn_ref, mn_ref,
                  h_s, c_s, n_s, m_s, *, nc, nb):
    wide = N_HEADS * BLOCK
    row = lax.broadcasted_iota(jnp.int32, (BLOCK, BLOCK), 0)
    col = lax.broadcasted_iota(jnp.int32, (BLOCK, BLOCK), 1)
    tsum = (jnp.where(row >= col, 1.0, 0.0).astype(bf16), jnp.where(row <= col, 1.0, 0.0).astype(bf16))
    roww = lax.broadcasted_iota(jnp.int32, (BLOCK, wide), 0)
    colw = lax.broadcasted_iota(jnp.int32, (BLOCK, wide), 1) & (BLOCK - 1)
    tri4 = (roww >= colw, roww <= colw)
    masks = _head_lane_masks()
    lane = lax.broadcasted_iota(jnp.int32, (1, LANES), 1)
    row8 = lax.broadcasted_iota(jnp.int32, (SUBLANES, BLOCK), 0)

    def onehot(shape, row_expr, col_expr):
        r = lax.broadcasted_iota(jnp.int32, shape, 0)
        c = lax.broadcasted_iota(jnp.int32, shape, 1)
        return jnp.where(row_expr(r) == col_expr(c), 1.0, 0.0).astype(bf16)

    g0s = (0, 2 * N_HEADS)
    valid = tuple((lane >= g0) & (lane < g0 + N_HEADS) for g0 in g0s)
    hexp = tuple(onehot((LANES, HW), lambda r: r, lambda c, g0=g0: c // DH + g0) for g0 in g0s)
    hselg = tuple(onehot((HW, LANES), lambda r, g0=g0: r // DH + g0, lambda c: c) for g0 in g0s)

    chains = [(bb, d) for bb in range(nb) for d in range(2)]
    for bb, d in chains:
        c_s[bb, d] = _expand_state(c0_ref[bb, d])
        n_s[bb, d] = n0_ref[bb, d]
        m_s[bb, d] = m0_ref[bb, d]

    def load(bb, d, c):
        r0 = pl.multiple_of(c * BLOCK, BLOCK)
        return (r0, x_ref[bb, pl.ds(r0, BLOCK), 0:3 * HW], g_ref[bb, pl.ds(r0, BLOCK), :],
                c_s[bb, d], n_s[bb, d], m_s[bb, d])

    def compute(d, blk, g, cmat, nrow, m_row):
        g0 = g0s[d]
        q = blk[:, 0:HW]
        k = blk[:, HW:2 * HW] * (DH ** -0.5)
        v = blk[:, 2 * HW:3 * HW]
        ga = g[:, 0:LANES]
        gb = g[:, LANES:2 * LANES]
        bc = _dot2_r(tsum[d], _log_sigmoid(gb))
        yield
        beta = ga - bc
        beta_t = beta.T
        yield
        cmx = beta
        sh = 1
        while sh < BLOCK:
            if d == 0:
                cmx = jnp.maximum(cmx, jnp.where(row >= sh, pltpu.roll(cmx, sh, 0), NEG))
            else:
                cmx = jnp.maximum(cmx, jnp.where(row < BLOCK - sh, pltpu.roll(cmx, BLOCK - sh, 0), NEG))
            sh *= 2
        mx = jnp.maximum(m_row, cmx)
        alpha = jnp.where(valid[d], -mx, 0.0)
        w_int = jnp.where(valid[d], jnp.exp(m_row - mx), 0.0)
        e_nb = jnp.exp(alpha - bc)
        yield
        z = jnp.concatenate([jnp.broadcast_to(alpha[:, g0 + h:g0 + h + 1], (BLOCK, BLOCK)) for h in range(N_HEADS)],
                            axis=1)
        beta_w = jnp.concatenate([beta_t[g0 + h:g0 + h + 1, :] for h in range(N_HEADS)], axis=1)
        ktf = k.astype(f32).T
        s_wide = _pair_qk(q, ktf.astype(bf16))
        yield
        p = s_wide * jnp.exp(jnp.where(tri4[d], z + beta_w, NEG))
        num = _pair_pv(p.astype(bf16), v)
        yield
        den = jnp.zeros((BLOCK, LANES), f32)
        for h in range(N_HEADS):
            den = jnp.where(lane == g0 + h, jnp.sum(p[:, BLOCK * h:BLOCK * (h + 1)], axis=1, keepdims=True), den)
        qc = _pair_state_read(q, cmat.astype(bf16))
        qn = _dot(q * nrow.astype(bf16), hselg[d])
        yield
        den = den + w_int * qn
        inv = jnp.where(valid[d], 1.0 / jnp.maximum(jnp.abs(den), e_nb), 0.0)
        fac = _dot(jnp.concatenate([w_int, inv], axis=0).astype(bf16), hexp[d])
        yield
        hout = (num + fac[0:BLOCK] * qc) * fac[BLOCK:2 * BLOCK]
        last = BLOCK - 1 if d == 0 else 0
        a_last = alpha[last:last + 1, :]
        m_new = jnp.where(valid[d], bc[last:last + 1, :] - a_last, 0.0)
        dec = jnp.exp(m_row + a_last)
        wr = jnp.zeros((SUBLANES, BLOCK), f32)
        kws = []
        for h in range(N_HEADS):
            w_row = jnp.exp(beta_t[g0 + h:g0 + h + 1, :] + a_last[:, g0 + h:g0 + h + 1])
            kws.append(ktf[DH * h:DH * (h + 1), :] * w_row)
            wr = jnp.where(row8 == h, w_row, wr)
        upd = _pair_state_update(jnp.concatenate(kws, axis=0).astype(bf16), v)
        wk = _dot(wr.astype(bf16), k)
        yield
        nnew = jnp.zeros((1, HW), f32)
        cnew = []
        for h in range(N_HEADS):
            rs = slice(DH * h, DH * (h + 1))
            dh = dec[:, g0 + h:g0 + h + 1]
            cnew.append(dh * cmat[rs, :] + upd[rs, :])
            nnew = jnp.where(masks[h], dh * nrow + wk[h:h + 1, :], nnew)
        return hout, jnp.concatenate(cnew, axis=0), nnew, m_new

    def step(i, second):
        loaded = [load(bb, d, i if d == 0 else nc - 1 - i) for bb, d in chains]
        if second:
            other = [h_s[bb, pl.ds(l[0], BLOCK), :] for (bb, d), l in zip(chains, loaded)]
            ogate = [x_ref[bb, pl.ds(l[0], BLOCK), 3 * HW:4 * HW] for (bb, d), l in zip(chains, loaded)]
        outs = _run_interleaved([compute(d, *l[1:]) for (bb, d), l in zip(chains, loaded)])
        for idx, ((bb, d), l) in enumerate(zip(chains, loaded)):
            hout, cnew, nnew, m_new = outs[idx]
            if second:
                y = _sigmoid(ogate[idx].astype(f32)) * (other[idx] + hout)
                y_ref[bb, pl.ds(l[0], BLOCK), :] = y.astype(y_ref.dtype)
            else:
                h_s[bb, pl.ds(l[0], BLOCK), :] = hout
            c_s[bb, d] = cnew
            n_s[bb, d] = nnew
            m_s[bb, d] = m_new

    def first_half(i, carry):
        step(i, False)
        return carry

    def second_half(i, carry):
        step(i, True)
        return carry

    lax.fori_loop(0, nc // 2, first_half, 0)
    lax.fori_loop(nc // 2, nc, second_half, 0)

    for bb, d in chains:
        cn_ref[bb, d] = _compact_state(c_s[bb, d])
        nn_ref[bb, d] = n_s[bb, d]
        mn_ref[bb, d] = m_s[bb, d]


SCAN_SEQS = 4


def _mlstm(mq, mg, c0, n0, m0, bsz, n):
    nc = n // BLOCK
    nb = min(SCAN_SEQS, bsz)
    assert nc % 2 == 0 and bsz % nb == 0
    spec = lambda shape: pl.BlockSpec((nb,) + shape, lambda b: (b,) + (0,) * len(shape))
    seq_in = (lambda shape: pl.BlockSpec((nb,) + shape, lambda b: (b,) + (0,) * len(shape),
                                         pipeline_mode=pl.Buffered(1))) if bsz == nb else spec
    return pl.pallas_call(
        functools.partial(_mlstm_kernel, nc=nc, nb=nb),
        grid=(bsz // nb,),
        in_specs=[
            seq_in((n, 4 * HW)),
            seq_in((n, 2 * LANES)),
            spec((2, HW, LANES)),
            spec((2, 1, HW)),
            spec((2, 1, LANES)),
        ],
        out_specs=[
            spec((n, HW)),
            spec((2, HW, DH)),
            spec((2, 1, HW)),
            spec((2, 1, LANES)),
        ],
        out_shape=[
            jax.ShapeDtypeStruct((bsz, n, HW), bf16),
            jax.ShapeDtypeStruct((bsz, 2, HW, DH), f32),
            jax.ShapeDtypeStruct((bsz, 2, 1, HW), f32),
            jax.ShapeDtypeStruct((bsz, 2, 1, LANES), f32),
        ],
        scratch_shapes=[
            pltpu.VMEM((nb, n, HW), f32),
            pltpu.VMEM((nb, 2, HW, LANES), f32),
            pltpu.VMEM((nb, 2, 1, HW), f32),
            pltpu.VMEM((nb, 2, 1, LANES), f32),
        ],
        compiler_params=_params(1),
        name="mlstm",
    )(mq.reshape(bsz, n, 4 * HW), mg.reshape(bsz, n, 2 * LANES), c0, n0, m0)


def _ret_kernel(x_ref, s0_ref, dl_ref, dr_ref, y_ref, sn_ref, o_s, s_s, qd_s, kd_s, cd_s, dm_s, *, nc, nb):
    avg = jnp.where(lax.broadcasted_iota(jnp.int32, (HW, HW), 0) // DH ==
                    lax.broadcasted_iota(jnp.int32, (HW, HW), 1) // DH, 1.0 / DH, 0.0).astype(bf16)

    @pl.when(pl.program_id(0) == 0)
    def _decay_tables():
        rowf = lax.broadcasted_iota(jnp.int32, (BLOCK, HW), 0).astype(f32)
        rel = (lax.broadcasted_iota(jnp.int32, (BLOCK, BLOCK), 0) -
               lax.broadcasted_iota(jnp.int32, (BLOCK, BLOCK), 1)).astype(f32)
        for d in range(2):
            lg = _log_sigmoid(dl_ref[d])
            if d == 0:
                qd_s[d] = jnp.exp((rowf + 1.0) * lg)
                kd_s[d] = jnp.exp((BLOCK - 1.0 - rowf) * lg)
                reld = rel
            else:
                qd_s[d] = jnp.exp((BLOCK - rowf) * lg)
                kd_s[d] = jnp.exp(rowf * lg)
                reld = -rel
            cd_s[d] = jnp.exp(float(BLOCK) * _log_sigmoid(dr_ref[d]))
            for h in range(N_HEADS):
                lgh = lg[:, DH * h:DH * h + 1]
                dm_s[d, :, BLOCK * h:BLOCK * (h + 1)] = jnp.where(reld >= 0.0,
                                                                  jnp.exp(jnp.maximum(reld, 0.0) * lgh), 0.0)

    chains = [(bb, d) for bb in range(nb) for d in range(2)]
    for bb, d in chains:
        s_s[bb, d] = _expand_state(s0_ref[bb, d])

    def compute(d, blk, sm, other, gate):
        q = blk[:, 0:HW]
        k = blk[:, HW:2 * HW] * (DH ** -0.5)
        v = blk[:, 2 * HW:3 * HW]
        kf = k.astype(f32)
        p = _pair_qk(q, kf.T.astype(bf16)) * dm_s[d]
        yield
        o = _pair_pv(p.astype(bf16), v) + qd_s[d] * _pair_state_read(q, sm.astype(bf16))
        yield
        kdt = (kf * kd_s[d]).T.astype(bf16)
        snew = cd_s[d] * sm + _pair_state_update(kdt, v)
        yield
        if other is None:
            return o, snew
        o = o + other
        gate = gate.astype(f32)
        dev = o - _dot2_l(o, avg)
        yield
        var = _dot((dev * dev).astype(bf16), avg)
        yield
        return gate * _sigmoid(gate) * (dev * lax.rsqrt(var + EPS)), snew

    def step(i, second):
        r0s = [pl.multiple_of((i if d == 0 else nc - 1 - i) * BLOCK, BLOCK) for bb, d in chains]
        loaded = [(x_ref[bb, pl.ds(r0, BLOCK), 0:3 * HW], s_s[bb, d]) for (bb, d), r0 in zip(chains, r0s)]
        if second:
            other = [o_s[bb, pl.ds(r0, BLOCK), :] for (bb, d), r0 in zip(chains, r0s)]
            gates = [x_ref[bb, pl.ds(r0, BLOCK), 3 * HW:4 * HW] for (bb, d), r0 in zip(chains, r0s)]
        else:
            other = gates = [None] * len(chains)
        outs = _run_interleaved([compute(d, *l, other[idx], gates[idx])
                                 for idx, ((bb, d), l) in enumerate(zip(chains, loaded))])
        for idx, ((bb, d), r0) in enumerate(zip(chains, r0s)):
            o, snew = outs[idx]
            if second:
                y_ref[bb, pl.ds(r0, BLOCK), :] = o.astype(y_ref.dtype)
            else:
                o_s[bb, pl.ds(r0, BLOCK), :] = o
            s_s[bb, d] = snew

    def first_half(i, carry):
        step(i, False)
        return carry

    def second_half(i, carry):
        step(i, True)
        return carry

    lax.fori_loop(0, nc // 2, first_half, 0)
    lax.fori_loop(nc // 2, nc, second_half, 0)

    for bb, d in chains:
        sn_ref[bb, d] = _compact_state(s_s[bb, d])


def _retention(rt, s0, dl, dr, bsz, n):
    nc = n // BLOCK
    nb = min(SCAN_SEQS, bsz)
    assert nc % 2 == 0 and bsz % nb == 0
    spec = lambda shape: pl.BlockSpec((nb,) + shape, lambda b: (b,) + (0,) * len(shape))
    seq_in = (lambda shape: pl.BlockSpec((nb,) + shape, lambda b: (b,) + (0,) * len(shape),
                                         pipeline_mode=pl.Buffered(1))) if bsz == nb else spec
    return pl.pallas_call(
        functools.partial(_ret_kernel, nc=nc, nb=nb),
        grid=(bsz // nb,),
        in_specs=[
            seq_in((n, 4 * HW)),
            spec((2, HW, LANES)),
            _const_spec((2, 1, HW)),
            _const_spec((2, HW, LANES)),
        ],
        out_specs=[
            spec((n, HW)),
            spec((2, HW, DH)),
        ],
        out_shape=[
            jax.ShapeDtypeStruct((bsz, n, HW), bf16),
            jax.ShapeDtypeStruct((bsz, 2, HW, DH), f32),
        ],
        scratch_shapes=[
            pltpu.VMEM((nb, n, HW), f32),
            pltpu.VMEM((nb, 2, HW, LANES), f32),
            pltpu.VMEM((2, BLOCK, HW), f32),
            pltpu.VMEM((2, BLOCK, HW), f32),
            pltpu.VMEM((2, HW, LANES), f32),
            pltpu.VMEM((2, BLOCK, N_HEADS * BLOCK), f32),
        ],
        compiler_params=_params(1),
        name="retention",
    )(rt.reshape(bsz, n, 4 * HW), s0, dl, dr)


def _attn_kernel(q_ref, k_ref, v_ref, ck_ref, cv_ref, cos_ref, sin_ref, sk_ref, o_ref, kt_s, vt_s,
                 *, n_ctx, n, band, qb, qps):
    i = pl.program_id(1)
    lane = lax.broadcasted_iota(jnp.int32, (1, LANES), 1)
    lo = lane < DH
    first = (lane % (DH // 2)) < (DH // 4)
    gmasks = _head_lane_masks()
    n_cblk = n_ctx // BLOCK

    def tile4(x):
        xr = pltpu.roll(x, DH, 1)
        a2 = jnp.where(lo, x, xr).astype(bf16)
        b2 = jnp.where(lo, xr, x).astype(bf16)
        return jnp.concatenate([a2, a2], axis=1), jnp.concatenate([b2, b2], axis=1)

    def rope(x, cos, sin):
        sw = jnp.where(first, pltpu.roll(x, LANES - DH // 4, 1), pltpu.roll(x, DH // 4, 1))
        return x * cos + sw * sin

    def put_block(blk, r0, kk, vv):
        k0, k1 = tile4(kk)
        kt_s[0, pl.ds(r0, BLOCK), :] = k0
        kt_s[1, pl.ds(r0, BLOCK), :] = k1
        vt = vv.T
        for j in range(A_KV_HEADS):
            vj = vt[DH * j:DH * (j + 1)]
            vt_s[j, blk] = jnp.concatenate([vj, vj], axis=0).astype(bf16)

    @pl.when(i == 0)
    def _prepare_keys():
        for c in range(n_cblk):
            put_block(c, c * BLOCK, ck_ref[c * BLOCK:(c + 1) * BLOCK, :], cv_ref[c * BLOCK:(c + 1) * BLOCK, :])
        if band:
            nb = n // BLOCK
            for blk in (n_cblk, n_cblk + 1 + nb):
                for j in range(A_KV_HEADS):
                    kt_s[j, blk * BLOCK:(blk + 1) * BLOCK, :] = jnp.zeros((BLOCK, HW), bf16)
                    vt_s[j, blk] = jnp.zeros((BLOCK, BLOCK), bf16)

            def body(c, carry):
                r0 = pl.multiple_of(c * BLOCK, BLOCK)
                kk = rope(k_ref[pl.ds(r0, BLOCK), :], cos_ref[pl.ds(r0, BLOCK), :], sin_ref[pl.ds(r0, BLOCK), :])
                put_block(n_cblk + 1 + c, pl.multiple_of(n_ctx + BLOCK + r0, BLOCK), kk, v_ref[pl.ds(r0, BLOCK), :])
                return carry

            lax.fori_loop(0, nb, body, 0, unroll=2)

    row_lo = lax.broadcasted_iota(jnp.int32, (2 * DH, qb), 0) < DH

    def query_block(u):
        q = q_ref[qb * u:qb * (u + 1), :]
        if not band:
            return q * (DH ** -0.5 * LOG2E), None, None
        iq = i * qps + u
        q0 = pl.multiple_of(iq * BLOCK, BLOCK)
        cos = cos_ref[pl.ds(q0, BLOCK), :]
        sin = sin_ref[pl.ds(q0, BLOCK), :]
        q = jnp.concatenate([rope(q[:, LANES * t:LANES * (t + 1)], cos, sin) for t in range(A_W // LANES)], axis=1)
        kj = lax.broadcasted_iota(jnp.int32, (3 * BLOCK, BLOCK), 0)
        qi = lax.broadcasted_iota(jnp.int32, (3 * BLOCK, BLOCK), 1)
        tok = kj + (iq - 1) * BLOCK
        ok = (kj >= qi) & (kj <= qi + 2 * WINDOW) & (tok >= 0) & (tok < n)
        bias = jnp.where(ok, 0.0, NEG)
        return (q * (DH ** -0.5 * LOG2E), jnp.concatenate([bias] * A_GROUP, axis=1),
                (pl.multiple_of(n_ctx + iq * BLOCK, BLOCK), n_cblk + iq))

    def kv_head(j, q, bias4, pos):
        q256 = q[:, HW * j:HW * (j + 1)]
        qs = jnp.concatenate([jnp.where(gmasks[g], q256, 0.0) for g in range(A_GROUP)], axis=0).astype(bf16)
        sink = jnp.concatenate(
            [jnp.broadcast_to(sk_ref[A_GROUP * j + g:A_GROUP * j + g + 1, 0:1], (1, qb)) for g in range(A_GROUP)],
            axis=1) * LOG2E
        s1 = _dot_nt(kt_s[j, 0:n_ctx, :], qs)
        yield
        m = jnp.maximum(jnp.max(s1, axis=0, keepdims=True), sink)
        if band:
            s2 = _dot_nt(kt_s[j, pl.ds(pos[0], 3 * BLOCK), :], qs) + bias4
            vband = vt_s[j, pl.ds(pos[1], 3)]
            yield
            m = jnp.maximum(m, jnp.max(s2, axis=0, keepdims=True))
        p1 = jnp.exp2(s1 - m)
        den = jnp.sum(p1, axis=0, keepdims=True) + jnp.exp2(sink - m)
        vctx = jnp.concatenate([vt_s[j, c] for c in range(n_cblk)], axis=1)
        acc = _dot(vctx, p1.astype(bf16))
        yield
        if band:
            p2 = jnp.exp2(s2 - m)
            den = den + jnp.sum(p2, axis=0, keepdims=True)
            acc = acc + _dot(jnp.concatenate([vband[t] for t in range(3)], axis=1), p2.astype(bf16))
            yield
        acc = acc * (1.0 / den)
        pairs = []
        for g in range(0, A_GROUP, 2):
            pair = jnp.where(row_lo, acc[:, qb * g:qb * (g + 1)], acc[:, qb * (g + 1):qb * (g + 2)])
            pairs.append(pair.T)
        return pairs

    blocks = [query_block(u) for u in range(qps)]
    outs = _run_interleaved([kv_head(j, *blocks[u]) for u in range(qps) for j in range(A_KV_HEADS)])
    for u in range(qps):
        tiles = [t for pairs in outs[A_KV_HEADS * u:A_KV_HEADS * (u + 1)] for t in pairs]
        o_ref[qb * u:qb * (u + 1), :] = jnp.concatenate(tiles, axis=1).astype(o_ref.dtype)


def _attention(aq, ak, av, ck, cv, ck_spec, cv_spec, cos, sin, sk, bsz, n, n_ctx, band):
    qb = BLOCK if band else min(n, 2 * BLOCK)
    qps = 2 if band else 1
    nq = n // (qb * qps)
    s_tot = n_ctx + (n + 2 * BLOCK if band else 0)
    return pl.pallas_call(
        functools.partial(_attn_kernel, n_ctx=n_ctx, n=n, band=band, qb=qb, qps=qps),
        grid=(bsz, nq),
        in_specs=[
            pl.BlockSpec((None, qb * qps, A_W), lambda b, i: (b, i, 0)),
            pl.BlockSpec((None, n, A_KV_W), lambda b, i: (b, 0, 0)),
            pl.BlockSpec((None, n, A_KV_W), lambda b, i: (b, 0, 0)),
            ck_spec,
            cv_spec,
            _const_spec(cos.shape),
            _const_spec(sin.shape),
            _const_spec((A_HEADS, LANES)),
        ],
        out_specs=pl.BlockSpec((None, qb * qps, A_W), lambda b, i: (b, i, 0)),
        out_shape=jax.ShapeDtypeStruct((bsz, n, A_W), bf16),
        scratch_shapes=[
            pltpu.VMEM((A_KV_HEADS, s_tot, HW), bf16),
            pltpu.VMEM((A_KV_HEADS, s_tot // BLOCK, BLOCK, BLOCK), bf16),
        ],
        compiler_params=_params(2),
        name="attention",
    )(aq.reshape(bsz, n, A_W), ak.reshape(bsz, n, A_KV_W), av.reshape(bsz, n, A_KV_W), ck, cv, cos, sin, sk)


def _rope_tables(n):
    tok = np.arange(n)
    pos = np.stack([tok // GRID_W, tok % GRID_W], axis=1).astype(np.float32)
    quarter = DH // 4
    freqs = np.power(np.float32(ROPE_BASE), -np.arange(quarter, dtype=np.float32) / np.float32(quarter))
    lane = np.arange(LANES) % DH
    axis = lane // (DH // 2)
    fidx = lane % quarter
    sign = np.where((lane % (DH // 2)) < quarter, -1.0, 1.0)
    ang = (pos[:, axis] * freqs.astype(np.float32)[fidx][None, :]).astype(np.float32).astype(np.float64)
    return jnp.asarray(np.cos(ang), f32), jnp.asarray(np.sin(ang) * sign[None, :], f32)


CONV_PAD = 16


def _conv_kernel(u_ref, w_ref, b_ref, lg_ref, lb_ref, y_ref, z_s, *, n):
    nc = n // BLOCK
    z_s[0:CONV_PAD, :] = jnp.zeros((CONV_PAD, HW), f32)
    z_s[CONV_PAD + n:2 * CONV_PAD + n, :] = jnp.zeros((CONV_PAD, HW), f32)

    def glu(c, carry):
        r0 = pl.multiple_of(c * BLOCK, BLOCK)
        blk = u_ref[pl.ds(r0, BLOCK), :].astype(f32)
        z_s[pl.ds(pl.multiple_of(r0 + CONV_PAD, SUBLANES), BLOCK), :] = blk[:, 0:HW] * _sigmoid(blk[:, HW:2 * HW])
        return carry

    lax.fori_loop(0, nc, glu, 0)

    first = CONV_PAD - CONV_K // 2
    span = BLOCK + 2 * CONV_PAD

    def tile(r0):
        win = z_s[pl.ds(r0, span), :]
        acc = jnp.zeros((BLOCK, HW), f32)
        for r in range(SUBLANES):
            taps = [kk for kk in range(CONV_K) if (first + kk) % SUBLANES == r]
            if not taps:
                continue
            shifted = win if r == 0 else pltpu.roll(win, span - r, 0)
            for kk in taps:
                a = (first + kk) // SUBLANES * SUBLANES
                acc = acc + shifted[a:a + BLOCK, :] * w_ref[kk:kk + 1, :]
            yield
        zc = acc + b_ref[...]
        mu = jnp.mean(zc, axis=-1, keepdims=True)
        dev = zc - mu
        var = jnp.mean(dev * dev, axis=-1, keepdims=True)
        t = dev * lax.rsqrt(var + EPS) * lg_ref[...] + lb_ref[...]
        return (t * _sigmoid(t)).astype(y_ref.dtype)

    def two_tiles(c, carry):
        r0s = [pl.multiple_of((2 * c + u) * BLOCK, BLOCK) for u in range(2)]
        outs = _run_interleaved([tile(r0) for r0 in r0s])
        for r0, out in zip(r0s, outs):
            y_ref[pl.ds(r0, BLOCK), :] = out
        return carry

    lax.fori_loop(0, nc // 2, two_tiles, 0)


def _conv(cu, w, b, lg, lb, bsz, n):
    return pl.pallas_call(
        functools.partial(_conv_kernel, n=n),
        grid=(bsz,),
        in_specs=[
            pl.BlockSpec((None, n, 2 * HW), lambda bb: (bb, 0, 0)),
            _const_spec((CONV_K, HW)),
            _const_spec((1, HW)),
            _const_spec((1, HW)),
            _const_spec((1, HW)),
        ],
        out_specs=pl.BlockSpec((None, n, HW), lambda bb: (bb, 0, 0)),
        out_shape=jax.ShapeDtypeStruct((bsz, n, HW), bf16),
        scratch_shapes=[pltpu.VMEM((n + 2 * CONV_PAD, HW), f32)],
        compiler_params=_params(1),
        name="conv",
    )(cu.reshape(bsz, n, 2 * HW), w, b, lg, lb)


def _post_kernel(x_ref, ym_ref, ya_ref, yc_ref, yr_ref, gt_ref, mod_ref, ng_ref, wbr_ref, wout_ref, w1_ref, w2_ref,
                 o_ref):
    def rms(a):
        return a * lax.rsqrt(jnp.mean(a * a, axis=-1, keepdims=True) + EPS)

    acc = None
    for i, y_ref in enumerate((ym_ref, ya_ref, yc_ref, yr_ref)):
        br = _dot(y_ref[...], wbr_ref[BR_OFFS[i]:BR_OFFS[i + 1], :]).astype(bf16)
        term = jnp.tanh(gt_ref[:, D_MODEL * i:D_MODEL * (i + 1)] * 0.5) * br + br
        acc = term if acc is None else acc + term
    mix = _dot(acc * 0.5, wout_ref[...])
    x = x_ref[...] + mod_ref[2:3, :] * (rms(mix) * ng_ref[1:2, :])
    h = (rms(x) * ng_ref[2:3, :] * (1.0 + mod_ref[4:5, :]) + mod_ref[3:4, :]).astype(bf16)
    chunk = 512
    f = jnp.zeros(x.shape, f32)
    for c in range(0, D_FF, chunk):
        a = jnp.maximum(_dot(h, w1_ref[:, c:c + chunk]), 0.0)
        f = f + _dot((a * a).astype(bf16), w2_ref[c:c + chunk, :])
    o_ref[...] = x + mod_ref[5:6, :] * (rms(f) * ng_ref[3:4, :])


def _post(x2d, ym, ya, yc, yr, gt, mod4, lw, layer, row_of_tile):
    m = x2d.shape[0]
    tm = ROW_TILE
    rows = lambda w: pl.BlockSpec((tm, w), lambda i: (i, 0))
    return pl.pallas_call(
        _post_kernel,
        grid=(m // tm,),
        in_specs=[
            rows(D_MODEL), rows(HW), rows(A_W), rows(HW), rows(HW), rows(N_BRANCH * D_MODEL),
            pl.BlockSpec((None, None, 6, D_MODEL), lambda i: (layer, row_of_tile(i), 0, 0)),
            _const_spec((4, D_MODEL)),
            _const_spec((BR_OFFS[-1], D_MODEL)),
            _const_spec((D_MODEL, D_MODEL)),
            _const_spec((D_MODEL, D_FF)),
            _const_spec((D_FF, D_MODEL)),
        ],
        out_specs=rows(D_MODEL),
        out_shape=jax.ShapeDtypeStruct((m, D_MODEL), f32),
        compiler_params=_params(1),
        name="post",
    )(x2d, ym.reshape(m, HW), ya.reshape(m, A_W), yc.reshape(m, HW), yr.reshape(m, HW), gt, mod4, lw['ng'],
      lw['w_branch'], lw['w_out'], lw['w_mlp1'], lw['w_mlp2'])


PACK_ROWS = 512


def _pack_kernel(wt_ref, o_ref):
    s = pl.program_id(1)
    gate_step = PK_MG // PACK_ROWS

    @pl.when(s != gate_step)
    def _plain():
        o_ref[...] = wt_ref[0].astype(bf16)

    @pl.when(s == gate_step)
    def _gates():
        row = lax.broadcasted_iota(jnp.int32, (LANES, 1), 0)
        gates = wt_ref[0, 0:LANES, :]
        o_ref[0:LANES, :] = jnp.where(row < N_GATES, gates, 0.0).astype(bf16)
        shifted = pltpu.roll(gates, LANES - N_HEADS, 0)
        o_ref[LANES:2 * LANES, :] = jnp.where(row < N_GATES - N_HEADS, shifted, 0.0).astype(bf16)
        o_ref[2 * LANES:PACK_ROWS, :] = wt_ref[0, N_GATES:N_GATES + PACK_ROWS - 2 * LANES, :].astype(bf16)


def _pack_w_in(w_in, b_in):
    gate_step = PK_MG // PACK_ROWS
    shift = 2 * LANES - N_GATES
    assert PK_MG % PACK_ROWS == 0 and PK_W % PACK_ROWS == 0 and PACK_ROWS >= 2 * LANES

    def src_row(s):
        return pl.multiple_of(jnp.where(s <= gate_step, PACK_ROWS * s, PACK_ROWS * s - shift), N_GATES)

    w_pk = pl.pallas_call(
        _pack_kernel,
        grid=(DEPTH, PK_W // PACK_ROWS),
        in_specs=[pl.BlockSpec((pl.Element(1), pl.Element(PACK_ROWS), pl.Element(D_MODEL)),
                               lambda l, s: (l, src_row(s), 0))],
        out_specs=pl.BlockSpec((None, PACK_ROWS, D_MODEL), lambda l, s: (l, s, 0)),
        out_shape=jax.ShapeDtypeStruct((DEPTH, PK_W, D_MODEL), bf16),
        compiler_params=_params(2),
        name="pack_w_in",
    )(jnp.swapaxes(w_in, 1, 2))
    g0 = PK_MG
    pad = lambda cols: jnp.zeros((DEPTH, cols), f32)
    b_pk = jnp.concatenate(
        [b_in[:, :g0 + N_GATES], pad(LANES - N_GATES), b_in[:, g0 + N_HEADS:g0 + N_GATES],
         pad(LANES - N_GATES + N_HEADS), b_in[:, g0 + N_GATES:]], axis=-1)
    return w_pk, b_pk[:, None, :]


def _stack_state(s):
    bsz = s.shape[0]
    s = s.reshape(bsz, 2, HW, DH)
    return jnp.pad(s, ((0, 0), (0, 0), (0, 0), (0, LANES - DH)))


def _layer(x2d, bsz, n, layer, mod4, lw, cache, row_of_tile):
    is_ctx = cache is None
    mq, mg, aq, ak, av, cv, rt, gt = _inproj(x2d, mod4, lw['ng'], lw['w_in'], lw['b_in'], layer, row_of_tile)

    if is_ctx:
        c0 = jnp.zeros((bsz, 2, HW, LANES), f32)
        n0 = jnp.zeros((bsz, 2, 1, HW), f32)
        m0 = jnp.zeros((bsz, 2, 1, LANES), f32)
        s0 = jnp.zeros((bsz, 2, HW, LANES), f32)
    else:
        c0 = _stack_state(cache['C'])
        n0 = cache['n'].reshape(bsz, 2, 1, HW)
        m0 = jnp.stack([jnp.pad(cache['m'][:, d], ((0, 0), (2 * N_HEADS * d, LANES - N_HEADS - 2 * N_HEADS * d)))
                        for d in range(2)], axis=1)[:, :, None, :]
        s0 = _stack_state(cache['S'])
    ym, c_fin, n_fin, m_fin = _mlstm(mq, mg, c0, n0, m0, bsz, n)
    yr, s_fin = _retention(rt, s0, lw['ret_dl'], lw['ret_dr'], bsz, n)

    ak3 = ak.reshape(bsz, n, A_KV_W)
    av3 = av.reshape(bsz, n, A_KV_W)
    if is_ctx:
        dummy = jnp.zeros((SUBLANES, LANES), f32)
        own_spec = pl.BlockSpec((None, n, A_KV_W), lambda b, i: (b, 0, 0))
        ya = _attention(aq, ak, av, ak3, av3, own_spec, own_spec, dummy, dummy, lw['sink'], bsz, n, n, False)
    else:
        n_ctx = cache['k'].shape[2]
        ck_spec = pl.BlockSpec((None, None, n_ctx, A_KV_W), lambda b, i: (b, layer, 0, 0))
        cos, sin = _rope_tables(n)
        ya = _attention(aq, ak, av, cache['k'], cache['v'], ck_spec, ck_spec, cos, sin, lw['sink'], bsz, n, n_ctx,
                        True)

    yc = _conv(cv, lw['conv_w'], lw['conv_b'], lw['conv_ln_g'], lw['conv_ln_b'], bsz, n)

    x2d = _post(x2d, ym, ya, yc, yr, gt, mod4, lw, layer, row_of_tile)

    ctx = None
    if is_ctx:
        ctx = (ak3.reshape(bsz, n, A_KV_HEADS, DH),
               av3.reshape(bsz, n, A_KV_HEADS, DH),
               c_fin.reshape(bsz, 2, N_HEADS, DH, DH),
               n_fin.reshape(bsz, 2, N_HEADS, DH),
               jnp.stack([m_fin[:, d, 0, 2 * N_HEADS * d:2 * N_HEADS * d + N_HEADS] for d in range(2)], axis=1),
               s_fin.reshape(bsz, 2, N_HEADS, DH, DH))
    return x2d, ctx


def kernel(x_prompt, x_sample, c, cache_k, cache_v, state_mlstm_C, state_mlstm_n, state_mlstm_m, state_ret,
           c_ctx, w_ada, b_ada, norm_g, w_in, b_in, w_branch, w_out, attn_sink, ret_decay_logit,
           conv_w, conv_b, conv_ln_g, conv_ln_b, w_mlp1, w_mlp2):
    bsz, seq, _ = x_prompt.shape
    dbsz, dseq, _ = x_sample.shape
    past = cache_k.shape[2]

    c_rows = jnp.zeros((SUBLANES, D_MODEL), f32).at[0].set(c_ctx).at[1:1 + dbsz].set(c)
    mod4 = _ada(c_rows, w_ada, b_ada).reshape(DEPTH, SUBLANES, 6, D_MODEL)

    w_pk, b_pk = _pack_w_in(w_in, b_in)
    layers = []
    for l in range(DEPTH):
        layers.append({
            'ng': norm_g[l], 'w_in': w_pk, 'b_in': b_pk,
            'w_branch': w_branch[l].astype(bf16), 'w_out': w_out[l].astype(bf16),
            'w_mlp1': w_mlp1[l].astype(bf16), 'w_mlp2': w_mlp2[l].astype(bf16),
            'sink': jnp.broadcast_to(attn_sink[l][:, None], (A_HEADS, LANES)),
            'ret_dl': jnp.repeat(ret_decay_logit[l], DH, axis=-1).reshape(2, 1, HW),
            'ret_dr': jnp.broadcast_to(jnp.repeat(ret_decay_logit[l], DH, axis=-1)[:, :, None], (2, HW, LANES)),
            'conv_w': conv_w[l], 'conv_b': conv_b[l][None, :],
            'conv_ln_g': conv_ln_g[l][None, :], 'conv_ln_b': conv_ln_b[l][None, :],
        })

    xp = x_prompt.reshape(bsz * seq, D_MODEL)
    ctxs = []
    for l in range(DEPTH):
        xp, ctx = _layer(xp, bsz, seq, l, mod4, layers[l], None, lambda i: 0)
        ctxs.append(ctx)

    xs = x_sample.reshape(dbsz * dseq, D_MODEL)
    tiles_per_seq = dseq // ROW_TILE
    ck = cache_k.reshape(dbsz, DEPTH, past, A_KV_W)
    cv = cache_v.reshape(dbsz, DEPTH, past, A_KV_W)
    for l in range(DEPTH):
        cache = {'k': ck, 'v': cv, 'C': state_mlstm_C[:, l], 'n': state_mlstm_n[:, l],
                 'm': state_mlstm_m[:, l], 'S': state_ret[:, l]}
        xs, _ = _layer(xs, dbsz, dseq, l, mod4, layers[l], cache, lambda i: 1 + i // tiles_per_seq)

    stack = lambda j: jnp.stack([ctxs[l][j] for l in range(DEPTH)], axis=1)
    return (xp.reshape(bsz, seq, D_MODEL), xs.reshape(dbsz, dseq, D_MODEL),
            stack(0), stack(1), stack(2), stack(3), stack(4), stack(5))
```
